```python
import jax, jax.numpy as jnp
from jax import lax
import numpy as np

D_MODEL = 2048
BATCH = 2
SEQ = 16384
DEPTH = 2
DEC_BATCH = 16
DEC_SEQ = 32
PAST_LEN = 1024

CHUNK = 64
N_MIXERS = 2
N_FOX = (DEPTH + 1) // 2
N_SWA = DEPTH // 2
FOX_HEADS = 16
FOX_HD = D_MODEL // FOX_HEADS
FOX_QBLOCK = 128
FORGET_BIAS_INIT = 3.0
SWA_HEADS = 32
SWA_KV_HEADS = 4
SWA_GROUP = SWA_HEADS // SWA_KV_HEADS
SWA_HD = D_MODEL // SWA_HEADS
SWA_WINDOW = 128
SWA_WINDOW_CHUNKS = SWA_WINDOW // CHUNK
ROPE_THETA = 500000.0
ROPE_DIMS = SWA_HD // 4
D_FF = 4 * D_MODEL
RMS_EPS = 1e-6

kernel_name = "fox_swa_sink_hybrid_stream_step"


def rms_norm(x, g):
    x32 = x.astype(jnp.float32)
    y = x32 * lax.rsqrt(jnp.mean(x32 * x32, axis=-1, keepdims=True) + RMS_EPS)
    return y.astype(x.dtype) * g


def ada_params(c, w_mod, b_mod):
    m = jax.nn.silu(c) @ w_mod + b_mod
    return jnp.split(m[:, None, :], 6, axis=-1)


def modulate(h, shift, scale):
    return h * (1 + scale) + shift


def sq_relu_mlp(h, w_up, w_down):
    a = jax.nn.relu(h @ w_up)
    return (a * a) @ w_down


def partial_rope(x, pos):
    half = ROPE_DIMS // 2
    inv_freq = ROPE_THETA ** (-jnp.arange(half, dtype=jnp.float32) * 2.0 / ROPE_DIMS)
    ang = pos.astype(jnp.float32)[:, None] * inv_freq[None, :]
    cos = jnp.cos(ang)[:, None, :]
    sin = jnp.sin(ang)[:, None, :]
    x32 = x.astype(jnp.float32)
    x1 = x32[..., :half]
    x2 = x32[..., half:ROPE_DIMS]
    rot = jnp.concatenate([x1 * cos - x2 * sin, x2 * cos + x1 * sin], axis=-1).astype(x.dtype)
    return jnp.concatenate([rot, x[..., ROPE_DIMS:]], axis=-1)


def fox_project(h, w_in, b_f):
    b, t, _ = h.shape
    d = D_MODEL
    proj = h @ w_in
    q = proj[..., :d].reshape(b, t, FOX_HEADS, FOX_HD)
    k = proj[..., d:2 * d].reshape(b, t, FOX_HEADS, FOX_HD)
    v = proj[..., 2 * d:3 * d].reshape(b, t, FOX_HEADS, FOX_HD)
    logf = jax.nn.log_sigmoid((proj[..., 3 * d:] + b_f).astype(jnp.float32))
    return q, k, v, logf


def fox_attend(q, k, v, cq, ck, q_pos, k_pos):
    s = jnp.einsum('bqhd,bkhd->bhqk', q, k).astype(jnp.float32) * (FOX_HD ** -0.5)
    s = s + jnp.swapaxes(cq, 1, 2)[..., :, None] - jnp.swapaxes(ck, 1, 2)[..., None, :]
    s = jnp.where(k_pos[None, :] <= q_pos[:, None], s, -jnp.inf)
    p = jax.nn.softmax(s, axis=-1)
    return jnp.einsum('bhqk,bkhd->bqhd', p.astype(v.dtype), v)


def fox_prompt(q, k, v, logf):
    b, s, h, d = q.shape
    c = jnp.cumsum(logf, axis=1)
    pos = jnp.arange(s)
    nb = s // FOX_QBLOCK
    qb = q.reshape(b, nb, FOX_QBLOCK, h, d).swapaxes(0, 1)
    cb = c.reshape(b, nb, FOX_QBLOCK, h).swapaxes(0, 1)
    pb = pos.reshape(nb, FOX_QBLOCK)
    out = lax.map(lambda a: fox_attend(a[0], k, v, a[1], c, a[2], pos), (qb, cb, pb))
    return out.swapaxes(0, 1).reshape(b, s, h * d)


def fox_sample(q, k, v, logf, ck_cache, cv_cache, clogf_cache):
    b, t, h, d = q.shape
    p_len = ck_cache.shape[1]
    k_all = jnp.concatenate([ck_cache.astype(k.dtype), k], axis=1)
    v_all = jnp.concatenate([cv_cache.astype(v.dtype), v], axis=1)
    c_all = jnp.cumsum(jnp.concatenate([clogf_cache.astype(jnp.float32), logf], axis=1), axis=1)
    k_pos = jnp.arange(p_len + t)
    q_pos = p_len + jnp.arange(t)
    out = fox_attend(q, k_all, v_all, c_all[:, p_len:], c_all, q_pos, k_pos)
    return out.reshape(b, t, h * d)


def swa_project(h, w_in, pos):
    b, t, _ = h.shape
    qd = SWA_HEADS * SWA_HD
    kvd = SWA_KV_HEADS * SWA_HD
    proj = h @ w_in
    q = partial_rope(proj[..., :qd].reshape(b, t, SWA_HEADS, SWA_HD), pos)
    k = partial_rope(proj[..., qd:qd + kvd].reshape(b, t, SWA_KV_HEADS, SWA_HD), pos)
    v = proj[..., qd + kvd:].reshape(b, t, SWA_KV_HEADS, SWA_HD)
    return q, k, v


def sink_attend(q, k, v, mask, sinks):
    s = jnp.einsum('bnqhgd,bnshd->bnhgqs', q, k).astype(jnp.float32) * (SWA_HD ** -0.5)
    s = jnp.where(mask[None, :, None, None], s, -jnp.inf)
    sink = sinks.astype(jnp.float32)[None, None, :, :, None, None]
    m = jnp.maximum(jnp.max(s, axis=-1, keepdims=True), sink)
    p = jnp.exp(s - m)
    p = p / (jnp.sum(p, axis=-1, keepdims=True) + jnp.exp(sink - m))
    return jnp.einsum('bnhgqs,bnshd->bnqhgd', p.astype(v.dtype), v)


def swa_prompt(q, k, v, sinks):
    b, s, _, _ = q.shape
    nc = s // CHUNK
    w = SWA_WINDOW_CHUNKS
    qc = q.reshape(b, nc, CHUNK, SWA_KV_HEADS, SWA_GROUP, SWA_HD)
    pad = jnp.zeros((b, w, CHUNK, SWA_KV_HEADS, SWA_HD), k.dtype)
    kp = jnp.concatenate([pad, k.reshape(b, nc, CHUNK, SWA_KV_HEADS, SWA_HD)], axis=1)
    vp = jnp.concatenate([pad.astype(v.dtype), v.reshape(b, nc, CHUNK, SWA_KV_HEADS, SWA_HD)], axis=1)
    kb = jnp.concatenate([kp[:, j:j + nc] for j in range(w + 1)], axis=2)
    vb = jnp.concatenate([vp[:, j:j + nc] for j in range(w + 1)], axis=2)
    offs = jnp.repeat(jnp.arange(w + 1), CHUNK)
    kchunk = jnp.arange(nc)[:, None] - w + offs[None, :]
    mask = (kchunk >= 0)[:, None, :]
    out = sink_attend(qc, kb, vb, mask, sinks.reshape(SWA_KV_HEADS, SWA_GROUP))
    return out.reshape(b, s, D_MODEL)


def swa_sample(q, k, v, sinks, ck_cache, cv_cache, past_len):
    b, t, _, _ = q.shape
    buf = ck_cache.shape[1]
    k_all = jnp.concatenate([ck_cache.astype(k.dtype), k], axis=1)
    v_all = jnp.concatenate([cv_cache.astype(v.dtype), v], axis=1)
    q_pos = past_len + jnp.arange(t)
    k_pos = jnp.concatenate([past_len - buf + jnp.arange(buf), q_pos])
    qch = q_pos // CHUNK
    kch = k_pos // CHUNK
    mask = (kch[None, :] <= qch[:, None]) & (kch[None, :] >= qch[:, None] - SWA_WINDOW_CHUNKS)
    out = sink_attend(q.reshape(b, 1, t, SWA_KV_HEADS, SWA_GROUP, SWA_HD), k_all[:, None], v_all[:, None],
                      mask[None], sinks.reshape(SWA_KV_HEADS, SWA_GROUP))
    return out.reshape(b, t, D_MODEL), k_all[:, -buf:], v_all[:, -buf:]


def setup_inputs(seed: int = 0) -> dict:
    key = jax.random.key(seed)
    ks = jax.random.split(key, 24)
    n = jax.random.normal
    d = D_MODEL
    wbuf = min(SWA_WINDOW, PAST_LEN)
    fox_in = 3 * d + FOX_HEADS
    swa_in = SWA_HEADS * SWA_HD + 2 * SWA_KV_HEADS * SWA_HD
    return {
        "x_prompt": n(ks[0], (BATCH, SEQ, d), jnp.float32),
        "x_sample": n(ks[1], (DEC_BATCH, DEC_SEQ, d), jnp.float32),
        "c_prompt": n(ks[2], (BATCH, d), jnp.float32),
        "c_sample": n(ks[3], (DEC_BATCH, d), jnp.float32),
        "cache_fox_k": n(ks[4], (N_FOX, DEC_BATCH, PAST_LEN, FOX_HEADS, FOX_HD), jnp.float32),
        "cache_fox_v": n(ks[5], (N_FOX, DEC_BATCH, PAST_LEN, FOX_HEADS, FOX_HD), jnp.float32),
        "cache_fox_logf": jax.nn.log_sigmoid(FORGET_BIAS_INIT + n(ks[6], (N_FOX, DEC_BATCH, PAST_LEN, FOX_HEADS), jnp.float32)),
        "cache_swa_k": n(ks[7], (N_SWA, DEC_BATCH, wbuf, SWA_KV_HEADS, SWA_HD), jnp.float32),
        "cache_swa_v": n(ks[8], (N_SWA, DEC_BATCH, wbuf, SWA_KV_HEADS, SWA_HD), jnp.float32),
        "ada_w": n(ks[9], (DEPTH, d, 6 * d), jnp.float32) * (0.5 * d ** -0.5),
        "ada_b": n(ks[10], (DEPTH, 6 * d), jnp.float32) * 0.02,
        "norm_mix_g": 1.0 + 0.05 * n(ks[11], (DEPTH, d), jnp.float32),
        "norm_ffn_g": 1.0 + 0.05 * n(ks[12], (DEPTH, d), jnp.float32),
        "fox_w_in": n(ks[13], (N_FOX, d, fox_in), jnp.float32) * d ** -0.5,
        "fox_b_f": FORGET_BIAS_INIT + 0.1 * n(ks[14], (N_FOX, FOX_HEADS), jnp.float32),
        "fox_w_out": n(ks[15], (N_FOX, d, d), jnp.float32) * d ** -0.5,
        "swa_w_in": n(ks[16], (N_SWA, d, swa_in), jnp.float32) * d ** -0.5,
        "swa_sinks": 0.5 * n(ks[17], (N_SWA, SWA_HEADS), jnp.float32),
        "swa_w_out": n(ks[18], (N_SWA, d, d), jnp.float32) * d ** -0.5,
        "ffn_w_up": n(ks[19], (DEPTH, d, D_FF), jnp.float32) * d ** -0.5,
        "ffn_w_down": n(ks[20], (DEPTH, D_FF, d), jnp.float32) * D_FF ** -0.5,
        "final_g": 1.0 + 0.05 * n(ks[21], (d,), jnp.float32),
    }


def reference(x_prompt, x_sample, c_prompt, c_sample, cache_fox_k, cache_fox_v, cache_fox_logf,
              cache_swa_k, cache_swa_v, ada_w, ada_b, norm_mix_g, norm_ffn_g, fox_w_in, fox_b_f,
              fox_w_out, swa_w_in, swa_sinks, swa_w_out, ffn_w_up, ffn_w_down, final_g):
    past_len = cache_fox_k.shape[2]
    pos_p = jnp.arange(x_prompt.shape[1])
    pos_s = past_len + jnp.arange(x_sample.shape[1])
    xp, xs = x_prompt, x_sample
    fkp, fvp, flp, fks, fvs, fls = [], [], [], [], [], []
    skp, svp, sks, svs = [], [], [], []
    for i in range(DEPTH):
        mp = ada_params(c_prompt, ada_w[i], ada_b[i])
        ms = ada_params(c_sample, ada_w[i], ada_b[i])
        hp = modulate(rms_norm(xp, norm_mix_g[i]), mp[0], mp[1])
        hs = modulate(rms_norm(xs, norm_mix_g[i]), ms[0], ms[1])
        j = i // N_MIXERS
        if i % N_MIXERS == 0:
            q, k, v, lf = fox_project(hp, fox_w_in[j], fox_b_f[j])
            op = fox_prompt(q, k, v, lf) @ fox_w_out[j]
            fkp.append(k); fvp.append(v); flp.append(lf)
            q, k, v, lf = fox_project(hs, fox_w_in[j], fox_b_f[j])
            os_ = fox_sample(q, k, v, lf, cache_fox_k[j], cache_fox_v[j], cache_fox_logf[j]) @ fox_w_out[j]
            fks.append(k); fvs.append(v); fls.append(lf)
        else:
            buf = cache_swa_k.shape[2]
            q, k, v = swa_project(hp, swa_w_in[j], pos_p)
            op = swa_prompt(q, k, v, swa_sinks[j]) @ swa_w_out[j]
            skp.append(k[:, -buf:]); svp.append(v[:, -buf:])
            q, k, v = swa_project(hs, swa_w_in[j], pos_s)
            o, kb, vb = swa_sample(q, k, v, swa_sinks[j], cache_swa_k[j], cache_swa_v[j], past_len)
            os_ = o @ swa_w_out[j]
            sks.append(kb); svs.append(vb)
        xp = xp + mp[2] * op
        xs = xs + ms[2] * os_
        hp = modulate(rms_norm(xp, norm_ffn_g[i]), mp[3], mp[4])
        hs = modulate(rms_norm(xs, norm_ffn_g[i]), ms[3], ms[4])
        xp = xp + mp[5] * sq_relu_mlp(hp, ffn_w_up[i], ffn_w_down[i])
        xs = xs + ms[5] * sq_relu_mlp(hs, ffn_w_up[i], ffn_w_down[i])
    y_prompt = rms_norm(xp, final_g)
    y_sample = rms_norm(xs, final_g)
    return (y_prompt, y_sample,
            jnp.stack(fkp), jnp.stack(fvp), jnp.stack(flp),
            jnp.stack(fks), jnp.stack(fvs), jnp.stack(fls),
            jnp.stack(skp), jnp.stack(svp),
            jnp.stack(sks), jnp.stack(svs))
```

```python
import functools

import numpy as np
import jax
import jax.numpy as jnp
from jax import lax
from jax.experimental import pallas as pl
from jax.experimental.pallas import tpu as pltpu

F32 = jnp.float32
BF16 = jnp.bfloat16

RMS_EPS = 1e-6
CHUNK = 64
SWA_WINDOW_CHUNKS = 2
ROPE_THETA = 500000.0
LANES = 128
NEG = -1e30
VMEM_LIMIT_BYTES = 56 * 1024 * 1024


def _cp(n_axes):
    return pltpu.CompilerParams(dimension_semantics=("arbitrary",) * n_axes,
                                vmem_limit_bytes=VMEM_LIMIT_BYTES)


def _dot(a, b):
    return jnp.dot(a, b, preferred_element_type=F32)


def _dot_nt(a, b):
    return lax.dot_general(a, b, (((1,), (1,)), ((), ())), preferred_element_type=F32)


def _rms(x):
    return x * lax.rsqrt(jnp.mean(x * x, axis=-1, keepdims=True) + RMS_EPS)


def _norm_mod(x, g, shift, scale):
    return (_rms(x) * g) * (1.0 + scale) + shift


def _log_sigmoid(z):
    return jnp.minimum(z, 0.0) - jnp.log1p(jnp.exp(-jnp.abs(z)))


def _ada_kernel(c_ref, w_ref, b_ref, o_ref):
    c = c_ref[...]
    a = (c / (1.0 + jnp.exp(-c))).astype(BF16)
    o_ref[...] = _dot(a, w_ref[...].astype(BF16)) + b_ref[...]


def _ada(c_all, ada_w, ada_b):
    depth, d, n6 = ada_w.shape
    r = c_all.shape[0]
    tn = min(1024, n6)
    return pl.pallas_call(
        _ada_kernel,
        grid=(depth, n6 // tn),
        in_specs=[pl.BlockSpec((r, d), lambda l, n: (0, 0)),
                  pl.BlockSpec((None, d, tn), lambda l, n: (l, 0, n)),
                  pl.BlockSpec((None, 1, tn), lambda l, n: (l, 0, n))],
        out_specs=pl.BlockSpec((None, r, tn), lambda l, n: (l, 0, n)),
        out_shape=jax.ShapeDtypeStruct((depth, r, n6), F32),
        compiler_params=_cp(2), name="ada_params",
    )(c_all, ada_w, ada_b.reshape(depth, 1, n6))


def _fox_in_kernel(x_ref, g_ref, sh_ref, sc_ref, wq_ref, wk_ref, wv_ref, wf_ref, bf_ref, *rest,
                   head_major, hd, nh, q_scale):
    if head_major:
        qb_ref, kb_ref, vb_ref, k_ref, v_ref, lf_ref, h_ref = rest
    else:
        qb_ref, k_ref, v_ref, lf_ref, h_ref = rest

    @pl.when(pl.program_id(1) == 0)
    def _():
        hb = _norm_mod(x_ref[...], g_ref[...], sh_ref[...], sc_ref[...]).astype(BF16)
        h_ref[...] = hb
        z = _dot(hb, wf_ref[...])[:, :nh] + bf_ref[...]
        lf_ref[...] = _log_sigmoid(z)

    hb = h_ref[...]
    q = _dot(hb, wq_ref[...]) * q_scale
    k = _dot(hb, wk_ref[...])
    v = _dot(hb, wv_ref[...])
    k_ref[...] = k
    v_ref[...] = v
    if head_major:
        for j in range(q.shape[1] // hd):
            sl = slice(j * hd, (j + 1) * hd)
            qb_ref[j] = q[:, sl].astype(BF16)
            kb_ref[j] = k[:, sl].astype(BF16)
            vb_ref[j] = v[:, sl].astype(BF16)
    else:
        qb_ref[...] = q.astype(BF16)


def _mod_spec(mod, tpb, cols=None):
    _, rows, d = mod.shape
    if cols is None:
        return pl.BlockSpec((None, rows, d), lambda i, n: (i // tpb, 0, 0))
    return pl.BlockSpec((None, rows, cols), lambda i, n: (i // tpb, 0, n))


def _fox_in(x, g, shift, scale, w_qkv, w_f, b_f, *, tm, tpb, head_major, nh, hd):
    m, d = x.shape
    tn = min(512, d)
    nt = d // tn
    hpt = tn // hd
    nb, s = m // (tm * tpb), tm * tpb
    in_specs = [pl.BlockSpec((tm, d), lambda i, n: (i, 0)),
                pl.BlockSpec((1, d), lambda i, n: (0, 0)),
                _mod_spec(shift, tpb), _mod_spec(scale, tpb),
                pl.BlockSpec((d, tn), lambda i, n: (0, n)),
                pl.BlockSpec((d, tn), lambda i, n: (0, n + nt)),
                pl.BlockSpec((d, tn), lambda i, n: (0, n + 2 * nt)),
                pl.BlockSpec((d, LANES), lambda i, n: (0, 0)),
                pl.BlockSpec((1, nh), lambda i, n: (0, 0))]
    rm_spec = pl.BlockSpec((tm, tn), lambda i, n: (i, n))
    lf_spec = pl.BlockSpec((tm, nh), lambda i, n: (i, 0))
    if head_major:
        hm_spec = pl.BlockSpec((None, hpt, tm, hd), lambda i, n: (i // tpb, n, i % tpb, 0))
        hm_shape = jax.ShapeDtypeStruct((nb, nh, s, hd), BF16)
        out_specs = [hm_spec, hm_spec, hm_spec, rm_spec, rm_spec, lf_spec]
        out_shape = [hm_shape, hm_shape, hm_shape]
    else:
        out_specs = [rm_spec, rm_spec, rm_spec, lf_spec]
        out_shape = [jax.ShapeDtypeStruct((m, d), BF16)]
    out_shape += [jax.ShapeDtypeStruct((m, d), F32), jax.ShapeDtypeStruct((m, d), F32),
                  jax.ShapeDtypeStruct((m, nh), F32)]
    return pl.pallas_call(
        functools.partial(_fox_in_kernel, head_major=head_major, hd=hd, nh=nh, q_scale=hd ** -0.5),
        grid=(m // tm, nt), in_specs=in_specs, out_specs=out_specs, out_shape=out_shape,
        scratch_shapes=[pltpu.VMEM((tm, d), BF16)],
        compiler_params=_cp(2), name="fox_in_proj",
    )(x, g, shift, scale, w_qkv, w_qkv, w_qkv, w_f, b_f)


def _split3(x):
    x1 = x.astype(BF16)
    r1 = x - x1.astype(F32)
    x2 = r1.astype(BF16)
    x3 = (r1 - x2.astype(F32)).astype(BF16)
    return x1, x2, x3


def _cumsum_kernel(x_ref, u_ref, l_ref, o_ref):
    u = u_ref[...]
    lo = l_ref[...]
    y = sum(_dot(p, u) for p in _split3(x_ref[...]))
    tot = jnp.broadcast_to(y[:, LANES - 1:LANES], y.shape)
    o_ref[...] = y + sum(_dot(lo, p) for p in _split3(tot))


def _cumsum_rows(x, nc):
    rows = x.shape[0]
    assert LANES % nc == 0 or nc % LANES == 0
    n = max(nc, LANES)
    assert rows % n == 0
    u = jnp.asarray(np.triu(np.ones((LANES, LANES), np.float32)), BF16)
    idx = np.arange(n)
    same = (idx[:, None] // nc) == (idx[None, :] // nc)
    lo = jnp.asarray((same & (idx[None, :] < idx[:, None])).astype(np.float32), BF16)
    return pl.pallas_call(
        _cumsum_kernel, grid=(rows // n,),
        in_specs=[pl.BlockSpec((n, LANES), lambda i: (i, 0)),
                  pl.BlockSpec((LANES, LANES), lambda i: (0, 0)),
                  pl.BlockSpec((n, n), lambda i: (0, 0))],
        out_specs=pl.BlockSpec((n, LANES), lambda i: (i, 0)),
        out_shape=jax.ShapeDtypeStruct(x.shape, F32),
        compiler_params=_cp(1), name="logf_cumsum",
    )(x, u, lo)


def _fox_attn_kernel(q_ref, k_ref, v_ref, c_ref, o_ref, m_ref, l_ref, acc_ref, *, tq):
    qi = pl.program_id(2)
    q = q_ref[...]
    c0 = c_ref[qi][:, 0:1]
    m_ref[...] = jnp.full(m_ref.shape, NEG, F32)
    l_ref[...] = jnp.zeros(l_ref.shape, F32)
    acc_ref[...] = jnp.zeros(acc_ref.shape, F32)

    def block(j, mask):
        rows = pl.ds(pl.multiple_of(j * tq, tq), tq)
        s = _dot_nt(q, k_ref[rows, :]) + (c0 - c_ref[j])
        if mask is not None:
            s = jnp.where(mask, s, NEG)
        m_prev = m_ref[...]
        m_new = jnp.maximum(m_prev, jnp.max(s, axis=-1, keepdims=True))
        alpha = jnp.exp(m_prev - m_new)
        p = jnp.exp(s - m_new)
        l_ref[...] = alpha * l_ref[...] + jnp.sum(p, axis=-1, keepdims=True)
        acc_ref[...] = alpha * acc_ref[...] + _dot(p.astype(BF16), v_ref[rows, :])
        m_ref[...] = m_new

    def body(j, carry):
        block(j, None)
        return carry

    lax.fori_loop(0, qi, body, 0)
    row = lax.broadcasted_iota(jnp.int32, (tq, tq), 0)
    col = lax.broadcasted_iota(jnp.int32, (tq, tq), 1)
    block(qi, col <= row)
    o_ref[...] = (acc_ref[...] / l_ref[...]).astype(o_ref.dtype)


def _fox_attn(qh, kh, vh, c4, *, tq):
    nb, nh, s, hd = qh.shape
    nq = s // tq
    kv_spec = pl.BlockSpec((None, None, s, hd), lambda b, h, i: (b, h, 0, 0))
    return pl.pallas_call(
        functools.partial(_fox_attn_kernel, tq=tq),
        grid=(nb, nh, nq),
        in_specs=[pl.BlockSpec((None, None, tq, hd), lambda b, h, i: (b, h, i, 0)),
                  kv_spec, kv_spec,
                  pl.BlockSpec((None, nq, 1, tq), lambda b, h, i: (b * nh + h, 0, 0, 0))],
        out_specs=pl.BlockSpec((None, tq, hd), lambda b, h, i: (b, i, h)),
        out_shape=jax.ShapeDtypeStruct((nb, s, nh * hd), BF16),
        scratch_shapes=[pltpu.VMEM((tq, 1), F32), pltpu.VMEM((tq, 1), F32), pltpu.VMEM((tq, hd), F32)],
        compiler_params=_cp(3), name="fox_attention",
    )(qh, kh, vh, c4)


def _fox_sample_kernel(q_ref, kc_ref, vc_ref, kn_ref, vn_ref, c_ref, o_ref, *, hb, hd, p_len, t):
    row = lax.broadcasted_iota(jnp.int32, (t, t), 0)
    col = lax.broadcasted_iota(jnp.int32, (t, t), 1)
    for j in range(hb):
        sl = slice(j * hd, (j + 1) * hd)
        q = q_ref[:, sl]
        c = c_ref[j:j + 1, :]
        c0 = c[:, p_len:p_len + 1]
        s_c = _dot_nt(q, kc_ref[:, sl].astype(BF16)) + (c0 - c[:, :p_len])
        s_n = _dot_nt(q, kn_ref[:, sl].astype(BF16)) + (c0 - c[:, p_len:p_len + t])
        s_n = jnp.where(col <= row, s_n, NEG)
        m = jnp.maximum(jnp.max(s_c, axis=-1, keepdims=True), jnp.max(s_n, axis=-1, keepdims=True))
        p_c = jnp.exp(s_c - m)
        p_n = jnp.exp(s_n - m)
        den = jnp.sum(p_c, axis=-1, keepdims=True) + jnp.sum(p_n, axis=-1, keepdims=True)
        o = _dot(p_c.astype(BF16), vc_ref[:, sl].astype(BF16)) + _dot(p_n.astype(BF16), vn_ref[:, sl].astype(BF16))
        o_ref[:, sl] = (o / den).astype(o_ref.dtype)


def _fox_sample_attn(q, k_cache, v_cache, k_new, v_new, c_all, *, nh, hd, t):
    nb, p_len, d = k_cache.shape
    hb = min(8, nh)
    w = hb * hd
    lc = c_all.shape[-1]
    cache_spec = pl.BlockSpec((None, p_len, w), lambda b, h: (b, 0, h))
    row_spec = pl.BlockSpec((t, w), lambda b, h: (b, h))
    return pl.pallas_call(
        functools.partial(_fox_sample_kernel, hb=hb, hd=hd, p_len=p_len, t=t),
        grid=(nb, nh // hb),
        in_specs=[row_spec, cache_spec, cache_spec, row_spec, row_spec,
                  pl.BlockSpec((None, hb, lc), lambda b, h: (b, h, 0))],
        out_specs=row_spec,
        out_shape=jax.ShapeDtypeStruct((nb * t, d), BF16),
        compiler_params=_cp(2), name="fox_sample_attention",
    )(q, k_cache, v_cache, k_new, v_new, c_all)


def _proj_res_kernel(a_ref, w_ref, x_ref, gate_ref, o_ref):
    o_ref[...] = x_ref[...] + gate_ref[...] * _dot(a_ref[...], w_ref[...])


def _proj_res(a, w, x, gate, *, tm, tpb):
    m, k = a.shape
    n = w.shape[1]
    tn = min(512, n)
    return pl.pallas_call(
        _proj_res_kernel, grid=(m // tm, n // tn),
        in_specs=[pl.BlockSpec((tm, k), lambda i, j: (i, 0)),
                  pl.BlockSpec((k, tn), lambda i, j: (0, j)),
                  pl.BlockSpec((tm, tn), lambda i, j: (i, j)),
                  _mod_spec(gate, tpb, tn)],
        out_specs=pl.BlockSpec((tm, tn), lambda i, j: (i, j)),
        out_shape=jax.ShapeDtypeStruct((m, n), F32),
        compiler_params=_cp(2), name="attn_out_proj",
    )(a, w, x, gate)


def _ffn_kernel(x_ref, g_ref, sh_ref, sc_ref, gate_ref, wu_ref, wd_ref, *rest, final):
    if final:
        fg_ref, o_ref, h_ref, acc_ref = rest
    else:
        o_ref, h_ref, acc_ref = rest
    f = pl.program_id(1)

    @pl.when(f == 0)
    def _():
        h_ref[...] = _norm_mod(x_ref[...], g_ref[...], sh_ref[...], sc_ref[...]).astype(BF16)
        acc_ref[...] = jnp.zeros(acc_ref.shape, F32)

    a = jnp.maximum(_dot(h_ref[...], wu_ref[...]), 0.0)
    acc_ref[...] += _dot((a * a).astype(BF16), wd_ref[...])

    @pl.when(f == pl.num_programs(1) - 1)
    def _():
        y = x_ref[...] + gate_ref[...] * acc_ref[...]
        if final:
            y = _rms(y) * fg_ref[...]
        o_ref[...] = y


def _ffn(x, g, shift, scale, gate, w_up, w_down, final_g, *, tm, tpb):
    m, d = x.shape
    ff = w_up.shape[1]
    tf = min(512, ff)
    vec = pl.BlockSpec((1, d), lambda i, f: (0, 0))
    in_specs = [pl.BlockSpec((tm, d), lambda i, f: (i, 0)), vec,
                _mod_spec(shift, tpb), _mod_spec(scale, tpb), _mod_spec(gate, tpb),
                pl.BlockSpec((d, tf), lambda i, f: (0, f)),
                pl.BlockSpec((tf, d), lambda i, f: (f, 0))]
    args = [x, g, shift, scale, gate, w_up, w_down]
    if final_g is not None:
        in_specs.append(vec)
        args.append(final_g)
    return pl.pallas_call(
        functools.partial(_ffn_kernel, final=final_g is not None),
        grid=(m // tm, ff // tf), in_specs=in_specs,
        out_specs=pl.BlockSpec((tm, d), lambda i, f: (i, 0)),
        out_shape=jax.ShapeDtypeStruct((m, d), F32),
        scratch_shapes=[pltpu.VMEM((tm, d), BF16), pltpu.VMEM((tm, d), F32)],
        compiler_params=_cp(2), name="ffn_final" if final_g is not None else "ffn",
    )(*args)


def _rope(r, a, b, c):
    half_shift = (LANES // 8) // 2
    cols = []
    for j in range(r.shape[1] // LANES):
        x = r[:, j * LANES:(j + 1) * LANES]
        cols.append(x * a + pltpu.roll(x, LANES - half_shift, 1) * b + pltpu.roll(x, half_shift, 1) * c)
    return jnp.concatenate(cols, axis=1)


def _dup_halves(x, hd):
    lo = lax.broadcasted_iota(jnp.int32, (x.shape[0], LANES), 1) < hd
    zero = jnp.zeros((x.shape[0], LANES), x.dtype)
    out = []
    for j in range(x.shape[1] // LANES):
        p = x[:, j * LANES:(j + 1) * LANES]
        r = pltpu.roll(p, hd, 1)
        out += [jnp.where(lo, p, zero), jnp.where(lo, zero, r), jnp.where(lo, r, zero), jnp.where(lo, zero, p)]
    return jnp.concatenate(out, axis=1)


def _swa_in_kernel(x_ref, g_ref, sh_ref, sc_ref, w_ref, ra_ref, rb_ref, rc_ref,
                   q_ref, kk_ref, vv_ref, k_ref, v_ref, h_ref, *, nqt, kvd, hd, q_scale):
    n = pl.program_id(1)

    @pl.when(n == 0)
    def _():
        h_ref[...] = _norm_mod(x_ref[...], g_ref[...], sh_ref[...], sc_ref[...]).astype(BF16)

    r = _dot(h_ref[...], w_ref[...])

    @pl.when(n < nqt)
    def _():
        q_ref[...] = (_rope(r, ra_ref[...], rb_ref[...], rc_ref[...]) * q_scale).astype(BF16)

    @pl.when(n == nqt)
    def _():
        k = _rope(r[:, :kvd], ra_ref[...], rb_ref[...], rc_ref[...])
        v = r[:, kvd:]
        k_ref[...] = k
        v_ref[...] = v
        kk_ref[...] = _dup_halves(k, hd).astype(BF16)
        vv_ref[...] = _dup_halves(v, hd).astype(BF16)


def _swa_in(x, g, shift, scale, w, tabs, *, tm, tpb, kvd, hd):
    m, d = x.shape
    tn = 2 * kvd
    nqt = d // tn
    ntab = tabs[0].shape[0] // tm
    kkw = (kvd // hd) * 2 * LANES
    tab_spec = pl.BlockSpec((tm, LANES), lambda i, n: (i % ntab, 0))
    const = lambda w_: pl.BlockSpec((tm, w_), lambda i, n: (i, 0))
    return pl.pallas_call(
        functools.partial(_swa_in_kernel, nqt=nqt, kvd=kvd, hd=hd, q_scale=hd ** -0.5),
        grid=(m // tm, nqt + 1),
        in_specs=[pl.BlockSpec((tm, d), lambda i, n: (i, 0)),
                  pl.BlockSpec((1, d), lambda i, n: (0, 0)),
                  _mod_spec(shift, tpb), _mod_spec(scale, tpb),
                  pl.BlockSpec((d, tn), lambda i, n: (0, n)),
                  tab_spec, tab_spec, tab_spec],
        out_specs=[pl.BlockSpec((tm, tn), lambda i, n: (i, jnp.minimum(n, nqt - 1))),
                   const(kkw), const(kkw), const(kvd), const(kvd)],
        out_shape=[jax.ShapeDtypeStruct((m, d), BF16),
                   jax.ShapeDtypeStruct((m, kkw), BF16), jax.ShapeDtypeStruct((m, kkw), BF16),
                   jax.ShapeDtypeStruct((m, kvd), F32), jax.ShapeDtypeStruct((m, kvd), F32)],
        scratch_shapes=[pltpu.VMEM((tm, d), BF16)],
        compiler_params=_cp(2), name="swa_in_proj",
    )(x, g, shift, scale, w, *tabs)


def _rope_tables(pos, hd, rope_dims):
    half = rope_dims // 2
    inv_freq = ROPE_THETA ** (-jnp.arange(half, dtype=F32) * 2.0 / rope_dims)
    ang = pos.astype(F32)[:, None] * inv_freq[None, :]
    cos, sin = jnp.cos(ang), jnp.sin(ang)
    lane = np.arange(LANES) % hd
    idx = lane % half
    a = jnp.where(lane < rope_dims, cos[:, idx], 1.0)
    b = jnp.where(lane < half, -sin[:, idx], 0.0)
    c = jnp.where((lane >= half) & (lane < rope_dims), sin[:, idx], 0.0)
    return a, b, c


def _swa_core(q, k_top, k_bot, v_top, v_bot, bias, sinks, *, group):
    t = q.shape[0]
    npair = group // 2
    win = k_top.shape[0]
    qs = jnp.concatenate([q[:, p * LANES:(p + 1) * LANES] for p in range(npair)], axis=0)
    s = _dot_nt(qs, jnp.concatenate([k_top, k_bot], axis=0)) + bias
    ps, invs = [], []
    for half in range(2):
        sh = s[:, half * win:(half + 1) * win]
        sink = sinks[:, half:half + 1]
        m = jnp.maximum(jnp.max(sh, axis=-1, keepdims=True), sink)
        p = jnp.exp(sh - m)
        invs.append(1.0 / (jnp.sum(p, axis=-1, keepdims=True) + jnp.exp(sink - m)))
        ps.append(p.astype(BF16))
    o = _dot(jnp.concatenate(ps, axis=1), jnp.concatenate([v_top, v_bot], axis=0))
    lane = lax.broadcasted_iota(jnp.int32, o.shape, 1)
    o = o * jnp.where(lane < LANES // 2, invs[0], invs[1])
    return jnp.concatenate([o[p * t:(p + 1) * t] for p in range(npair)], axis=1)


def _swa_attn_kernel(q_ref, kh_ref, km_ref, vh_ref, vm_ref, bias_ref, sink_ref, o_ref, *, group):
    cat = lambda a, b, sl: jnp.concatenate([a[:, sl], b[:, sl]], axis=0)
    top, bot = slice(0, LANES), slice(LANES, 2 * LANES)
    o = _swa_core(q_ref[...], cat(kh_ref, km_ref, top), cat(kh_ref, km_ref, bot),
                  cat(vh_ref, vm_ref, top), cat(vh_ref, vm_ref, bot),
                  bias_ref[...], sink_ref[...], group=group)
    o_ref[...] = o.astype(o_ref.dtype)


def _swa_attn(q, kk, vv, bias2, sink_rows, *, nb, s, kv, group, hd):
    t = 2 * CHUNK
    nt = s // t
    gw = group * hd
    main = pl.BlockSpec((t, 2 * LANES), lambda b, i, g: (b * nt + i, g))
    halo = pl.BlockSpec((t, 2 * LANES), lambda b, i, g: (b * nt + jnp.maximum(i - 1, 0), g))
    return pl.pallas_call(
        functools.partial(_swa_attn_kernel, group=group),
        grid=(nb, nt, kv),
        in_specs=[pl.BlockSpec((t, gw), lambda b, i, g: (b * nt + i, g)),
                  halo, main, halo, main,
                  pl.BlockSpec((None,) + bias2.shape[1:], lambda b, i, g: (jnp.minimum(i, 1), 0, 0)),
                  pl.BlockSpec((None,) + sink_rows.shape[1:], lambda b, i, g: (g, 0, 0))],
        out_specs=pl.BlockSpec((t, gw), lambda b, i, g: (b * nt + i, g)),
        out_shape=jax.ShapeDtypeStruct(q.shape, BF16),
        compiler_params=_cp(3), name="swa_attention",
    )(q, kk, kk, vv, vv, bias2, sink_rows)


def _swa_sample_kernel(q_ref, kc_ref, vc_ref, kkn_ref, vvn_ref, bias_ref, sink_ref, o_ref, *, kv, group, hd, t):
    gw = group * hd
    kkc = _dup_halves(kc_ref[...], hd).astype(BF16)
    vvc = _dup_halves(vc_ref[...], hd).astype(BF16)
    pad = jnp.zeros((kc_ref.shape[0] - t, LANES), BF16)
    for g in range(kv):
        def win(cache, new, off):
            sl = slice(g * 2 * LANES + off, g * 2 * LANES + off + LANES)
            return jnp.concatenate([cache[:, sl], new[:, sl], pad], axis=0)
        o = _swa_core(q_ref[:, g * gw:(g + 1) * gw],
                      win(kkc, kkn_ref, 0), win(kkc, kkn_ref, LANES),
                      win(vvc, vvn_ref, 0), win(vvc, vvn_ref, LANES),
                      bias_ref[...], sink_ref[g], group=group)
        o_ref[:, g * gw:(g + 1) * gw] = o.astype(o_ref.dtype)


def _swa_sample_attn(q, k_cache, v_cache, kk_new, vv_new, bias, sink_rows, *, kv, group, hd, t):
    nb, buf, kvd = k_cache.shape
    d = q.shape[1]
    full = lambda a: pl.BlockSpec(a.shape, lambda b: (0,) * a.ndim)
    cache_spec = pl.BlockSpec((None, buf, kvd), lambda b: (b, 0, 0))
    new_spec = pl.BlockSpec((t, kk_new.shape[1]), lambda b: (b, 0))
    return pl.pallas_call(
        functools.partial(_swa_sample_kernel, kv=kv, group=group, hd=hd, t=t),
        grid=(nb,),
        in_specs=[pl.BlockSpec((t, d), lambda b: (b, 0)), cache_spec, cache_spec, new_spec, new_spec,
                  full(bias), full(sink_rows)],
        out_specs=pl.BlockSpec((t, d), lambda b: (b, 0)),
        out_shape=jax.ShapeDtypeStruct(q.shape, BF16),
        compiler_params=_cp(1), name="swa_sample_attention",
    )(q, k_cache, v_cache, kk_new, vv_new, bias, sink_rows)


def _window_bias(valid, npair):
    b = np.where(valid, 0.0, NEG).astype(np.float32)
    return np.tile(b, (npair, 2))


def _prompt_bias(npair):
    t = 2 * CHUNK
    qc = np.arange(t)[:, None] // CHUNK
    kc = np.arange(t + SWA_WINDOW_CHUNKS * CHUNK)[None, :] // CHUNK
    valid = (kc >= qc) & (kc <= qc + SWA_WINDOW_CHUNKS)
    first = valid & (kc >= SWA_WINDOW_CHUNKS)
    return jnp.asarray(np.stack([_window_bias(first, npair), _window_bias(valid, npair)]))


def _sample_bias(past_len, buf, t, npair):
    q_pos = past_len + np.arange(t)
    k_pos = np.concatenate([past_len - buf + np.arange(buf), q_pos])
    qch, kch = q_pos // CHUNK, k_pos // CHUNK
    valid = np.zeros((t, 2 * buf), bool)
    valid[:, :buf + t] = (kch[None, :] <= qch[:, None]) & (kch[None, :] >= qch[:, None] - SWA_WINDOW_CHUNKS)
    return jnp.asarray(_window_bias(valid, npair))


def _sink_rows(sinks, kv, group, t):
    s = sinks.reshape(kv, group // 2, 1, 2)
    return jnp.broadcast_to(s, (kv, group // 2, t, 2)).reshape(kv, (group // 2) * t, 2)


def kernel(x_prompt, x_sample, c_prompt, c_sample, cache_fox_k, cache_fox_v, cache_fox_logf, cache_swa_k,
           cache_swa_v, ada_w, ada_b, norm_mix_g, norm_ffn_g, fox_w_in, fox_b_f, fox_w_out, swa_w_in,
           swa_sinks, swa_w_out, ffn_w_up, ffn_w_down, final_g):
    bp, s, d = x_prompt.shape
    bs, t, _ = x_sample.shape
    depth = ada_w.shape[0]
    past_len = cache_fox_k.shape[2]
    nh_fox, hd_fox = cache_fox_k.shape[3], cache_fox_k.shape[4]
    buf, kv, hd_swa = cache_swa_k.shape[2], cache_swa_k.shape[3], cache_swa_k.shape[4]
    nh_swa = swa_sinks.shape[1]
    group = nh_swa // kv
    kvd = kv * hd_swa
    rope_dims = hd_swa // 4
    assert hd_fox == LANES and hd_swa == LANES // 2 and rope_dims == 16 and group % 2 == 0
    assert buf == SWA_WINDOW_CHUNKS * CHUNK and t <= buf and s % (2 * CHUNK) == 0

    mp, ms = bp * s, bs * t
    tm_p = min(512, s)
    tpb_p = s // tm_p
    tq = min(512, s)

    mods = _ada(jnp.concatenate([c_prompt, c_sample], axis=0), ada_w, ada_b)

    def split_mods(i):
        six = jnp.split(mods[i], 6, axis=-1)
        prompt = [m[:bp, None, :] for m in six]
        sample = [jnp.repeat(m[bp:], t, axis=0)[None] for m in six]
        return prompt, sample

    row = lambda v: v.reshape(1, -1)
    xp = x_prompt.reshape(mp, d)
    xs = x_sample.reshape(ms, d)
    pos_p = jnp.arange(s)
    pos_s = past_len + jnp.arange(t)
    tabs_p = _rope_tables(pos_p, hd_swa, rope_dims)
    tabs_s = tuple(jnp.tile(a, (bs, 1)) for a in _rope_tables(pos_s, hd_swa, rope_dims))
    bias_p = _prompt_bias(group // 2)
    bias_s = _sample_bias(past_len, buf, t, group // 2)

    outs = {k: [] for k in ("fkp", "fvp", "flp", "fks", "fvs", "fls", "skp", "svp", "sks", "svs")}
    for i in range(depth):
        mod_p, mod_s = split_mods(i)
        j = i // 2
        if i % 2 == 0:
            w_in = fox_w_in[j]
            w_qkv = w_in[:, :3 * d].astype(BF16)
            w_f = jnp.pad(w_in[:, 3 * d:], ((0, 0), (0, LANES - nh_fox))).astype(BF16)
            b_f = row(fox_b_f[j])
            w_out = fox_w_out[j].astype(BF16)
            qh, kh, vh, k32, v32, lf = _fox_in(xp, row(norm_mix_g[i]), mod_p[0], mod_p[1], w_qkv, w_f, b_f,
                                               tm=tm_p, tpb=tpb_p, head_major=True, nh=nh_fox, hd=hd_fox)
            lft = lf.reshape(bp, s, nh_fox).transpose(0, 2, 1)
            c = _cumsum_rows(lft.reshape(-1, LANES), s // LANES)
            op = _fox_attn(qh, kh, vh, c.reshape(bp * nh_fox, s // tq, 1, tq), tq=tq)
            xp = _proj_res(op.reshape(mp, d), w_out, xp, mod_p[2], tm=tm_p, tpb=tpb_p)
            outs["fkp"].append(k32.reshape(bp, s, nh_fox, hd_fox))
            outs["fvp"].append(v32.reshape(bp, s, nh_fox, hd_fox))
            outs["flp"].append(lf.reshape(bp, s, nh_fox))
            qs_, k32s, v32s, lfs = _fox_in(xs, row(norm_mix_g[i]), mod_s[0], mod_s[1], w_qkv, w_f, b_f,
                                           tm=ms, tpb=1, head_major=False, nh=nh_fox, hd=hd_fox)
            lf_all = jnp.concatenate([cache_fox_logf[j], lfs.reshape(bs, t, nh_fox)], axis=1)
            lc = LANES * int(2 ** np.ceil(np.log2(-(-(past_len + t) // LANES))))
            lf_all = jnp.pad(lf_all.transpose(0, 2, 1), ((0, 0), (0, 0), (0, lc - past_len - t)))
            c_s = _cumsum_rows(lf_all.reshape(-1, LANES), lc // LANES).reshape(bs, nh_fox, lc)
            os_ = _fox_sample_attn(qs_, cache_fox_k[j].reshape(bs, past_len, d),
                                   cache_fox_v[j].reshape(bs, past_len, d), k32s, v32s, c_s,
                                   nh=nh_fox, hd=hd_fox, t=t)
            xs = _proj_res(os_, w_out, xs, mod_s[2], tm=ms, tpb=1)
            outs["fks"].append(k32s.reshape(bs, t, nh_fox, hd_fox))
            outs["fvs"].append(v32s.reshape(bs, t, nh_fox, hd_fox))
            outs["fls"].append(lfs.reshape(bs, t, nh_fox))
        else:
            w_in = swa_w_in[j].astype(BF16)
            w_out = swa_w_out[j].astype(BF16)
            q, kk, vv, k32, v32 = _swa_in(xp, row(norm_mix_g[i]), mod_p[0], mod_p[1], w_in, tabs_p,
                                          tm=tm_p, tpb=tpb_p, kvd=kvd, hd=hd_swa)
            op = _swa_attn(q, kk, vv, bias_p, _sink_rows(swa_sinks[j], kv, group, 2 * CHUNK),
                           nb=bp, s=s, kv=kv, group=group, hd=hd_swa)
            xp = _proj_res(op, w_out, xp, mod_p[2], tm=tm_p, tpb=tpb_p)
            outs["skp"].append(k32.reshape(bp, s, kv, hd_swa)[:, -buf:])
            outs["svp"].append(v32.reshape(bp, s, kv, hd_swa)[:, -buf:])
            q, kk, vv, k32, v32 = _swa_in(xs, row(norm_mix_g[i]), mod_s[0], mod_s[1], w_in, tabs_s,
                                          tm=ms, tpb=1, kvd=kvd, hd=hd_swa)
            os_ = _swa_sample_attn(q, cache_swa_k[j].reshape(bs, buf, kvd), cache_swa_v[j].reshape(bs, buf, kvd),
                                   kk, vv, bias_s, _sink_rows(swa_sinks[j], kv, group, t),
                                   kv=kv, group=group, hd=hd_swa, t=t)
            xs = _proj_res(os_, w_out, xs, mod_s[2], tm=ms, tpb=1)
            k_all = jnp.concatenate([cache_swa_k[j], k32.reshape(bs, t, kv, hd_swa)], axis=1)
            v_all = jnp.concatenate([cache_swa_v[j], v32.reshape(bs, t, kv, hd_swa)], axis=1)
            outs["sks"].append(k_all[:, -buf:])
            outs["svs"].append(v_all[:, -buf:])
        fg = row(final_g) if i == depth - 1 else None
        w_up = ffn_w_up[i].astype(BF16)
        w_down = ffn_w_down[i].astype(BF16)
        xp = _ffn(xp, row(norm_ffn_g[i]), mod_p[3], mod_p[4], mod_p[5], w_up, w_down, fg, tm=tm_p, tpb=tpb_p)
        xs = _ffn(xs, row(norm_ffn_g[i]), mod_s[3], mod_s[4], mod_s[5], w_up, w_down, fg, tm=ms, tpb=1)

    st = lambda k: jnp.stack(outs[k])
    return (xp.reshape(bp, s, d), xs.reshape(bs, t, d),
            st("fkp"), st("fvp"), st("flp"), st("fks"), st("fvs"), st("fls"),
            st("skp"), st("svp"), st("sks"), st("svs"))
```

```python
import functools

import numpy as np
import jax
import jax.numpy as jnp
from jax import lax
from jax.experimental import pallas as pl
from jax.experimental.pallas import tpu as pltpu

F32 = jnp.float32
BF16 = jnp.bfloat16

RMS_EPS = 1e-6
CHUNK = 64
SWA_WINDOW_CHUNKS = 2
ROPE_THETA = 500000.0
LANES = 128
NEG = -1e30
NEG_BLOCK = -3e38
VMEM_LIMIT_BYTES = 56 * 1024 * 1024
LOG2E = 1.4426950408889634
FOX_SKIP_LOG2 = 150.0
FOX_TK = 512
FOX_NSUB = 4


def _cp(n_axes):
    return pltpu.CompilerParams(dimension_semantics=("arbitrary",) * n_axes,
                                vmem_limit_bytes=VMEM_LIMIT_BYTES)


def _dot(a, b):
    return jnp.dot(a, b, preferred_element_type=F32)


def _dot_nt(a, b):
    return lax.dot_general(a, b, (((1,), (1,)), ((), ())), preferred_element_type=F32)


def _rms(x):
    return x * lax.rsqrt(jnp.mean(x * x, axis=-1, keepdims=True) + RMS_EPS)


def _norm_mod(x, g, shift, scale):
    return (_rms(x) * g) * (1.0 + scale) + shift


def _log_sigmoid(z):
    return jnp.minimum(z, 0.0) - jnp.log1p(jnp.exp(-jnp.abs(z)))


def _ada_kernel(c_ref, w_ref, b_ref, o_ref):
    c = c_ref[...]
    a = (c / (1.0 + jnp.exp(-c))).astype(BF16)
    o_ref[...] = _dot(a, w_ref[...].astype(BF16)) + b_ref[...]


def _ada(c_all, ada_w, ada_b):
    depth, d, n6 = ada_w.shape
    r = c_all.shape[0]
    tn = min(1024, n6)
    return pl.pallas_call(
        _ada_kernel,
        grid=(depth, n6 // tn),
        in_specs=[pl.BlockSpec((r, d), lambda l, n: (0, 0)),
                  pl.BlockSpec((None, d, tn), lambda l, n: (l, 0, n)),
                  pl.BlockSpec((None, 1, tn), lambda l, n: (l, 0, n))],
        out_specs=pl.BlockSpec((None, r, tn), lambda l, n: (l, 0, n)),
        out_shape=jax.ShapeDtypeStruct((depth, r, n6), F32),
        compiler_params=_cp(2), name="ada_params",
    )(c_all, ada_w, ada_b.reshape(depth, 1, n6))


def _fox_in_kernel(x_ref, g_ref, sh_ref, sc_ref, wq_ref, wk_ref, wv_ref, wf_ref, bf_ref, *rest,
                   head_major, hd, nh, q_scale):
    if head_major:
        qb_ref, kb_ref, vb_ref, k_ref, v_ref, lf_ref, h_ref = rest
    else:
        qb_ref, k_ref, v_ref, lf_ref, h_ref = rest

    @pl.when(pl.program_id(1) == 0)
    def _():
        hb = _norm_mod(x_ref[...], g_ref[...], sh_ref[...], sc_ref[...]).astype(BF16)
        h_ref[...] = hb
        z = _dot(hb, wf_ref[...])[:, :nh] + bf_ref[...]
        lf_ref[...] = _log_sigmoid(z)

    hb = h_ref[...]
    q = _dot(hb, wq_ref[...]) * q_scale
    k = _dot(hb, wk_ref[...])
    v = _dot(hb, wv_ref[...])
    k_ref[...] = k
    v_ref[...] = v
    if head_major:
        for j in range(q.shape[1] // hd):
            sl = slice(j * hd, (j + 1) * hd)
            qb_ref[j] = q[:, sl].astype(BF16)
            kb_ref[j] = k[:, sl].astype(BF16)
            vb_ref[j] = v[:, sl].astype(BF16)
    else:
        qb_ref[...] = q.astype(BF16)


def _mod_spec(mod, tpb, cols=None):
    _, rows, d = mod.shape
    if cols is None:
        return pl.BlockSpec((None, rows, d), lambda i, n: (i // tpb, 0, 0))
    return pl.BlockSpec((None, rows, cols), lambda i, n: (i // tpb, 0, n))


def _fox_in(x, g, shift, scale, w_qkv, w_f, b_f, *, tm, tpb, head_major, nh, hd):
    m, d = x.shape
    tn = min(512, d)
    nt = d // tn
    hpt = tn // hd
    nb, s = m // (tm * tpb), tm * tpb
    in_specs = [pl.BlockSpec((tm, d), lambda i, n: (i, 0)),
                pl.BlockSpec((1, d), lambda i, n: (0, 0)),
                _mod_spec(shift, tpb), _mod_spec(scale, tpb),
                pl.BlockSpec((d, tn), lambda i, n: (0, n)),
                pl.BlockSpec((d, tn), lambda i, n: (0, n + nt)),
                pl.BlockSpec((d, tn), lambda i, n: (0, n + 2 * nt)),
                pl.BlockSpec((d, LANES), lambda i, n: (0, 0)),
                pl.BlockSpec((1, nh), lambda i, n: (0, 0))]
    rm_spec = pl.BlockSpec((tm, tn), lambda i, n: (i, n))
    lf_spec = pl.BlockSpec((tm, nh), lambda i, n: (i, 0))
    if head_major:
        hm_spec = pl.BlockSpec((None, hpt, tm, hd), lambda i, n: (i // tpb, n, i % tpb, 0))
        hm_shape = jax.ShapeDtypeStruct((nb, nh, s, hd), BF16)
        out_specs = [hm_spec, hm_spec, hm_spec, rm_spec, rm_spec, lf_spec]
        out_shape = [hm_shape, hm_shape, hm_shape]
    else:
        out_specs = [rm_spec, rm_spec, rm_spec, lf_spec]
        out_shape = [jax.ShapeDtypeStruct((m, d), BF16)]
    out_shape += [jax.ShapeDtypeStruct((m, d), F32), jax.ShapeDtypeStruct((m, d), F32),
                  jax.ShapeDtypeStruct((m, nh), F32)]
    return pl.pallas_call(
        functools.partial(_fox_in_kernel, head_major=head_major, hd=hd, nh=nh, q_scale=hd ** -0.5 * LOG2E),
        grid=(m // tm, nt), in_specs=in_specs, out_specs=out_specs, out_shape=out_shape,
        scratch_shapes=[pltpu.VMEM((tm, d), BF16)],
        compiler_params=_cp(2), name="fox_in_proj",
    )(x, g, shift, scale, w_qkv, w_qkv, w_qkv, w_f, b_f)


def _split3(x):
    x1 = x.astype(BF16)
    r1 = x - x1.astype(F32)
    x2 = r1.astype(BF16)
    x3 = (r1 - x2.astype(F32)).astype(BF16)
    return x1, x2, x3


def _cumsum_kernel(x_ref, u_ref, l_ref, o_ref):
    u = u_ref[...]
    lo = l_ref[...]
    y = sum(_dot(p, u) for p in _split3(x_ref[...]))
    tot = jnp.broadcast_to(y[:, LANES - 1:LANES], y.shape)
    o_ref[...] = (y + sum(_dot(lo, p) for p in _split3(tot))) * LOG2E


def _cumsum_rows(x, nc):
    rows = x.shape[0]
    assert LANES % nc == 0 or nc % LANES == 0
    n = max(nc, LANES)
    assert rows % n == 0
    u = jnp.asarray(np.triu(np.ones((LANES, LANES), np.float32)), BF16)
    idx = np.arange(n)
    same = (idx[:, None] // nc) == (idx[None, :] // nc)
    lo = jnp.asarray((same & (idx[None, :] < idx[:, None])).astype(np.float32), BF16)
    return pl.pallas_call(
        _cumsum_kernel, grid=(rows // n,),
        in_specs=[pl.BlockSpec((n, LANES), lambda i: (i, 0)),
                  pl.BlockSpec((LANES, LANES), lambda i: (0, 0)),
                  pl.BlockSpec((n, n), lambda i: (0, 0))],
        out_specs=pl.BlockSpec((n, LANES), lambda i: (i, 0)),
        out_shape=jax.ShapeDtypeStruct(x.shape, F32),
        compiler_params=_cp(1), name="logf_cumsum",
    )(x, u, lo)


def _max_row_norm_kernel(x_ref, o_ref):
    x = x_ref[...].astype(F32)
    n2 = jnp.sum(x * x, axis=-1, keepdims=True)
    o_ref[...] = jnp.broadcast_to(jnp.sqrt(jnp.max(n2, axis=0, keepdims=True)), o_ref.shape)


def _max_row_norm(x):
    nb, nh, s, hd = x.shape
    out = pl.pallas_call(
        _max_row_norm_kernel, grid=(nb, nh),
        in_specs=[pl.BlockSpec((None, None, s, hd), lambda b, h: (b, h, 0, 0))],
        out_specs=pl.BlockSpec((None, 8, LANES), lambda b, h: (b * nh + h, 0, 0)),
        out_shape=jax.ShapeDtypeStruct((nb * nh, 8, LANES), F32),
        compiler_params=_cp(2), name="max_row_norm",
    )(x)
    return out[:, 0, 0]


def _fox_first_blocks(c, qmax, kmax, tk):
    bh, s = c.shape
    cb = c.reshape(bh, s // tk, tk)
    c_start, c_end = cb[:, :, 0], cb[:, :, -1]
    budget = (2.0 * qmax * kmax + FOX_SKIP_LOG2)[:, None, None]
    skippable = (c_end[:, None, :] - c_start[:, :, None]) > budget
    earlier = np.tril(np.ones((s // tk, s // tk), bool), -1)
    return jnp.sum(skippable & earlier, axis=-1).astype(jnp.int32).reshape(-1)


def _fox_attn_kernel(first_ref, q_ref, k_ref, v_ref, c_ref, o_ref, m_ref, acc_ref, *, tk, nsub, hd):
    b, h, qi = pl.program_id(0), pl.program_id(1), pl.program_id(2)
    nkb = pl.num_programs(2) * nsub
    m_ref[...] = jnp.full(m_ref.shape, NEG, F32)
    acc_ref[...] = jnp.zeros(acc_ref.shape, F32)
    ones = jnp.ones((tk, hd), BF16)
    ncol = tk // LANES

    def step(r, j, mask=None, valid=None):
        rows = pl.ds(pl.multiple_of(j * tk, tk), tk)
        c_row = c_ref[j]
        if valid is not None:
            c_row = jnp.where(valid, c_row, -NEG_BLOCK)
        s = _dot_nt(q_ref[r * tk:(r + 1) * tk, :], k_ref[rows, :]) - c_row
        if mask is not None:
            s = jnp.where(mask, s, NEG)
        cols = [s[:, c * LANES:(c + 1) * LANES] for c in range(ncol)]
        m_cur = functools.reduce(jnp.maximum, cols)
        m_prev = m_ref[r]
        m_next = jnp.maximum(m_prev, jnp.max(m_cur, axis=-1, keepdims=True))
        alpha = jnp.exp2(m_prev - m_next)
        p = jnp.concatenate([jnp.exp2(col - m_next) for col in cols], axis=1).astype(BF16)
        pv = _dot(p, jnp.concatenate([v_ref[rows, :], ones], axis=1))
        acc_ref[r] = jnp.concatenate([alpha, alpha], axis=1) * acc_ref[r] + pv
        m_ref[r] = m_next

    base = qi * nsub
    off = (b * pl.num_programs(1) + h) * nkb + base
    n = functools.reduce(jnp.maximum, [base + r - first_ref[off + r] for r in range(nsub)])

    def body(t, carry):
        for r in range(nsub):
            j = base + r - n + t
            step(r, jnp.maximum(j, 0), valid=j >= 0)
        return carry

    lax.fori_loop(0, n, body, 0)
    row = lax.broadcasted_iota(jnp.int32, (tk, tk), 0)
    col = lax.broadcasted_iota(jnp.int32, (tk, tk), 1)
    causal = col <= row
    for r in range(nsub):
        step(r, base + r, mask=causal)
    for r in range(nsub):
        acc = acc_ref[r]
        o_ref[r * tk:(r + 1) * tk, :] = (acc[:, :hd] / acc[:, hd:]).astype(o_ref.dtype)


def _fox_attn(qh, kh, vh, c, first, *, tk, nsub):
    nb, nh, s, hd = qh.shape
    tq = tk * nsub
    nq = s // tq
    kv_spec = pl.BlockSpec((None, None, s, hd), lambda b, h, i, f: (b, h, 0, 0))
    return pl.pallas_call(
        functools.partial(_fox_attn_kernel, tk=tk, nsub=nsub, hd=hd),
        grid_spec=pltpu.PrefetchScalarGridSpec(
            num_scalar_prefetch=1, grid=(nb, nh, nq),
            in_specs=[pl.BlockSpec((None, None, tq, hd), lambda b, h, i, f: (b, h, i, 0)),
                      kv_spec, kv_spec,
                      pl.BlockSpec((None, s // tk, 1, tk), lambda b, h, i, f: (b * nh + h, 0, 0, 0))],
            out_specs=pl.BlockSpec((None, tq, hd), lambda b, h, i, f: (b, i, h)),
            scratch_shapes=[pltpu.VMEM((nsub, tk, LANES), F32), pltpu.VMEM((nsub, tk, 2 * hd), F32)]),
        out_shape=jax.ShapeDtypeStruct((nb, s, nh * hd), BF16),
        compiler_params=_cp(3), name="fox_attention",
    )(first, qh, kh, vh, c.reshape(nb * nh, s // tk, 1, tk))


def _fox_sample_kernel(q_ref, kc_ref, vc_ref, kn_ref, vn_ref, c_ref, o_ref, *, hb, hd, p_len, t):
    row = lax.broadcasted_iota(jnp.int32, (t, t), 0)
    col = lax.broadcasted_iota(jnp.int32, (t, t), 1)
    for j in range(hb):
        sl = slice(j * hd, (j + 1) * hd)
        q = q_ref[:, sl]
        c = c_ref[j:j + 1, :]
        c0 = c[:, p_len:p_len + 1]
        s_c = _dot_nt(q, kc_ref[:, sl].astype(BF16)) + (c0 - c[:, :p_len])
        s_n = _dot_nt(q, kn_ref[:, sl].astype(BF16)) + (c0 - c[:, p_len:p_len + t])
        s_n = jnp.where(col <= row, s_n, NEG)
        m = jnp.maximum(jnp.max(s_c, axis=-1, keepdims=True), jnp.max(s_n, axis=-1, keepdims=True))
        p_c = jnp.exp2(s_c - m)
        p_n = jnp.exp2(s_n - m)
        den = jnp.sum(p_c, axis=-1, keepdims=True) + jnp.sum(p_n, axis=-1, keepdims=True)
        o = _dot(p_c.astype(BF16), vc_ref[:, sl].astype(BF16)) + _dot(p_n.astype(BF16), vn_ref[:, sl].astype(BF16))
        o_ref[:, sl] = (o / den).astype(o_ref.dtype)


def _fox_sample_attn(q, k_cache, v_cache, k_new, v_new, c_all, *, nh, hd, t):
    nb, p_len, d = k_cache.shape
    hb = min(8, nh)
    w = hb * hd
    lc = c_all.shape[-1]
    cache_spec = pl.BlockSpec((None, p_len, w), lambda b, h: (b, 0, h))
    row_spec = pl.BlockSpec((t, w), lambda b, h: (b, h))
    return pl.pallas_call(
        functools.partial(_fox_sample_kernel, hb=hb, hd=hd, p_len=p_len, t=t),
        grid=(nb, nh // hb),
        in_specs=[row_spec, cache_spec, cache_spec, row_spec, row_spec,
                  pl.BlockSpec((None, hb, lc), lambda b, h: (b, h, 0))],
        out_specs=row_spec,
        out_shape=jax.ShapeDtypeStruct((nb * t, d), BF16),
        compiler_params=_cp(2), name="fox_sample_attention",
    )(q, k_cache, v_cache, k_new, v_new, c_all)


def _proj_res_kernel(a_ref, w_ref, x_ref, gate_ref, o_ref):
    o_ref[...] = x_ref[...] + gate_ref[...] * _dot(a_ref[...], w_ref[...])


def _proj_res(a, w, x, gate, *, tm, tpb):
    m, k = a.shape
    n = w.shape[1]
    return pl.pallas_call(
        _proj_res_kernel, grid=(m // tm, 1),
        in_specs=[pl.BlockSpec((tm, k), lambda i, j: (i, 0)),
                  pl.BlockSpec((k, n), lambda i, j: (0, 0)),
                  pl.BlockSpec((tm, n), lambda i, j: (i, 0)),
                  _mod_spec(gate, tpb)],
        out_specs=pl.BlockSpec((tm, n), lambda i, j: (i, 0)),
        out_shape=jax.ShapeDtypeStruct((m, n), F32),
        compiler_params=_cp(2), name="attn_out_proj",
    )(a, w, x, gate)


def _ffn_kernel(x_ref, g_ref, sh_ref, sc_ref, gate_ref, wu_ref, wd_ref, *rest, final):
    if final:
        fg_ref, o_ref, h_ref, acc_ref = rest
    else:
        o_ref, h_ref, acc_ref = rest
    f = pl.program_id(1)

    @pl.when(f == 0)
    def _():
        h_ref[...] = _norm_mod(x_ref[...], g_ref[...], sh_ref[...], sc_ref[...]).astype(BF16)
        acc_ref[...] = jnp.zeros(acc_ref.shape, F32)

    a = jnp.maximum(_dot(h_ref[...], wu_ref[...]), 0.0)
    acc_ref[...] += _dot((a * a).astype(BF16), wd_ref[...])

    @pl.when(f == pl.num_programs(1) - 1)
    def _():
        y = x_ref[...] + gate_ref[...] * acc_ref[...]
        if final:
            y = _rms(y) * fg_ref[...]
        o_ref[...] = y


def _ffn(x, g, shift, scale, gate, w_up, w_down, final_g, *, tm, tpb):
    m, d = x.shape
    ff = w_up.shape[1]
    tf = min(512, ff)
    vec = pl.BlockSpec((1, d), lambda i, f: (0, 0))
    in_specs = [pl.BlockSpec((tm, d), lambda i, f: (i, 0)), vec,
                _mod_spec(shift, tpb), _mod_spec(scale, tpb), _mod_spec(gate, tpb),
                pl.BlockSpec((d, tf), lambda i, f: (0, f)),
                pl.BlockSpec((tf, d), lambda i, f: (f, 0))]
    args = [x, g, shift, scale, gate, w_up, w_down]
    if final_g is not None:
        in_specs.append(vec)
        args.append(final_g)
    return pl.pallas_call(
        functools.partial(_ffn_kernel, final=final_g is not None),
        grid=(m // tm, ff // tf), in_specs=in_specs,
        out_specs=pl.BlockSpec((tm, d), lambda i, f: (i, 0)),
        out_shape=jax.ShapeDtypeStruct((m, d), F32),
        scratch_shapes=[pltpu.VMEM((tm, d), BF16), pltpu.VMEM((tm, d), F32)],
        compiler_params=_cp(2), name="ffn_final" if final_g is not None else "ffn",
    )(*args)


def _rope(r, a, b, c):
    half_shift = (LANES // 8) // 2
    cols = []
    for j in range(r.shape[1] // LANES):
        x = r[:, j * LANES:(j + 1) * LANES]
        cols.append(x * a + pltpu.roll(x, LANES - half_shift, 1) * b + pltpu.roll(x, half_shift, 1) * c)
    return jnp.concatenate(cols, axis=1)


def _dup_halves(x, hd):
    lo = lax.broadcasted_iota(jnp.int32, (x.shape[0], LANES), 1) < hd
    zero = jnp.zeros((x.shape[0], LANES), x.dtype)
    out = []
    for j in range(x.shape[1] // LANES):
        p = x[:, j * LANES:(j + 1) * LANES]
        r = pltpu.roll(p, hd, 1)
        out += [jnp.where(lo, p, zero), jnp.where(lo, zero, r), jnp.where(lo, r, zero), jnp.where(lo, zero, p)]
    return jnp.concatenate(out, axis=1)


def _swa_in_kernel(x_ref, g_ref, sh_ref, sc_ref, w_ref, ra_ref, rb_ref, rc_ref,
                   q_ref, kk_ref, vv_ref, k_ref, v_ref, h_ref, *, nqt, kvd, hd, q_scale):
    n = pl.program_id(1)

    @pl.when(n == 0)
    def _():
        h_ref[...] = _norm_mod(x_ref[...], g_ref[...], sh_ref[...], sc_ref[...]).astype(BF16)

    r = _dot(h_ref[...], w_ref[...])

    @pl.when(n < nqt)
    def _():
        q_ref[...] = (_rope(r, ra_ref[...], rb_ref[...], rc_ref[...]) * q_scale).astype(BF16)

    @pl.when(n == nqt)
    def _():
        k = _rope(r[:, :kvd], ra_ref[...], rb_ref[...], rc_ref[...])
        v = r[:, kvd:]
        k_ref[...] = k
        v_ref[...] = v
        kk_ref[...] = _dup_halves(k, hd).astype(BF16)
        vv_ref[...] = _dup_halves(v, hd).astype(BF16)


def _swa_in(x, g, shift, scale, w, tabs, *, tm, tpb, kvd, hd):
    m, d = x.shape
    tn = 2 * kvd
    nqt = d // tn
    ntab = tabs[0].shape[0] // tm
    kkw = (kvd // hd) * 2 * LANES
    tab_spec = pl.BlockSpec((tm, LANES), lambda i, n: (i % ntab, 0))
    const = lambda w_: pl.BlockSpec((tm, w_), lambda i, n: (i, 0))
    return pl.pallas_call(
        functools.partial(_swa_in_kernel, nqt=nqt, kvd=kvd, hd=hd, q_scale=hd ** -0.5 * LOG2E),
        grid=(m // tm, nqt + 1),
        in_specs=[pl.BlockSpec((tm, d), lambda i, n: (i, 0)),
                  pl.BlockSpec((1, d), lambda i, n: (0, 0)),
                  _mod_spec(shift, tpb), _mod_spec(scale, tpb),
                  pl.BlockSpec((d, tn), lambda i, n: (0, n)),
                  tab_spec, tab_spec, tab_spec],
        out_specs=[pl.BlockSpec((tm, tn), lambda i, n: (i, jnp.minimum(n, nqt - 1))),
                   const(kkw), const(kkw), const(kvd), const(kvd)],
        out_shape=[jax.ShapeDtypeStruct((m, d), BF16),
                   jax.ShapeDtypeStruct((m, kkw), BF16), jax.ShapeDtypeStruct((m, kkw), BF16),
                   jax.ShapeDtypeStruct((m, kvd), F32), jax.ShapeDtypeStruct((m, kvd), F32)],
        scratch_shapes=[pltpu.VMEM((tm, d), BF16)],
        compiler_params=_cp(2), name="swa_in_proj",
    )(x, g, shift, scale, w, *tabs)


def _rope_tables(pos, hd, rope_dims):
    half = rope_dims // 2
    inv_freq = ROPE_THETA ** (-jnp.arange(half, dtype=F32) * 2.0 / rope_dims)
    ang = pos.astype(F32)[:, None] * inv_freq[None, :]
    cos, sin = jnp.cos(ang), jnp.sin(ang)
    lane = np.arange(LANES) % hd
    idx = lane % half
    a = jnp.where(lane < rope_dims, cos[:, idx], 1.0)
    b = jnp.where(lane < half, -sin[:, idx], 0.0)
    c = jnp.where((lane >= half) & (lane < rope_dims), sin[:, idx], 0.0)
    return a, b, c


def _swa_core(q, k_top, k_bot, v_top, v_bot, bias, sinks, *, group):
    t = q.shape[0]
    npair = group // 2
    win = k_top.shape[0]
    half_lanes = lax.broadcasted_iota(jnp.int32, (win, LANES), 1) < LANES // 2
    count_even = jnp.where(half_lanes, 1.0, 0.0).astype(BF16)
    count_odd = jnp.where(half_lanes, 0.0, 1.0).astype(BF16)
    v2 = jnp.concatenate([jnp.concatenate([v_top, count_even], axis=1),
                          jnp.concatenate([v_bot, count_odd], axis=1)], axis=0)
    qs = jnp.concatenate([q[:, p * LANES:(p + 1) * LANES] for p in range(npair)], axis=0)
    s = _dot_nt(qs, jnp.concatenate([k_top, k_bot], axis=0)) + bias
    ps, es = [], []
    for half in range(2):
        cols = [s[:, half * win + c * LANES: half * win + (c + 1) * LANES] for c in range(win // LANES)]
        sink = sinks[:, half * LANES:(half + 1) * LANES]
        m = jnp.maximum(jnp.max(functools.reduce(jnp.maximum, cols), axis=-1, keepdims=True), sink)
        ps += [jnp.exp2(col - m) for col in cols]
        es.append(jnp.exp2(sink - m))
    o = _dot(jnp.concatenate(ps, axis=1).astype(BF16), v2)
    lane = lax.broadcasted_iota(jnp.int32, (o.shape[0], LANES), 1)
    o = o[:, :LANES] / (o[:, LANES:] + jnp.where(lane < LANES // 2, es[0], es[1]))
    return jnp.concatenate([o[p * t:(p + 1) * t] for p in range(npair)], axis=1)


def _swa_attn_kernel(q_ref, kh_ref, km_ref, vh_ref, vm_ref, bias_ref, sink_ref, o_ref, *, kv, group, hd):
    gw = group * hd
    bias = bias_ref[...]
    for g in range(kv):
        def win(halo, main, off):
            sl = slice(g * 2 * LANES + off, g * 2 * LANES + off + LANES)
            return jnp.concatenate([halo[:, sl], main[:, sl]], axis=0)
        o = _swa_core(q_ref[:, g * gw:(g + 1) * gw],
                      win(kh_ref, km_ref, 0), win(kh_ref, km_ref, LANES),
                      win(vh_ref, vm_ref, 0), win(vh_ref, vm_ref, LANES),
                      bias, sink_ref[g], group=group)
        o_ref[:, g * gw:(g + 1) * gw] = o.astype(o_ref.dtype)


def _swa_attn(q, kk, vv, bias2, sink_rows, *, nb, s, kv, group, hd):
    t = 2 * CHUNK
    nt = s // t
    d = q.shape[1]
    kkw = kk.shape[1]
    main = pl.BlockSpec((t, kkw), lambda b, i: (b * nt + i, 0))
    halo = pl.BlockSpec((t, kkw), lambda b, i: (b * nt + jnp.maximum(i - 1, 0), 0))
    return pl.pallas_call(
        functools.partial(_swa_attn_kernel, kv=kv, group=group, hd=hd),
        grid=(nb, nt),
        in_specs=[pl.BlockSpec((t, d), lambda b, i: (b * nt + i, 0)),
                  halo, main, halo, main,
                  pl.BlockSpec((None,) + bias2.shape[1:], lambda b, i: (jnp.minimum(i, 1), 0, 0)),
                  pl.BlockSpec(sink_rows.shape, lambda b, i: (0, 0, 0))],
        out_specs=pl.BlockSpec((t, d), lambda b, i: (b * nt + i, 0)),
        out_shape=jax.ShapeDtypeStruct(q.shape, BF16),
        compiler_params=_cp(2), name="swa_attention",
    )(q, kk, kk, vv, vv, bias2, sink_rows)


def _swa_sample_kernel(q_ref, kc_ref, vc_ref, kkn_ref, vvn_ref, bias_ref, sink_ref, o_ref, *, kv, group, hd, t):
    gw = group * hd
    kkc = _dup_halves(kc_ref[...], hd).astype(BF16)
    vvc = _dup_halves(vc_ref[...], hd).astype(BF16)
    pad = jnp.zeros((kc_ref.shape[0] - t, LANES), BF16)
    for g in range(kv):
        def win(cache, new, off):
            sl = slice(g * 2 * LANES + off, g * 2 * LANES + off + LANES)
            return jnp.concatenate([cache[:, sl], new[:, sl], pad], axis=0)
        o = _swa_core(q_ref[:, g * gw:(g + 1) * gw],
                      win(kkc, kkn_ref, 0), win(kkc, kkn_ref, LANES),
                      win(vvc, vvn_ref, 0), win(vvc, vvn_ref, LANES),
                      bias_ref[...], sink_ref[g], group=group)
        o_ref[:, g * gw:(g + 1) * gw] = o.astype(o_ref.dtype)


def _swa_sample_attn(q, k_cache, v_cache, kk_new, vv_new, bias, sink_rows, *, kv, group, hd, t):
    nb, buf, kvd = k_cache.shape
    d = q.shape[1]
    full = lambda a: pl.BlockSpec(a.shape, lambda b: (0,) * a.ndim)
    cache_spec = pl.BlockSpec((None, buf, kvd), lambda b: (b, 0, 0))
    new_spec = pl.BlockSpec((t, kk_new.shape[1]), lambda b: (b, 0))
    return pl.pallas_call(
        functools.partial(_swa_sample_kernel, kv=kv, group=group, hd=hd, t=t),
        grid=(nb,),
        in_specs=[pl.BlockSpec((t, d), lambda b: (b, 0)), cache_spec, cache_spec, new_spec, new_spec,
                  full(bias), full(sink_rows)],
        out_specs=pl.BlockSpec((t, d), lambda b: (b, 0)),
        out_shape=jax.ShapeDtypeStruct(q.shape, BF16),
        compiler_params=_cp(1), name="swa_sample_attention",
    )(q, k_cache, v_cache, kk_new, vv_new, bias, sink_rows)


def _window_bias(valid, npair):
    b = np.where(valid, 0.0, NEG).astype(np.float32)
    return np.tile(b, (npair, 2))


def _prompt_bias(npair):
    t = 2 * CHUNK
    qc = np.arange(t)[:, None] // CHUNK
    kc = np.arange(t + SWA_WINDOW_CHUNKS * CHUNK)[None, :] // CHUNK
    valid = (kc >= qc) & (kc <= qc + SWA_WINDOW_CHUNKS)
    first = valid & (kc >= SWA_WINDOW_CHUNKS)
    return jnp.asarray(np.stack([_window_bias(first, npair), _window_bias(valid, npair)]))


def _sample_bias(past_len, buf, t, npair):
    q_pos = past_len + np.arange(t)
    k_pos = np.concatenate([past_len - buf + np.arange(buf), q_pos])
    qch, kch = q_pos // CHUNK, k_pos // CHUNK
    valid = np.zeros((t, 2 * buf), bool)
    valid[:, :buf + t] = (kch[None, :] <= qch[:, None]) & (kch[None, :] >= qch[:, None] - SWA_WINDOW_CHUNKS)
    return jnp.asarray(_window_bias(valid, npair))


def _sink_rows(sinks, kv, group, t):
    s = (sinks * LOG2E).reshape(kv, group // 2, 1, 2, 1)
    return jnp.broadcast_to(s, (kv, group // 2, t, 2, LANES)).reshape(kv, (group // 2) * t, 2 * LANES)


def kernel(x_prompt, x_sample, c_prompt, c_sample, cache_fox_k, cache_fox_v, cache_fox_logf, cache_swa_k,
           cache_swa_v, ada_w, ada_b, norm_mix_g, norm_ffn_g, fox_w_in, fox_b_f, fox_w_out, swa_w_in,
           swa_sinks, swa_w_out, ffn_w_up, ffn_w_down, final_g):
    bp, s, d = x_prompt.shape
    bs, t, _ = x_sample.shape
    depth = ada_w.shape[0]
    past_len = cache_fox_k.shape[2]
    nh_fox, hd_fox = cache_fox_k.shape[3], cache_fox_k.shape[4]
    buf, kv, hd_swa = cache_swa_k.shape[2], cache_swa_k.shape[3], cache_swa_k.shape[4]
    nh_swa = swa_sinks.shape[1]
    group = nh_swa // kv
    kvd = kv * hd_swa
    rope_dims = hd_swa // 4
    assert hd_fox == LANES and hd_swa == LANES // 2 and rope_dims == 16 and group % 2 == 0
    assert buf == SWA_WINDOW_CHUNKS * CHUNK and t <= buf and s % (2 * CHUNK) == 0

    mp, ms = bp * s, bs * t
    tm_p = min(512, s)
    tpb_p = s // tm_p
    tm_fox = min(1024, s)
    tk = min(FOX_TK, s)
    tq = min(FOX_NSUB * tk, s)

    mods = _ada(jnp.concatenate([c_prompt, c_sample], axis=0), ada_w, ada_b)

    def split_mods(i):
        six = jnp.split(mods[i], 6, axis=-1)
        prompt = [m[:bp, None, :] for m in six]
        sample = [jnp.repeat(m[bp:], t, axis=0)[None] for m in six]
        return prompt, sample

    row = lambda v: v.reshape(1, -1)
    xp = x_prompt.reshape(mp, d)
    xs = x_sample.reshape(ms, d)
    pos_p = jnp.arange(s)
    pos_s = past_len + jnp.arange(t)
    tabs_p = _rope_tables(pos_p, hd_swa, rope_dims)
    tabs_s = tuple(jnp.tile(a, (bs, 1)) for a in _rope_tables(pos_s, hd_swa, rope_dims))
    bias_p = _prompt_bias(group // 2)
    bias_s = _sample_bias(past_len, buf, t, group // 2)

    outs = {k: [] for k in ("fkp", "fvp", "flp", "fks", "fvs", "fls", "skp", "svp", "sks", "svs")}
    for i in range(depth):
        mod_p, mod_s = split_mods(i)
        j = i // 2
        if i % 2 == 0:
            w_in = fox_w_in[j]
            w_qkv = w_in[:, :3 * d].astype(BF16)
            w_f = jnp.pad(w_in[:, 3 * d:], ((0, 0), (0, LANES - nh_fox))).astype(BF16)
            b_f = row(fox_b_f[j])
            w_out = fox_w_out[j].astype(BF16)
            qh, kh, vh, k32, v32, lf = _fox_in(xp, row(norm_mix_g[i]), mod_p[0], mod_p[1], w_qkv, w_f, b_f,
                                               tm=tm_fox, tpb=s // tm_fox, head_major=True, nh=nh_fox,
                                               hd=hd_fox)
            lft = lf.reshape(bp, s, nh_fox).transpose(0, 2, 1)
            c = _cumsum_rows(lft.reshape(-1, LANES), s // LANES).reshape(bp * nh_fox, s)
            first = _fox_first_blocks(c, _max_row_norm(qh), _max_row_norm(kh), tk)
            op = _fox_attn(qh, kh, vh, c, first, tk=tk, nsub=tq // tk)
            xp = _proj_res(op.reshape(mp, d), w_out, xp, mod_p[2], tm=tm_p, tpb=tpb_p)
            outs["fkp"].append(k32.reshape(bp, s, nh_fox, hd_fox))
            outs["fvp"].append(v32.reshape(bp, s, nh_fox, hd_fox))
            outs["flp"].append(lf.reshape(bp, s, nh_fox))
            qs_, k32s, v32s, lfs = _fox_in(xs, row(norm_mix_g[i]), mod_s[0], mod_s[1], w_qkv, w_f, b_f,
                                           tm=ms, tpb=1, head_major=False, nh=nh_fox, hd=hd_fox)
            lf_all = jnp.concatenate([cache_fox_logf[j], lfs.reshape(bs, t, nh_fox)], axis=1)
            lc = LANES * int(2 ** np.ceil(np.log2(-(-(past_len + t) // LANES))))
            lf_all = jnp.pad(lf_all.transpose(0, 2, 1), ((0, 0), (0, 0), (0, lc - past_len - t)))
            c_s = _cumsum_rows(lf_all.reshape(-1, LANES), lc // LANES).reshape(bs, nh_fox, lc)
            os_ = _fox_sample_attn(qs_, cache_fox_k[j].reshape(bs, past_len, d),
                                   cache_fox_v[j].reshape(bs, past_len, d), k32s, v32s, c_s,
                                   nh=nh_fox, hd=hd_fox, t=t)
            xs = _proj_res(os_, w_out, xs, mod_s[2], tm=ms, tpb=1)
            outs["fks"].append(k32s.reshape(bs, t, nh_fox, hd_fox))
            outs["fvs"].append(v32s.reshape(bs, t, nh_fox, hd_fox))
            outs["fls"].append(lfs.reshape(bs, t, nh_fox))
        else:
            w_in = swa_w_in[j].astype(BF16)
            w_out = swa_w_out[j].astype(BF16)
            q, kk, vv, k32, v32 = _swa_in(xp, row(norm_mix_g[i]), mod_p[0], mod_p[1], w_in, tabs_p,
                                          tm=tm_p, tpb=tpb_p, kvd=kvd, hd=hd_swa)
            op = _swa_attn(q, kk, vv, bias_p, _sink_rows(swa_sinks[j], kv, group, 2 * CHUNK),
                           nb=bp, s=s, kv=kv, group=group, hd=hd_swa)
            xp = _proj_res(op, w_out, xp, mod_p[2], tm=tm_p, tpb=tpb_p)
            outs["skp"].append(k32.reshape(bp, s, kv, hd_swa)[:, -buf:])
            outs["svp"].append(v32.reshape(bp, s, kv, hd_swa)[:, -buf:])
            q, kk, vv, k32, v32 = _swa_in(xs, row(norm_mix_g[i]), mod_s[0], mod_s[1], w_in, tabs_s,
                                          tm=ms, tpb=1, kvd=kvd, hd=hd_swa)
            os_ = _swa_sample_attn(q, cache_swa_k[j].reshape(bs, buf, kvd), cache_swa_v[j].reshape(bs, buf, kvd),
                                   kk, vv, bias_s, _sink_rows(swa_sinks[j], kv, group, t),
                                   kv=kv, group=group, hd=hd_swa, t=t)
            xs = _proj_res(os_, w_out, xs, mod_s[2], tm=ms, tpb=1)
            k_all = jnp.concatenate([cache_swa_k[j], k32.reshape(bs, t, kv, hd_swa)], axis=1)
            v_all = jnp.concatenate([cache_swa_v[j], v32.reshape(bs, t, kv, hd_swa)], axis=1)
            outs["sks"].append(k_all[:, -buf:])
            outs["svs"].append(v_all[:, -buf:])
        fg = row(final_g) if i == depth - 1 else None
        w_up = ffn_w_up[i].astype(BF16)
        w_down = ffn_w_down[i].astype(BF16)
        xp = _ffn(xp, row(norm_ffn_g[i]), mod_p[3], mod_p[4], mod_p[5], w_up, w_down, fg, tm=tm_p, tpb=tpb_p)
        xs = _ffn(xs, row(norm_ffn_g[i]), mod_s[3], mod_s[4], mod_s[5], w_up, w_down, fg, tm=ms, tpb=1)

    st = lambda k: jnp.stack(outs[k])
    return (xp.reshape(bp, s, d), xs.reshape(bs, t, d),
            st("fkp"), st("fvp"), st("flp"), st("fks"), st("fvs"), st("fls"),
            st("skp"), st("svp"), st("sks"), st("svs"))
```

```python
import functools

import numpy as np
import jax
import jax.numpy as jnp
from jax import lax
from jax.experimental import pallas as pl
from jax.experimental.pallas import tpu as pltpu

F32 = jnp.float32
BF16 = jnp.bfloat16

RMS_EPS = 1e-6
CHUNK = 64
SWA_WINDOW_CHUNKS = 2
ROPE_THETA = 500000.0
LANES = 128
NEG = -1e30
NEG_BLOCK = -3e38
VMEM_LIMIT_BYTES = 56 * 1024 * 1024
LOG2E = 1.4426950408889634
FOX_SKIP_LOG2 = 150.0
FOX_TK = 512
FOX_NSUB = 4


def _cp(n_axes):
    return pltpu.CompilerParams(dimension_semantics=("arbitrary",) * n_axes,
                                vmem_limit_bytes=VMEM_LIMIT_BYTES)


def _dot(a, b):
    return jnp.dot(a, b, preferred_element_type=F32)


def _dot_nt(a, b):
    return lax.dot_general(a, b, (((1,), (1,)), ((), ())), preferred_element_type=F32)


def _rms(x):
    return x * lax.rsqrt(jnp.mean(x * x, axis=-1, keepdims=True) + RMS_EPS)


NORM_ROWS = 16


def _norm_mod_store(h_ref, x_ref, g_ref, sh_ref, sc_ref):
    per_token = sh_ref.shape[0] > 1
    g = g_ref[...]
    if not per_token:
        gain, shift = g * (1.0 + sc_ref[...]), sh_ref[...]

    def body(i, carry):
        rows = pl.ds(pl.multiple_of(i * NORM_ROWS, NORM_ROWS), NORM_ROWS)
        x = x_ref[rows, :]
        if per_token:
            gain_, shift_ = g * (1.0 + sc_ref[rows, :]), sh_ref[rows, :]
        else:
            gain_, shift_ = gain, shift
        h_ref[rows, :] = (_rms(x) * gain_ + shift_).astype(h_ref.dtype)
        return carry

    lax.fori_loop(0, x_ref.shape[0] // NORM_ROWS, body, 0, unroll=4)


def _log_sigmoid(z):
    return jnp.minimum(z, 0.0) - jnp.log1p(jnp.exp(-jnp.abs(z)))


def _ada_kernel(c_ref, w_ref, b_ref, o_ref):
    c = c_ref[...]
    a = (c / (1.0 + jnp.exp(-c))).astype(BF16)
    o_ref[...] = _dot(a, w_ref[...].astype(BF16)) + b_ref[...]


def _ada(c_all, ada_w, ada_b):
    depth, d, n6 = ada_w.shape
    r = c_all.shape[0]
    tn = min(1024, n6)
    return pl.pallas_call(
        _ada_kernel,
        grid=(depth, n6 // tn),
        in_specs=[pl.BlockSpec((r, d), lambda l, n: (0, 0)),
                  pl.BlockSpec((None, d, tn), lambda l, n: (l, 0, n)),
                  pl.BlockSpec((None, 1, tn), lambda l, n: (l, 0, n))],
        out_specs=pl.BlockSpec((None, r, tn), lambda l, n: (l, 0, n)),
        out_shape=jax.ShapeDtypeStruct((depth, r, n6), F32),
        compiler_params=_cp(2), name="ada_params",
    )(c_all, ada_w, ada_b.reshape(depth, 1, n6))


def _fox_in_kernel(x_ref, g_ref, sh_ref, sc_ref, wq_ref, wk_ref, wv_ref, wf_ref, bf_ref, *rest,
                   head_major, hd, nh, q_scale):
    if head_major:
        qb_ref, kb_ref, vb_ref, k_ref, v_ref, lf_ref, h_ref = rest
    else:
        qb_ref, k_ref, v_ref, lf_ref, h_ref = rest

    @pl.when(pl.program_id(1) == 0)
    def _():
        _norm_mod_store(h_ref, x_ref, g_ref, sh_ref, sc_ref)
        z = _dot(h_ref[...], wf_ref[...])[:, :nh] + bf_ref[...]
        lf_ref[...] = _log_sigmoid(z)

    hb = h_ref[...]
    q = _dot(hb, wq_ref[...]) * q_scale
    k = _dot(hb, wk_ref[...])
    v = _dot(hb, wv_ref[...])
    k_ref[...] = k
    v_ref[...] = v
    if head_major:
        for j in range(q.shape[1] // hd):
            sl = slice(j * hd, (j + 1) * hd)
            qb_ref[j] = q[:, sl].astype(BF16)
            kb_ref[j] = k[:, sl].astype(BF16)
            vb_ref[j] = v[:, sl].astype(BF16)
    else:
        qb_ref[...] = q.astype(BF16)


def _mod_spec(mod, tpb, cols=None):
    _, rows, d = mod.shape
    if cols is None:
        return pl.BlockSpec((None, rows, d), lambda i, n: (i // tpb, 0, 0))
    return pl.BlockSpec((None, rows, cols), lambda i, n: (i // tpb, 0, n))


def _fox_in(x, g, shift, scale, w_in, w_f, b_f, *, layer, tm, tpb, head_major, nh, hd):
    m, d = x.shape
    tn = min(512, d)
    nt = d // tn
    hpt = tn // hd
    nb, s = m // (tm * tpb), tm * tpb
    in_specs = [pl.BlockSpec((tm, d), lambda i, n: (i, 0)),
                pl.BlockSpec((1, d), lambda i, n: (0, 0)),
                _mod_spec(shift, tpb), _mod_spec(scale, tpb),
                pl.BlockSpec((None, d, tn), lambda i, n: (layer, 0, n)),
                pl.BlockSpec((None, d, tn), lambda i, n: (layer, 0, n + nt)),
                pl.BlockSpec((None, d, tn), lambda i, n: (layer, 0, n + 2 * nt)),
                pl.BlockSpec((d, LANES), lambda i, n: (0, 0)),
                pl.BlockSpec((1, nh), lambda i, n: (0, 0))]
    rm_spec = pl.BlockSpec((tm, tn), lambda i, n: (i, n))
    lf_spec = pl.BlockSpec((tm, nh), lambda i, n: (i, 0))
    if head_major:
        hm_spec = pl.BlockSpec((None, hpt, tm, hd), lambda i, n: (i // tpb, n, i % tpb, 0))
        hm_shape = jax.ShapeDtypeStruct((nb, nh, s, hd), BF16)
        out_specs = [hm_spec, hm_spec, hm_spec, rm_spec, rm_spec, lf_spec]
        out_shape = [hm_shape, hm_shape, hm_shape]
    else:
        out_specs = [rm_spec, rm_spec, rm_spec, lf_spec]
        out_shape = [jax.ShapeDtypeStruct((m, d), BF16)]
    out_shape += [jax.ShapeDtypeStruct((m, d), F32), jax.ShapeDtypeStruct((m, d), F32),
                  jax.ShapeDtypeStruct((m, nh), F32)]
    return pl.pallas_call(
        functools.partial(_fox_in_kernel, head_major=head_major, hd=hd, nh=nh, q_scale=hd ** -0.5 * LOG2E),
        grid=(m // tm, nt), in_specs=in_specs, out_specs=out_specs, out_shape=out_shape,
        scratch_shapes=[pltpu.VMEM((tm, d), BF16)],
        compiler_params=_cp(2), name="fox_in_proj",
    )(x, g, shift, scale, w_in, w_in, w_in, w_f, b_f)


def _split3(x):
    x1 = x.astype(BF16)
    r1 = x - x1.astype(F32)
    x2 = r1.astype(BF16)
    x3 = (r1 - x2.astype(F32)).astype(BF16)
    return x1, x2, x3


def _cumsum_kernel(x_ref, u_ref, l_ref, o_ref):
    u = u_ref[...]
    lo = l_ref[...]
    y = sum(_dot(p, u) for p in _split3(x_ref[...]))
    tot = jnp.broadcast_to(y[:, LANES - 1:LANES], y.shape)
    o_ref[...] = (y + sum(_dot(lo, p) for p in _split3(tot))) * LOG2E


def _cumsum_rows(x, nc):
    rows = x.shape[0]
    assert LANES % nc == 0 or nc % LANES == 0
    n = max(nc, LANES)
    assert rows % n == 0
    u = jnp.asarray(np.triu(np.ones((LANES, LANES), np.float32)), BF16)
    idx = np.arange(n)
    same = (idx[:, None] // nc) == (idx[None, :] // nc)
    lo = jnp.asarray((same & (idx[None, :] < idx[:, None])).astype(np.float32), BF16)
    return pl.pallas_call(
        _cumsum_kernel, grid=(rows // n,),
        in_specs=[pl.BlockSpec((n, LANES), lambda i: (i, 0)),
                  pl.BlockSpec((LANES, LANES), lambda i: (0, 0)),
                  pl.BlockSpec((n, n), lambda i: (0, 0))],
        out_specs=pl.BlockSpec((n, LANES), lambda i: (i, 0)),
        out_shape=jax.ShapeDtypeStruct(x.shape, F32),
        compiler_params=_cp(1), name="logf_cumsum",
    )(x, u, lo)


def _max_row_norm_kernel(x_ref, o_ref):
    x = x_ref[...].astype(F32)
    n2 = jnp.sum(x * x, axis=-1, keepdims=True)
    o_ref[...] = jnp.broadcast_to(jnp.sqrt(jnp.max(n2, axis=0, keepdims=True)), o_ref.shape)


def _max_row_norm(x):
    nb, nh, s, hd = x.shape
    out = pl.pallas_call(
        _max_row_norm_kernel, grid=(nb, nh),
        in_specs=[pl.BlockSpec((None, None, s, hd), lambda b, h: (b, h, 0, 0))],
        out_specs=pl.BlockSpec((None, 8, LANES), lambda b, h: (b * nh + h, 0, 0)),
        out_shape=jax.ShapeDtypeStruct((nb * nh, 8, LANES), F32),
        compiler_params=_cp(2), name="max_row_norm",
    )(x)
    return out[:, 0, 0]


def _fox_first_blocks(c, qmax, kmax, tk):
    bh, s = c.shape
    cb = c.reshape(bh, s // tk, tk)
    c_start, c_end = cb[:, :, 0], cb[:, :, -1]
    budget = (2.0 * qmax * kmax + FOX_SKIP_LOG2)[:, None, None]
    skippable = (c_end[:, None, :] - c_start[:, :, None]) > budget
    earlier = np.tril(np.ones((s // tk, s // tk), bool), -1)
    return jnp.sum(skippable & earlier, axis=-1).astype(jnp.int32).reshape(-1)


def _fox_attn_kernel(first_ref, q_ref, k_ref, v_ref, c_ref, o_ref, m_ref, acc_ref, *, tk, nsub, hd):
    b, h, qi = pl.program_id(0), pl.program_id(1), pl.program_id(2)
    nkb = pl.num_programs(2) * nsub
    m_ref[...] = jnp.full(m_ref.shape, NEG, F32)
    acc_ref[...] = jnp.zeros(acc_ref.shape, F32)
    ones = jnp.ones((tk, hd), BF16)
    ncol = tk // LANES

    def step(r, j, mask=None, valid=None):
        rows = pl.ds(pl.multiple_of(j * tk, tk), tk)
        c_row = c_ref[j]
        if valid is not None:
            c_row = jnp.where(valid, c_row, -NEG_BLOCK)
        s = _dot_nt(q_ref[r * tk:(r + 1) * tk, :], k_ref[rows, :]) - c_row
        if mask is not None:
            s = jnp.where(mask, s, NEG)
        cols = [s[:, c * LANES:(c + 1) * LANES] for c in range(ncol)]
        m_cur = functools.reduce(jnp.maximum, cols)
        m_prev = m_ref[r]
        m_next = jnp.maximum(m_prev, jnp.max(m_cur, axis=-1, keepdims=True))
        alpha = jnp.exp2(m_prev - m_next)
        p = jnp.concatenate([jnp.exp2(col - m_next) for col in cols], axis=1).astype(BF16)
        pv = _dot(p, jnp.concatenate([v_ref[rows, :], ones], axis=1))
        acc_ref[r] = jnp.concatenate([alpha, alpha], axis=1) * acc_ref[r] + pv
        m_ref[r] = m_next

    base = qi * nsub
    off = (b * pl.num_programs(1) + h) * nkb + base
    n = functools.reduce(jnp.maximum, [base + r - first_ref[off + r] for r in range(nsub)])

    def body(t, carry):
        for r in range(nsub):
            j = base + r - n + t
            step(r, jnp.maximum(j, 0), valid=j >= 0)
        return carry

    lax.fori_loop(0, n, body, 0)
    row = lax.broadcasted_iota(jnp.int32, (tk, tk), 0)
    col = lax.broadcasted_iota(jnp.int32, (tk, tk), 1)
    causal = col <= row
    for r in range(nsub):
        step(r, base + r, mask=causal)
    for r in range(nsub):
        acc = acc_ref[r]
        o_ref[r * tk:(r + 1) * tk, :] = (acc[:, :hd] / acc[:, hd:]).astype(o_ref.dtype)


def _fox_attn(qh, kh, vh, c, first, *, tk, nsub):
    nb, nh, s, hd = qh.shape
    tq = tk * nsub
    nq = s // tq
    kv_spec = pl.BlockSpec((None, None, s, hd), lambda b, h, i, f: (b, h, 0, 0))
    return pl.pallas_call(
        functools.partial(_fox_attn_kernel, tk=tk, nsub=nsub, hd=hd),
        grid_spec=pltpu.PrefetchScalarGridSpec(
            num_scalar_prefetch=1, grid=(nb, nh, nq),
            in_specs=[pl.BlockSpec((None, None, tq, hd), lambda b, h, i, f: (b, h, i, 0)),
                      kv_spec, kv_spec,
                      pl.BlockSpec((None, s // tk, 1, tk), lambda b, h, i, f: (b * nh + h, 0, 0, 0))],
            out_specs=pl.BlockSpec((None, tq, hd), lambda b, h, i, f: (b, i, h)),
            scratch_shapes=[pltpu.VMEM((nsub, tk, LANES), F32), pltpu.VMEM((nsub, tk, 2 * hd), F32)]),
        out_shape=jax.ShapeDtypeStruct((nb, s, nh * hd), BF16),
        compiler_params=_cp(3), name="fox_attention",
    )(first, qh, kh, vh, c.reshape(nb * nh, s // tk, 1, tk))


def _fox_sample_kernel(q_ref, kc_ref, vc_ref, kn_ref, vn_ref, c_ref, cf_ref, o_ref, *, hb, hd, p_len, t):
    n = p_len * hb
    kf = kc_ref[...].reshape(n, hd).astype(BF16)
    vf = vc_ref[...].reshape(n, hd).astype(BF16)
    heads = [slice(j * hd, (j + 1) * hd) for j in range(hb)]
    q8 = jnp.concatenate([q_ref[:, sl] for sl in heads], axis=0)
    row_head = lax.shift_right_logical(lax.broadcasted_iota(jnp.int32, (hb * t, LANES), 0), t.bit_length() - 1)
    col_head = lax.broadcasted_iota(jnp.int32, (hb * t, LANES), 1) & (hb - 1)
    own = jnp.where(row_head == col_head, 0.0, NEG)
    s = _dot_nt(q8, kf)
    cf = cf_ref[...]
    cols = [s[:, g * LANES:(g + 1) * LANES] - cf[:, g * LANES:(g + 1) * LANES] + own for g in range(n // LANES)]
    tri = lax.broadcasted_iota(jnp.int32, (t, t), 1) <= lax.broadcasted_iota(jnp.int32, (t, t), 0)
    s_n = jnp.concatenate(
        [jnp.where(tri, _dot_nt(q_ref[:, sl], kn_ref[:, sl].astype(BF16)) - c_ref[j:j + 1, p_len:p_len + t], NEG)
         for j, sl in enumerate(heads)], axis=0)
    m = jnp.maximum(jnp.max(functools.reduce(jnp.maximum, cols), axis=-1, keepdims=True),
                    jnp.max(s_n, axis=-1, keepdims=True))
    ps = [jnp.exp2(col - m) for col in cols]
    p_n = jnp.exp2(s_n - m)
    den = jnp.sum(functools.reduce(jnp.add, ps), axis=-1, keepdims=True) + jnp.sum(p_n, axis=-1, keepdims=True)
    o = _dot(jnp.concatenate(ps, axis=1).astype(BF16), vf)
    o_n = jnp.concatenate([_dot(p_n[j * t:(j + 1) * t].astype(BF16), vn_ref[:, sl].astype(BF16))
                           for j, sl in enumerate(heads)], axis=0)
    o = (o + o_n) / den
    for j, sl in enumerate(heads):
        o_ref[:, sl] = o[j * t:(j + 1) * t].astype(o_ref.dtype)


def _fox_sample_attn(q, k_cache, v_cache, k_new, v_new, c_all, *, layer, t):
    _, nb, p_len, nh, hd = k_cache.shape
    hb = min(8, nh)
    assert t & (t - 1) == 0 and hb & (hb - 1) == 0 and (p_len * hb) % LANES == 0 and LANES % hb == 0
    w = hb * hd
    lc = c_all.shape[-1]
    c_flat = c_all[:, :, :p_len].reshape(nb, nh // hb, hb, p_len).transpose(0, 1, 3, 2)
    c_flat = c_flat.reshape(nb, nh // hb, 1, p_len * hb)
    cache_spec = pl.BlockSpec((None, None, p_len, hb, hd), lambda b, h: (layer, b, 0, h, 0))
    row_spec = pl.BlockSpec((t, w), lambda b, h: (b, h))
    return pl.pallas_call(
        functools.partial(_fox_sample_kernel, hb=hb, hd=hd, p_len=p_len, t=t),
        grid=(nb, nh // hb),
        in_specs=[row_spec, cache_spec, cache_spec, row_spec, row_spec,
                  pl.BlockSpec((None, hb, lc), lambda b, h: (b, h, 0)),
                  pl.BlockSpec((None, None, 1, p_len * hb), lambda b, h: (b, h, 0, 0))],
        out_specs=row_spec,
        out_shape=jax.ShapeDtypeStruct((nb * t, nh * hd), BF16),
        compiler_params=_cp(2), name="fox_sample_attention",
    )(q, k_cache, v_cache, k_new, v_new, c_all, c_flat)


def _proj_res_kernel(a_ref, w_ref, x_ref, gate_ref, o_ref):
    o_ref[...] = x_ref[...] + gate_ref[...] * _dot(a_ref[...], w_ref[...])


def _proj_res(a, w, x, gate, *, layer, tm, tpb):
    m, k = a.shape
    n = w.shape[2]
    return pl.pallas_call(
        _proj_res_kernel, grid=(m // tm, 1),
        in_specs=[pl.BlockSpec((tm, k), lambda i, j: (i, 0)),
                  pl.BlockSpec((None, k, n), lambda i, j: (layer, 0, 0)),
                  pl.BlockSpec((tm, n), lambda i, j: (i, 0)),
                  _mod_spec(gate, tpb)],
        out_specs=pl.BlockSpec((tm, n), lambda i, j: (i, 0)),
        out_shape=jax.ShapeDtypeStruct((m, n), F32),
        compiler_params=_cp(2), name="attn_out_proj",
    )(a, w, x, gate)


def _ffn_kernel(x_ref, g_ref, sh_ref, sc_ref, gate_ref, wu_ref, wd_ref, *rest, final):
    if final:
        fg_ref, o_ref, h_ref, acc_ref = rest
    else:
        o_ref, h_ref, acc_ref = rest
    f = pl.program_id(1)

    @pl.when(f == 0)
    def _():
        _norm_mod_store(h_ref, x_ref, g_ref, sh_ref, sc_ref)
        acc_ref[...] = jnp.zeros(acc_ref.shape, F32)

    a = jnp.maximum(_dot(h_ref[...], wu_ref[...]), 0.0)
    acc_ref[...] += _dot((a * a).astype(BF16), wd_ref[...])

    @pl.when(f == pl.num_programs(1) - 1)
    def _():
        y = x_ref[...] + gate_ref[...] * acc_ref[...]
        if final:
            y = _rms(y) * fg_ref[...]
        o_ref[...] = y


def _ffn(x, g, shift, scale, gate, w_up, w_down, final_g, *, layer, tm, tpb):
    m, d = x.shape
    ff = w_up.shape[2]
    tf = min(1024, ff)
    vec = pl.BlockSpec((1, d), lambda i, f: (0, 0))
    in_specs = [pl.BlockSpec((tm, d), lambda i, f: (i, 0)), vec,
                _mod_spec(shift, tpb), _mod_spec(scale, tpb), _mod_spec(gate, tpb),
                pl.BlockSpec((None, d, tf), lambda i, f: (layer, 0, f)),
                pl.BlockSpec((None, tf, d), lambda i, f: (layer, f, 0))]
    args = [x, g, shift, scale, gate, w_up, w_down]
    if final_g is not None:
        in_specs.append(vec)
        args.append(final_g)
    return pl.pallas_call(
        functools.partial(_ffn_kernel, final=final_g is not None),
        grid=(m // tm, ff // tf), in_specs=in_specs,
        out_specs=pl.BlockSpec((tm, d), lambda i, f: (i, 0)),
        out_shape=jax.ShapeDtypeStruct((m, d), F32),
        scratch_shapes=[pltpu.VMEM((tm, d), BF16), pltpu.VMEM((tm, d), F32)],
        compiler_params=_cp(2), name="ffn_final" if final_g is not None else "ffn",
    )(*args)


def _rope(r, a, b, c):
    half_shift = (LANES // 8) // 2
    cols = []
    for j in range(r.shape[1] // LANES):
        x = r[:, j * LANES:(j + 1) * LANES]
        cols.append(x * a + pltpu.roll(x, LANES - half_shift, 1) * b + pltpu.roll(x, half_shift, 1) * c)
    return jnp.concatenate(cols, axis=1)


def _dup_halves(x, hd):
    lo = lax.broadcasted_iota(jnp.int32, (x.shape[0], LANES), 1) < hd
    zero = jnp.zeros((x.shape[0], LANES), x.dtype)
    out = []
    for j in range(x.shape[1] // LANES):
        p = x[:, j * LANES:(j + 1) * LANES]
        r = pltpu.roll(p, hd, 1)
        out += [jnp.where(lo, p, zero), jnp.where(lo, zero, r), jnp.where(lo, r, zero), jnp.where(lo, zero, p)]
    return jnp.concatenate(out, axis=1)


def _swa_in_kernel(x_ref, g_ref, sh_ref, sc_ref, w_ref, ra_ref, rb_ref, rc_ref,
                   q_ref, kk_ref, vv_ref, k_ref, v_ref, h_ref, *, kvd, hd, q_scale):
    _norm_mod_store(h_ref, x_ref, g_ref, sh_ref, sc_ref)
    h = h_ref[...]
    tabs = (ra_ref[...], rb_ref[...], rc_ref[...])
    d = q_ref.shape[1]
    tn = 2 * kvd
    for n in range(d // tn):
        r = _dot(h, w_ref[:, n * tn:(n + 1) * tn])
        q_ref[:, n * tn:(n + 1) * tn] = (_rope(r, *tabs) * q_scale).astype(BF16)
    r = _dot(h, w_ref[:, d:])
    k = _rope(r[:, :kvd], *tabs)
    v = r[:, kvd:]
    k_ref[...] = k
    v_ref[...] = v
    kk_ref[...] = _dup_halves(k, hd).astype(BF16)
    vv_ref[...] = _dup_halves(v, hd).astype(BF16)


def _swa_in(x, g, shift, scale, w, tabs, *, layer, tm, tpb, kvd, hd):
    m, d = x.shape
    ntab = tabs[0].shape[0] // tm
    kkw = (kvd // hd) * 2 * LANES
    tab_spec = pl.BlockSpec((tm, LANES), lambda i, n: (i % ntab, 0))
    const = lambda w_: pl.BlockSpec((tm, w_), lambda i, n: (i, 0))
    return pl.pallas_call(
        functools.partial(_swa_in_kernel, kvd=kvd, hd=hd, q_scale=hd ** -0.5 * LOG2E),
        grid=(m // tm, 1),
        in_specs=[pl.BlockSpec((tm, d), lambda i, n: (i, 0)),
                  pl.BlockSpec((1, d), lambda i, n: (0, 0)),
                  _mod_spec(shift, tpb), _mod_spec(scale, tpb),
                  pl.BlockSpec((None, d, d + 2 * kvd), lambda i, n: (layer, 0, 0)),
                  tab_spec, tab_spec, tab_spec],
        out_specs=[const(d), const(kkw), const(kkw), const(kvd), const(kvd)],
        out_shape=[jax.ShapeDtypeStruct((m, d), BF16),
                   jax.ShapeDtypeStruct((m, kkw), BF16), jax.ShapeDtypeStruct((m, kkw), BF16),
                   jax.ShapeDtypeStruct((m, kvd), F32), jax.ShapeDtypeStruct((m, kvd), F32)],
        scratch_shapes=[pltpu.VMEM((tm, d), BF16)],
        compiler_params=_cp(2), name="swa_in_proj",
    )(x, g, shift, scale, w, *tabs)


def _rope_tables(pos, hd, rope_dims):
    half = rope_dims // 2
    inv_freq = ROPE_THETA ** (-jnp.arange(half, dtype=F32) * 2.0 / rope_dims)
    ang = pos.astype(F32)[:, None] * inv_freq[None, :]
    cos, sin = jnp.cos(ang), jnp.sin(ang)
    lane = np.arange(LANES) % hd
    idx = lane % half
    a = jnp.where(lane < rope_dims, cos[:, idx], 1.0)
    b = jnp.where(lane < half, -sin[:, idx], 0.0)
    c = jnp.where((lane >= half) & (lane < rope_dims), sin[:, idx], 0.0)
    return a, b, c


def _swa_core(q, k_top, k_bot, v_top, v_bot, bias, sinks, *, group):
    t = q.shape[0]
    npair = group // 2
    win = k_top.shape[0]
    half_lanes = lax.broadcasted_iota(jnp.int32, (win, LANES), 1) < LANES // 2
    count_even = jnp.where(half_lanes, 1.0, 0.0).astype(BF16)
    count_odd = jnp.where(half_lanes, 0.0, 1.0).astype(BF16)
    v2 = jnp.concatenate([jnp.concatenate([v_top, count_even], axis=1),
                          jnp.concatenate([v_bot, count_odd], axis=1)], axis=0)
    qs = jnp.concatenate([q[:, p * LANES:(p + 1) * LANES] for p in range(npair)], axis=0)
    s = _dot_nt(qs, jnp.concatenate([k_top, k_bot], axis=0)) + bias
    ps, es = [], []
    for half in range(2):
        cols = [s[:, half * win + c * LANES: half * win + (c + 1) * LANES] for c in range(win // LANES)]
        sink = sinks[:, half * LANES:(half + 1) * LANES]
        m = jnp.maximum(jnp.max(functools.reduce(jnp.maximum, cols), axis=-1, keepdims=True), sink)
        ps += [jnp.exp2(col - m) for col in cols]
        es.append(jnp.exp2(sink - m))
    o = _dot(jnp.concatenate(ps, axis=1).astype(BF16), v2)
    lane = lax.broadcasted_iota(jnp.int32, (o.shape[0], LANES), 1)
    o = o[:, :LANES] / (o[:, LANES:] + jnp.where(lane < LANES // 2, es[0], es[1]))
    return jnp.concatenate([o[p * t:(p + 1) * t] for p in range(npair)], axis=1)


def _swa_attn_kernel(q_ref, kh_ref, km_ref, vh_ref, vm_ref, bias_ref, sink_ref, o_ref, *, kv, group, hd):
    gw = group * hd
    bias = bias_ref[...]
    for g in range(kv):
        def win(halo, main, off):
            sl = slice(g * 2 * LANES + off, g * 2 * LANES + off + LANES)
            return jnp.concatenate([halo[:, sl], main[:, sl]], axis=0)
        o = _swa_core(q_ref[:, g * gw:(g + 1) * gw],
                      win(kh_ref, km_ref, 0), win(kh_ref, km_ref, LANES),
                      win(vh_ref, vm_ref, 0), win(vh_ref, vm_ref, LANES),
                      bias, sink_ref[g], group=group)
        o_ref[:, g * gw:(g + 1) * gw] = o.astype(o_ref.dtype)


def _swa_attn(q, kk, vv, bias2, sink_rows, *, nb, s, kv, group, hd):
    t = 2 * CHUNK
    nt = s // t
    d = q.shape[1]
    kkw = kk.shape[1]
    main = pl.BlockSpec((t, kkw), lambda b, i: (b * nt + i, 0))
    halo = pl.BlockSpec((t, kkw), lambda b, i: (b * nt + jnp.maximum(i - 1, 0), 0))
    return pl.pallas_call(
        functools.partial(_swa_attn_kernel, kv=kv, group=group, hd=hd),
        grid=(nb, nt),
        in_specs=[pl.BlockSpec((t, d), lambda b, i: (b * nt + i, 0)),
                  halo, main, halo, main,
                  pl.BlockSpec((None,) + bias2.shape[1:], lambda b, i: (jnp.minimum(i, 1), 0, 0)),
                  pl.BlockSpec(sink_rows.shape, lambda b, i: (0, 0, 0))],
        out_specs=pl.BlockSpec((t, d), lambda b, i: (b * nt + i, 0)),
        out_shape=jax.ShapeDtypeStruct(q.shape, BF16),
        compiler_params=_cp(2), name="swa_attention",
    )(q, kk, kk, vv, vv, bias2, sink_rows)


def _swa_sample_kernel(q_ref, kc_ref, vc_ref, kkn_ref, vvn_ref, bias_ref, sink_ref, o_ref, *, kv, group, hd, t):
    gw = group * hd
    kkc = _dup_halves(kc_ref[...], hd).astype(BF16)
    vvc = _dup_halves(vc_ref[...], hd).astype(BF16)
    pad = jnp.zeros((kc_ref.shape[0] - t, LANES), BF16)
    for g in range(kv):
        def win(cache, new, off):
            sl = slice(g * 2 * LANES + off, g * 2 * LANES + off + LANES)
            return jnp.concatenate([cache[:, sl], new[:, sl], pad], axis=0)
        o = _swa_core(q_ref[:, g * gw:(g + 1) * gw],
                      win(kkc, kkn_ref, 0), win(kkc, kkn_ref, LANES),
                      win(vvc, vvn_ref, 0), win(vvc, vvn_ref, LANES),
                      bias_ref[...], sink_ref[g], group=group)
        o_ref[:, g * gw:(g + 1) * gw] = o.astype(o_ref.dtype)


def _swa_sample_attn(q, k_cache, v_cache, kk_new, vv_new, bias, sink_rows, *, kv, group, hd, t):
    nb, buf, kvd = k_cache.shape
    d = q.shape[1]
    full = lambda a: pl.BlockSpec(a.shape, lambda b: (0,) * a.ndim)
    cache_spec = pl.BlockSpec((None, buf, kvd), lambda b: (b, 0, 0))
    new_spec = pl.BlockSpec((t, kk_new.shape[1]), lambda b: (b, 0))
    return pl.pallas_call(
        functools.partial(_swa_sample_kernel, kv=kv, group=group, hd=hd, t=t),
        grid=(nb,),
        in_specs=[pl.BlockSpec((t, d), lambda b: (b, 0)), cache_spec, cache_spec, new_spec, new_spec,
                  full(bias), full(sink_rows)],
        out_specs=pl.BlockSpec((t, d), lambda b: (b, 0)),
        out_shape=jax.ShapeDtypeStruct(q.shape, BF16),
        compiler_params=_cp(1), name="swa_sample_attention",
    )(q, k_cache, v_cache, kk_new, vv_new, bias, sink_rows)


def _window_bias(valid, npair):
    b = np.where(valid, 0.0, NEG).astype(np.float32)
    return np.tile(b, (npair, 2))


def _prompt_bias(npair):
    t = 2 * CHUNK
    qc = np.arange(t)[:, None] // CHUNK
    kc = np.arange(t + SWA_WINDOW_CHUNKS * CHUNK)[None, :] // CHUNK
    valid = (kc >= qc) & (kc <= qc + SWA_WINDOW_CHUNKS)
    first = valid & (kc >= SWA_WINDOW_CHUNKS)
    return jnp.asarray(np.stack([_window_bias(first, npair), _window_bias(valid, npair)]))


def _sample_bias(past_len, buf, t, npair):
    q_pos = past_len + np.arange(t)
    k_pos = np.concatenate([past_len - buf + np.arange(buf), q_pos])
    qch, kch = q_pos // CHUNK, k_pos // CHUNK
    valid = np.zeros((t, 2 * buf), bool)
    valid[:, :buf + t] = (kch[None, :] <= qch[:, None]) & (kch[None, :] >= qch[:, None] - SWA_WINDOW_CHUNKS)
    return jnp.asarray(_window_bias(valid, npair))


def _sink_rows(sinks, kv, group, t):
    s = (sinks * LOG2E).reshape(kv, group // 2, 1, 2, 1)
    return jnp.broadcast_to(s, (kv, group // 2, t, 2, LANES)).reshape(kv, (group // 2) * t, 2 * LANES)


def kernel(x_prompt, x_sample, c_prompt, c_sample, cache_fox_k, cache_fox_v, cache_fox_logf, cache_swa_k,
           cache_swa_v, ada_w, ada_b, norm_mix_g, norm_ffn_g, fox_w_in, fox_b_f, fox_w_out, swa_w_in,
           swa_sinks, swa_w_out, ffn_w_up, ffn_w_down, final_g):
    bp, s, d = x_prompt.shape
    bs, t, _ = x_sample.shape
    depth = ada_w.shape[0]
    past_len = cache_fox_k.shape[2]
    nh_fox, hd_fox = cache_fox_k.shape[3], cache_fox_k.shape[4]
    buf, kv, hd_swa = cache_swa_k.shape[2], cache_swa_k.shape[3], cache_swa_k.shape[4]
    nh_swa = swa_sinks.shape[1]
    group = nh_swa // kv
    kvd = kv * hd_swa
    rope_dims = hd_swa // 4
    assert hd_fox == LANES and hd_swa == LANES // 2 and rope_dims == 16 and group % 2 == 0
    assert buf == SWA_WINDOW_CHUNKS * CHUNK and t <= buf and s % (2 * CHUNK) == 0

    mp, ms = bp * s, bs * t
    tm_p = min(512, s)
    tpb_p = s // tm_p
    tm_fox = min(1024, s)
    tk = min(FOX_TK, s)
    tq = min(FOX_NSUB * tk, s)

    mods = _ada(jnp.concatenate([c_prompt, c_sample], axis=0), ada_w, ada_b)

    def split_mods(i):
        six = jnp.split(mods[i], 6, axis=-1)
        prompt = [m[:bp, None, :] for m in six]
        sample = [jnp.repeat(m[bp:], t, axis=0)[None] for m in six]
        return prompt, sample

    row = lambda v: v.reshape(1, -1)
    xp = x_prompt.reshape(mp, d)
    xs = x_sample.reshape(ms, d)
    pos_p = jnp.arange(s)
    pos_s = past_len + jnp.arange(t)
    tabs_p = _rope_tables(pos_p, hd_swa, rope_dims)
    tabs_s = tuple(jnp.tile(a, (bs, 1)) for a in _rope_tables(pos_s, hd_swa, rope_dims))
    bias_p = _prompt_bias(group // 2)
    bias_s = _sample_bias(past_len, buf, t, group // 2)

    fox_w_in_b, fox_w_out_b = fox_w_in.astype(BF16), fox_w_out.astype(BF16)
    swa_w_in_b, swa_w_out_b = swa_w_in.astype(BF16), swa_w_out.astype(BF16)
    w_up_b, w_down_b = ffn_w_up.astype(BF16), ffn_w_down.astype(BF16)

    outs = {k: [] for k in ("fkp", "fvp", "flp", "fks", "fvs", "fls", "skp", "svp", "sks", "svs")}
    for i in range(depth):
        mod_p, mod_s = split_mods(i)
        j = i // 2
        if i % 2 == 0:
            w_f = jnp.pad(fox_w_in[j, :, 3 * d:], ((0, 0), (0, LANES - nh_fox))).astype(BF16)
            b_f = row(fox_b_f[j])
            qh, kh, vh, k32, v32, lf = _fox_in(xp, row(norm_mix_g[i]), mod_p[0], mod_p[1], fox_w_in_b, w_f, b_f,
                                               layer=j, tm=tm_fox, tpb=s // tm_fox, head_major=True,
                                               nh=nh_fox, hd=hd_fox)
            lft = lf.reshape(bp, s, nh_fox).transpose(0, 2, 1)
            c = _cumsum_rows(lft.reshape(-1, LANES), s // LANES).reshape(bp * nh_fox, s)
            first = _fox_first_blocks(c, _max_row_norm(qh), _max_row_norm(kh), tk)
            op = _fox_attn(qh, kh, vh, c, first, tk=tk, nsub=tq // tk)
            xp = _proj_res(op.reshape(mp, d), fox_w_out_b, xp, mod_p[2], layer=j, tm=tm_p, tpb=tpb_p)
            outs["fkp"].append(k32.reshape(bp, s, nh_fox, hd_fox))
            outs["fvp"].append(v32.reshape(bp, s, nh_fox, hd_fox))
            outs["flp"].append(lf.reshape(bp, s, nh_fox))
            qs_, k32s, v32s, lfs = _fox_in(xs, row(norm_mix_g[i]), mod_s[0], mod_s[1], fox_w_in_b, w_f, b_f,
                                           layer=j, tm=ms, tpb=1, head_major=False, nh=nh_fox, hd=hd_fox)
            lf_all = jnp.concatenate([cache_fox_logf[j], lfs.reshape(bs, t, nh_fox)], axis=1)
            lc = LANES * int(2 ** np.ceil(np.log2(-(-(past_len + t) // LANES))))
            lf_all = jnp.pad(lf_all.transpose(0, 2, 1), ((0, 0), (0, 0), (0, lc - past_len - t)))
            c_s = _cumsum_rows(lf_all.reshape(-1, LANES), lc // LANES).reshape(bs, nh_fox, lc)
            os_ = _fox_sample_attn(qs_, cache_fox_k, cache_fox_v, k32s, v32s, c_s, layer=j, t=t)
            xs = _proj_res(os_, fox_w_out_b, xs, mod_s[2], layer=j, tm=ms, tpb=1)
            outs["fks"].append(k32s.reshape(bs, t, nh_fox, hd_fox))
            outs["fvs"].append(v32s.reshape(bs, t, nh_fox, hd_fox))
            outs["fls"].append(lfs.reshape(bs, t, nh_fox))
        else:
            q, kk, vv, k32, v32 = _swa_in(xp, row(norm_mix_g[i]), mod_p[0], mod_p[1], swa_w_in_b, tabs_p,
                                          layer=j, tm=tm_p, tpb=tpb_p, kvd=kvd, hd=hd_swa)
            op = _swa_attn(q, kk, vv, bias_p, _sink_rows(swa_sinks[j], kv, group, 2 * CHUNK),
                           nb=bp, s=s, kv=kv, group=group, hd=hd_swa)
            xp = _proj_res(op, swa_w_out_b, xp, mod_p[2], layer=j, tm=tm_p, tpb=tpb_p)
            outs["skp"].append(k32.reshape(bp, s, kvd)[:, -buf:].reshape(bp, buf, kv, hd_swa))
            outs["svp"].append(v32.reshape(bp, s, kvd)[:, -buf:].reshape(bp, buf, kv, hd_swa))
            q, kk, vv, k32, v32 = _swa_in(xs, row(norm_mix_g[i]), mod_s[0], mod_s[1], swa_w_in_b, tabs_s,
                                          layer=j, tm=ms, tpb=1, kvd=kvd, hd=hd_swa)
            os_ = _swa_sample_attn(q, cache_swa_k[j].reshape(bs, buf, kvd), cache_swa_v[j].reshape(bs, buf, kvd),
                                   kk, vv, bias_s, _sink_rows(swa_sinks[j], kv, group, t),
                                   kv=kv, group=group, hd=hd_swa, t=t)
            xs = _proj_res(os_, swa_w_out_b, xs, mod_s[2], layer=j, tm=ms, tpb=1)
            k_all = jnp.concatenate([cache_swa_k[j], k32.reshape(bs, t, kv, hd_swa)], axis=1)
            v_all = jnp.concatenate([cache_swa_v[j], v32.reshape(bs, t, kv, hd_swa)], axis=1)
            outs["sks"].append(k_all[:, -buf:])
            outs["svs"].append(v_all[:, -buf:])
        fg = row(final_g) if i == depth - 1 else None
        xp = _ffn(xp, row(norm_ffn_g[i]), mod_p[3], mod_p[4], mod_p[5], w_up_b, w_down_b, fg,
                  layer=i, tm=tm_p, tpb=tpb_p)
        xs = _ffn(xs, row(norm_ffn_g[i]), mod_s[3], mod_s[4], mod_s[5], w_up_b, w_down_b, fg,
                  layer=i, tm=ms, tpb=1)

    st = lambda k: jnp.stack(outs[k])
    return (xp.reshape(bp, s, d), xs.reshape(bs, t, d),
            st("fkp"), st("fvp"), st("flp"), st("fks"), st("fvs"), st("fls"),
            st("skp"), st("svp"), st("sks"), st("svs"))
```

```python
import functools

import numpy as np
import jax
import jax.numpy as jnp
from jax import lax
from jax.experimental import pallas as pl
from jax.experimental.pallas import tpu as pltpu

F32 = jnp.float32
BF16 = jnp.bfloat16

RMS_EPS = 1e-6
CHUNK = 64
SWA_WINDOW_CHUNKS = 2
ROPE_THETA = 500000.0
LANES = 128
NEG = -1e30
NEG_BLOCK = -3e38
VMEM_LIMIT_BYTES = 56 * 1024 * 1024
LOG2E = 1.4426950408889634
FOX_SKIP_LOG2 = 150.0
FFN_STEP_ELEMS = 512 * 1024
FOX_TK = 512
FOX_NSUB = 8


def _cp(n_axes):
    return pltpu.CompilerParams(dimension_semantics=("arbitrary",) * n_axes,
                                vmem_limit_bytes=VMEM_LIMIT_BYTES)


def _dot(a, b):
    return jnp.dot(a, b, preferred_element_type=F32)


def _dot_nt(a, b):
    return lax.dot_general(a, b, (((1,), (1,)), ((), ())), preferred_element_type=F32)


def _rms(x):
    return x * lax.rsqrt(jnp.mean(x * x, axis=-1, keepdims=True) + RMS_EPS)


NORM_ROWS = 16


def _norm_mod_store(h_ref, x_ref, g_ref, sh_ref, sc_ref):
    per_token = sh_ref.shape[0] > 1
    g = g_ref[...]
    if not per_token:
        gain, shift = g * (1.0 + sc_ref[...]), sh_ref[...]

    def body(i, carry):
        rows = pl.ds(pl.multiple_of(i * NORM_ROWS, NORM_ROWS), NORM_ROWS)
        x = x_ref[rows, :]
        if per_token:
            gain_, shift_ = g * (1.0 + sc_ref[rows, :]), sh_ref[rows, :]
        else:
            gain_, shift_ = gain, shift
        h_ref[rows, :] = (_rms(x) * gain_ + shift_).astype(h_ref.dtype)
        return carry

    lax.fori_loop(0, x_ref.shape[0] // NORM_ROWS, body, 0, unroll=4)


def _log_sigmoid(z):
    return jnp.minimum(z, 0.0) - jnp.log1p(jnp.exp(-jnp.abs(z)))


def _ada_kernel(c_ref, w_ref, b_ref, o_ref):
    c = c_ref[...]
    a = (c / (1.0 + jnp.exp(-c))).astype(BF16)
    o_ref[...] = _dot(a, w_ref[...].astype(BF16)) + b_ref[...]


def _ada(c_all, ada_w, ada_b):
    depth, d, n6 = ada_w.shape
    r = c_all.shape[0]
    tn = min(1024, n6)
    return pl.pallas_call(
        _ada_kernel,
        grid=(depth, n6 // tn),
        in_specs=[pl.BlockSpec((r, d), lambda l, n: (0, 0)),
                  pl.BlockSpec((None, d, tn), lambda l, n: (l, 0, n)),
                  pl.BlockSpec((None, 1, tn), lambda l, n: (l, 0, n))],
        out_specs=pl.BlockSpec((None, r, tn), lambda l, n: (l, 0, n)),
        out_shape=jax.ShapeDtypeStruct((depth, r, n6), F32),
        compiler_params=_cp(2), name="ada_params",
    )(c_all, ada_w, ada_b.reshape(depth, 1, n6))


def _fox_in_kernel(x_ref, g_ref, sh_ref, sc_ref, wq_ref, wk_ref, wv_ref, wf_ref, bf_ref, *rest,
                   head_major, hd, nh, q_scale):
    if head_major:
        qb_ref, kb_ref, vb_ref, k_ref, v_ref, lf_ref, qn_ref, kn_ref, h_ref = rest
    else:
        qb_ref, k_ref, v_ref, lf_ref, h_ref = rest

    @pl.when(pl.program_id(1) == 0)
    def _():
        _norm_mod_store(h_ref, x_ref, g_ref, sh_ref, sc_ref)
        z = _dot(h_ref[...], wf_ref[...])[:, :nh] + bf_ref[...]
        lf_ref[...] = _log_sigmoid(z)

    hb = h_ref[...]
    q = _dot(hb, wq_ref[...]) * q_scale
    k = _dot(hb, wk_ref[...])
    v = _dot(hb, wv_ref[...])
    k_ref[...] = k
    v_ref[...] = v
    if head_major:
        ones = jnp.ones((hd, hd), BF16)

        def max_row_norm2(x):
            n2 = _dot((x * x).astype(BF16), ones)
            return jnp.max(n2.reshape(x.shape[0] // 8, 8, hd), axis=0)

        qn, kn = [], []
        for j in range(q.shape[1] // hd):
            sl = slice(j * hd, (j + 1) * hd)
            qb_ref[j] = q[:, sl].astype(BF16)
            kb_ref[j] = k[:, sl].astype(BF16)
            vb_ref[j] = v[:, sl].astype(BF16)
            qn.append(max_row_norm2(q[:, sl]))
            kn.append(max_row_norm2(k[:, sl]))
        qn_ref[...] = jnp.concatenate(qn, axis=1)
        kn_ref[...] = jnp.concatenate(kn, axis=1)
    else:
        qb_ref[...] = q.astype(BF16)


def _mod_spec(mod, tpb, cols=None):
    _, rows, d = mod.shape
    if cols is None:
        return pl.BlockSpec((None, rows, d), lambda i, n: (i // tpb, 0, 0))
    return pl.BlockSpec((None, rows, cols), lambda i, n: (i // tpb, 0, n))


def _fox_in(x, g, shift, scale, w_in, w_f, b_f, *, layer, tm, tpb, head_major, nh, hd):
    m, d = x.shape
    tn = min(512, d)
    nt = d // tn
    hpt = tn // hd
    nb, s = m // (tm * tpb), tm * tpb
    in_specs = [pl.BlockSpec((tm, d), lambda i, n: (i, 0)),
                pl.BlockSpec((1, d), lambda i, n: (0, 0)),
                _mod_spec(shift, tpb), _mod_spec(scale, tpb),
                pl.BlockSpec((None, d, tn), lambda i, n: (layer, 0, n)),
                pl.BlockSpec((None, d, tn), lambda i, n: (layer, 0, n + nt)),
                pl.BlockSpec((None, d, tn), lambda i, n: (layer, 0, n + 2 * nt)),
                pl.BlockSpec((d, LANES), lambda i, n: (0, 0)),
                pl.BlockSpec((1, nh), lambda i, n: (0, 0))]
    rm_spec = pl.BlockSpec((tm, tn), lambda i, n: (i, n))
    lf_spec = pl.BlockSpec((tm, nh), lambda i, n: (i, 0))
    if head_major:
        hm_spec = pl.BlockSpec((None, hpt, tm, hd), lambda i, n: (i // tpb, n, i % tpb, 0))
        hm_shape = jax.ShapeDtypeStruct((nb, nh, s, hd), BF16)
        n2_spec = pl.BlockSpec((8, tn), lambda i, n: (i, n))
        out_specs = [hm_spec, hm_spec, hm_spec, rm_spec, rm_spec, lf_spec, n2_spec, n2_spec]
        out_shape = [hm_shape, hm_shape, hm_shape]
    else:
        out_specs = [rm_spec, rm_spec, rm_spec, lf_spec]
        out_shape = [jax.ShapeDtypeStruct((m, d), BF16)]
    out_shape += [jax.ShapeDtypeStruct((m, d), F32), jax.ShapeDtypeStruct((m, d), F32),
                  jax.ShapeDtypeStruct((m, nh), F32)]
    if head_major:
        out_shape += [jax.ShapeDtypeStruct((m // tm * 8, d), F32)] * 2
    return pl.pallas_call(
        functools.partial(_fox_in_kernel, head_major=head_major, hd=hd, nh=nh, q_scale=hd ** -0.5 * LOG2E),
        grid=(m // tm, nt), in_specs=in_specs, out_specs=out_specs, out_shape=out_shape,
        scratch_shapes=[pltpu.VMEM((tm, d), BF16)],
        compiler_params=_cp(2), name="fox_in_proj",
    )(x, g, shift, scale, w_in, w_in, w_in, w_f, b_f)


def _split3(x):
    x1 = x.astype(BF16)
    r1 = x - x1.astype(F32)
    x2 = r1.astype(BF16)
    x3 = (r1 - x2.astype(F32)).astype(BF16)
    return x1, x2, x3


def _cumsum_kernel(x_ref, u_ref, l_ref, o_ref):
    u = u_ref[...]
    lo = l_ref[...]
    y = sum(_dot(p, u) for p in _split3(x_ref[...]))
    tot = jnp.broadcast_to(y[:, LANES - 1:LANES], y.shape)
    o_ref[...] = (y + sum(_dot(lo, p) for p in _split3(tot))) * LOG2E


def _cumsum_rows(x, nc):
    rows = x.shape[0]
    assert LANES % nc == 0 or nc % LANES == 0
    n = max(nc, LANES)
    assert rows % n == 0
    u = jnp.asarray(np.triu(np.ones((LANES, LANES), np.float32)), BF16)
    idx = np.arange(n)
    same = (idx[:, None] // nc) == (idx[None, :] // nc)
    lo = jnp.asarray((same & (idx[None, :] < idx[:, None])).astype(np.float32), BF16)
    return pl.pallas_call(
        _cumsum_kernel, grid=(rows // n,),
        in_specs=[pl.BlockSpec((n, LANES), lambda i: (i, 0)),
                  pl.BlockSpec((LANES, LANES), lambda i: (0, 0)),
                  pl.BlockSpec((n, n), lambda i: (0, 0))],
        out_specs=pl.BlockSpec((n, LANES), lambda i: (i, 0)),
        out_shape=jax.ShapeDtypeStruct(x.shape, F32),
        compiler_params=_cp(1), name="logf_cumsum",
    )(x, u, lo)


NORM_SLACK = 1.01


def _head_norm_bound(n2, nb, nh, hd):
    n2 = n2.reshape(nb, -1, nh, hd)
    return (NORM_SLACK * jnp.sqrt(jnp.max(n2, axis=(1, 3)))).reshape(-1)


def _fox_first_blocks(c, qmax, kmax, tk):
    bh, s = c.shape
    cb = c.reshape(bh, s // tk, tk)
    c_start, c_end = cb[:, :, 0], cb[:, :, -1]
    budget = (2.0 * qmax * kmax + FOX_SKIP_LOG2)[:, None, None]
    skippable = (c_end[:, None, :] - c_start[:, :, None]) > budget
    earlier = np.tril(np.ones((s // tk, s // tk), bool), -1)
    return jnp.sum(skippable & earlier, axis=-1).astype(jnp.int32).reshape(-1)


def _fox_attn_kernel(first_ref, q_ref, k_ref, v_ref, c_ref, o_ref, m_ref, acc_ref, *, tk, nsub, hd):
    b, h, qi = pl.program_id(0), pl.program_id(1), pl.program_id(2)
    nkb = pl.num_programs(2) * nsub
    m_ref[...] = jnp.full(m_ref.shape, NEG, F32)
    acc_ref[...] = jnp.zeros(acc_ref.shape, F32)
    ones = jnp.ones((tk, hd), BF16)
    ncol = tk // LANES

    def step(r, j, mask=None, valid=None):
        rows = pl.ds(pl.multiple_of(j * tk, tk), tk)
        c_row = c_ref[j]
        if valid is not None:
            c_row = jnp.where(valid, c_row, -NEG_BLOCK)
        s = _dot_nt(q_ref[r * tk:(r + 1) * tk, :], k_ref[rows, :]) - c_row
        if mask is not None:
            s = jnp.where(mask, s, NEG)
        cols = [s[:, c * LANES:(c + 1) * LANES] for c in range(ncol)]
        m_cur = functools.reduce(jnp.maximum, cols)
        m_prev = m_ref[r]
        m_next = jnp.maximum(m_prev, jnp.max(m_cur, axis=-1, keepdims=True))
        alpha = jnp.exp2(m_prev - m_next)
        p = jnp.concatenate([jnp.exp2(col - m_next) for col in cols], axis=1).astype(BF16)
        pv = _dot(p, jnp.concatenate([v_ref[rows, :], ones], axis=1))
        acc_ref[r] = jnp.concatenate([alpha, alpha], axis=1) * acc_ref[r] + pv
        m_ref[r] = m_next

    base = qi * nsub
    off = (b * pl.num_programs(1) + h) * nkb + base
    n = functools.reduce(jnp.maximum, [base + r - first_ref[off + r] for r in range(nsub)])

    def body(t, carry):
        for r in range(nsub):
            j = base + r - n + t
            step(r, jnp.maximum(j, 0), valid=j >= 0)
        return carry

    lax.fori_loop(0, n, body, 0)
    row = lax.broadcasted_iota(jnp.int32, (tk, tk), 0)
    col = lax.broadcasted_iota(jnp.int32, (tk, tk), 1)
    causal = col <= row
    for r in range(nsub):
        step(r, base + r, mask=causal)
    for r in range(nsub):
        acc = acc_ref[r]
        o_ref[r * tk:(r + 1) * tk, :] = (acc[:, :hd] / acc[:, hd:]).astype(o_ref.dtype)


def _fox_attn(qh, kh, vh, c, first, *, tk, nsub):
    nb, nh, s, hd = qh.shape
    tq = tk * nsub
    nq = s // tq
    kv_spec = pl.BlockSpec((None, None, s, hd), lambda b, h, i, f: (b, h, 0, 0))
    return pl.pallas_call(
        functools.partial(_fox_attn_kernel, tk=tk, nsub=nsub, hd=hd),
        grid_spec=pltpu.PrefetchScalarGridSpec(
            num_scalar_prefetch=1, grid=(nb, nh, nq),
            in_specs=[pl.BlockSpec((None, None, tq, hd), lambda b, h, i, f: (b, h, i, 0)),
                      kv_spec, kv_spec,
                      pl.BlockSpec((None, s // tk, 1, tk), lambda b, h, i, f: (b * nh + h, 0, 0, 0))],
            out_specs=pl.BlockSpec((None, tq, hd), lambda b, h, i, f: (b, i, h)),
            scratch_shapes=[pltpu.VMEM((nsub, tk, LANES), F32), pltpu.VMEM((nsub, tk, 2 * hd), F32)]),
        out_shape=jax.ShapeDtypeStruct((nb, s, nh * hd), BF16),
        compiler_params=_cp(3), name="fox_attention",
    )(first, qh, kh, vh, c.reshape(nb * nh, s // tk, 1, tk))


def _fox_sample_kernel(q_ref, kc_ref, vc_ref, kn_ref, vn_ref, c_ref, cf_ref, o_ref, *, hb, hd, p_len, t):
    n = p_len * hb
    kf = kc_ref[...].reshape(n, hd).astype(BF16)
    vf = vc_ref[...].reshape(n, hd).astype(BF16)
    heads = [slice(j * hd, (j + 1) * hd) for j in range(hb)]
    q8 = jnp.concatenate([q_ref[:, sl] for sl in heads], axis=0)
    row_head = lax.shift_right_logical(lax.broadcasted_iota(jnp.int32, (hb * t, LANES), 0), t.bit_length() - 1)
    col_head = lax.broadcasted_iota(jnp.int32, (hb * t, LANES), 1) & (hb - 1)
    own = jnp.where(row_head == col_head, 0.0, NEG)
    s = _dot_nt(q8, kf)
    cf = cf_ref[...]
    cols = [s[:, g * LANES:(g + 1) * LANES] - cf[:, g * LANES:(g + 1) * LANES] + own for g in range(n // LANES)]
    tri = lax.broadcasted_iota(jnp.int32, (t, t), 1) <= lax.broadcasted_iota(jnp.int32, (t, t), 0)
    s_n = jnp.concatenate(
        [jnp.where(tri, _dot_nt(q_ref[:, sl], kn_ref[:, sl].astype(BF16)) - c_ref[j:j + 1, p_len:p_len + t], NEG)
         for j, sl in enumerate(heads)], axis=0)
    m = jnp.maximum(jnp.max(functools.reduce(jnp.maximum, cols), axis=-1, keepdims=True),
                    jnp.max(s_n, axis=-1, keepdims=True))
    ps = [jnp.exp2(col - m) for col in cols]
    p_n = jnp.exp2(s_n - m)
    den = jnp.sum(functools.reduce(jnp.add, ps), axis=-1, keepdims=True) + jnp.sum(p_n, axis=-1, keepdims=True)
    o = _dot(jnp.concatenate(ps, axis=1).astype(BF16), vf)
    o_n = jnp.concatenate([_dot(p_n[j * t:(j + 1) * t].astype(BF16), vn_ref[:, sl].astype(BF16))
                           for j, sl in enumerate(heads)], axis=0)
    o = (o + o_n) / den
    for j, sl in enumerate(heads):
        o_ref[:, sl] = o[j * t:(j + 1) * t].astype(o_ref.dtype)


def _fox_sample_attn(q, k_cache, v_cache, k_new, v_new, c_all, *, layer, t):
    _, nb, p_len, nh, hd = k_cache.shape
    hb = min(8, nh)
    assert t & (t - 1) == 0 and hb & (hb - 1) == 0 and (p_len * hb) % LANES == 0 and LANES % hb == 0
    w = hb * hd
    lc = c_all.shape[-1]
    c_flat = c_all[:, :, :p_len].reshape(nb, nh // hb, hb, p_len).transpose(0, 1, 3, 2)
    c_flat = c_flat.reshape(nb, nh // hb, 1, p_len * hb)
    cache_spec = pl.BlockSpec((None, None, p_len, hb, hd), lambda b, h: (layer, b, 0, h, 0))
    row_spec = pl.BlockSpec((t, w), lambda b, h: (b, h))
    return pl.pallas_call(
        functools.partial(_fox_sample_kernel, hb=hb, hd=hd, p_len=p_len, t=t),
        grid=(nb, nh // hb),
        in_specs=[row_spec, cache_spec, cache_spec, row_spec, row_spec,
                  pl.BlockSpec((None, hb, lc), lambda b, h: (b, h, 0)),
                  pl.BlockSpec((None, None, 1, p_len * hb), lambda b, h: (b, h, 0, 0))],
        out_specs=row_spec,
        out_shape=jax.ShapeDtypeStruct((nb * t, nh * hd), BF16),
        compiler_params=_cp(2), name="fox_sample_attention",
    )(q, k_cache, v_cache, k_new, v_new, c_all, c_flat)


def _proj_res_kernel(a_ref, w_ref, x_ref, gate_ref, o_ref):
    o_ref[...] = x_ref[...] + gate_ref[...] * _dot(a_ref[...], w_ref[...])


def _proj_res(a, w, x, gate, *, layer, tm, tpb):
    m, k = a.shape
    n = w.shape[2]
    return pl.pallas_call(
        _proj_res_kernel, grid=(m // tm, 1),
        in_specs=[pl.BlockSpec((tm, k), lambda i, j: (i, 0)),
                  pl.BlockSpec((None, k, n), lambda i, j: (layer, 0, 0)),
                  pl.BlockSpec((tm, n), lambda i, j: (i, 0)),
                  _mod_spec(gate, tpb)],
        out_specs=pl.BlockSpec((tm, n), lambda i, j: (i, 0)),
        out_shape=jax.ShapeDtypeStruct((m, n), F32),
        compiler_params=_cp(2), name="attn_out_proj",
    )(a, w, x, gate)


def _ffn_kernel(x_ref, g_ref, sh_ref, sc_ref, gate_ref, wu_ref, wd_ref, *rest, final):
    if final:
        fg_ref, o_ref, h_ref = rest
    else:
        o_ref, h_ref = rest
    f = pl.program_id(1)

    @pl.when(f == 0)
    def _():
        _norm_mod_store(h_ref, x_ref, g_ref, sh_ref, sc_ref)
        o_ref[...] = jnp.zeros(o_ref.shape, F32)

    a = jnp.maximum(_dot(h_ref[...], wu_ref[...]), 0.0)
    o_ref[...] += _dot((a * a).astype(BF16), wd_ref[...])

    @pl.when(f == pl.num_programs(1) - 1)
    def _():
        y = x_ref[...] + gate_ref[...] * o_ref[...]
        if final:
            y = _rms(y) * fg_ref[...]
        o_ref[...] = y


def _ffn(x, g, shift, scale, gate, w_up, w_down, final_g, *, layer, tm, tpb):
    m, d = x.shape
    ff = w_up.shape[2]
    tf = min(ff, FFN_STEP_ELEMS // tm)
    vec = pl.BlockSpec((1, d), lambda i, f: (0, 0))
    in_specs = [pl.BlockSpec((tm, d), lambda i, f: (i, 0)), vec,
                _mod_spec(shift, tpb), _mod_spec(scale, tpb), _mod_spec(gate, tpb),
                pl.BlockSpec((None, d, tf), lambda i, f: (layer, 0, f)),
                pl.BlockSpec((None, tf, d), lambda i, f: (layer, f, 0))]
    args = [x, g, shift, scale, gate, w_up, w_down]
    if final_g is not None:
        in_specs.append(vec)
        args.append(final_g)
    return pl.pallas_call(
        functools.partial(_ffn_kernel, final=final_g is not None),
        grid=(m // tm, ff // tf), in_specs=in_specs,
        out_specs=pl.BlockSpec((tm, d), lambda i, f: (i, 0)),
        out_shape=jax.ShapeDtypeStruct((m, d), F32),
        scratch_shapes=[pltpu.VMEM((tm, d), BF16)],
        compiler_params=_cp(2), name="ffn_final" if final_g is not None else "ffn",
    )(*args)


def _rope(r, a, b, c):
    half_shift = (LANES // 8) // 2
    cols = []
    for j in range(r.shape[1] // LANES):
        x = r[:, j * LANES:(j + 1) * LANES]
        cols.append(x * a + pltpu.roll(x, LANES - half_shift, 1) * b + pltpu.roll(x, half_shift, 1) * c)
    return jnp.concatenate(cols, axis=1)


def _dup_halves(x, hd):
    lo = lax.broadcasted_iota(jnp.int32, (x.shape[0], LANES), 1) < hd
    zero = jnp.zeros((x.shape[0], LANES), x.dtype)
    out = []
    for j in range(x.shape[1] // LANES):
        p = x[:, j * LANES:(j + 1) * LANES]
        r = pltpu.roll(p, hd, 1)
        out += [jnp.where(lo, p, zero), jnp.where(lo, zero, r), jnp.where(lo, r, zero), jnp.where(lo, zero, p)]
    return jnp.concatenate(out, axis=1)


def _swa_in_kernel(x_ref, g_ref, sh_ref, sc_ref, w_ref, ra_ref, rb_ref, rc_ref,
                   q_ref, kk_ref, vv_ref, k_ref, v_ref, h_ref, *, kvd, hd, q_scale):
    _norm_mod_store(h_ref, x_ref, g_ref, sh_ref, sc_ref)
    h = h_ref[...]
    tabs = (ra_ref[...], rb_ref[...], rc_ref[...])
    d = q_ref.shape[1]
    tn = 2 * kvd
    for n in range(d // tn):
        r = _dot(h, w_ref[:, n * tn:(n + 1) * tn])
        q_ref[:, n * tn:(n + 1) * tn] = (_rope(r, *tabs) * q_scale).astype(BF16)
    r = _dot(h, w_ref[:, d:])
    k = _rope(r[:, :kvd], *tabs)
    v = r[:, kvd:]
    k_ref[...] = k
    v_ref[...] = v
    kk_ref[...] = _dup_halves(k, hd).astype(BF16)
    vv_ref[...] = _dup_halves(v, hd).astype(BF16)


def _swa_in(x, g, shift, scale, w, tabs, *, layer, tm, tpb, kvd, hd):
    m, d = x.shape
    ntab = tabs[0].shape[0] // tm
    kkw = (kvd // hd) * 2 * LANES
    tab_spec = pl.BlockSpec((tm, LANES), lambda i, n: (i % ntab, 0))
    const = lambda w_: pl.BlockSpec((tm, w_), lambda i, n: (i, 0))
    return pl.pallas_call(
        functools.partial(_swa_in_kernel, kvd=kvd, hd=hd, q_scale=hd ** -0.5 * LOG2E),
        grid=(m // tm, 1),
        in_specs=[pl.BlockSpec((tm, d), lambda i, n: (i, 0)),
                  pl.BlockSpec((1, d), lambda i, n: (0, 0)),
                  _mod_spec(shift, tpb), _mod_spec(scale, tpb),
                  pl.BlockSpec((None, d, d + 2 * kvd), lambda i, n: (layer, 0, 0)),
                  tab_spec, tab_spec, tab_spec],
        out_specs=[const(d), const(kkw), const(kkw), const(kvd), const(kvd)],
        out_shape=[jax.ShapeDtypeStruct((m, d), BF16),
                   jax.ShapeDtypeStruct((m, kkw), BF16), jax.ShapeDtypeStruct((m, kkw), BF16),
                   jax.ShapeDtypeStruct((m, kvd), F32), jax.ShapeDtypeStruct((m, kvd), F32)],
        scratch_shapes=[pltpu.VMEM((tm, d), BF16)],
        compiler_params=_cp(2), name="swa_in_proj",
    )(x, g, shift, scale, w, *tabs)


def _rope_tables(pos, hd, rope_dims):
    half = rope_dims // 2
    inv_freq = ROPE_THETA ** (-jnp.arange(half, dtype=F32) * 2.0 / rope_dims)
    ang = pos.astype(F32)[:, None] * inv_freq[None, :]
    cos, sin = jnp.cos(ang), jnp.sin(ang)
    lane = np.arange(LANES) % hd
    idx = lane % half
    a = jnp.where(lane < rope_dims, cos[:, idx], 1.0)
    b = jnp.where(lane < half, -sin[:, idx], 0.0)
    c = jnp.where((lane >= half) & (lane < rope_dims), sin[:, idx], 0.0)
    return a, b, c


def _swa_core(q, k_top, k_bot, v_top, v_bot, bias, sinks, *, group):
    t = q.shape[0]
    npair = group // 2
    win = k_top.shape[0]
    half_lanes = lax.broadcasted_iota(jnp.int32, (win, LANES), 1) < LANES // 2
    count_even = jnp.where(half_lanes, 1.0, 0.0).astype(BF16)
    count_odd = jnp.where(half_lanes, 0.0, 1.0).astype(BF16)
    v2 = jnp.concatenate([jnp.concatenate([v_top, count_even], axis=1),
                          jnp.concatenate([v_bot, count_odd], axis=1)], axis=0)
    qs = jnp.concatenate([q[:, p * LANES:(p + 1) * LANES] for p in range(npair)], axis=0)
    s = _dot_nt(qs, jnp.concatenate([k_top, k_bot], axis=0)) + bias
    ps, es = [], []
    for half in range(2):
        cols = [s[:, half * win + c * LANES: half * win + (c + 1) * LANES] for c in range(win // LANES)]
        sink = sinks[:, half * LANES:(half + 1) * LANES]
        m = jnp.maximum(jnp.max(functools.reduce(jnp.maximum, cols), axis=-1, keepdims=True), sink)
        ps += [jnp.exp2(col - m) for col in cols]
        es.append(jnp.exp2(sink - m))
    o = _dot(jnp.concatenate(ps, axis=1).astype(BF16), v2)
    lane = lax.broadcasted_iota(jnp.int32, (o.shape[0], LANES), 1)
    o = o[:, :LANES] / (o[:, LANES:] + jnp.where(lane < LANES // 2, es[0], es[1]))
    return jnp.concatenate([o[p * t:(p + 1) * t] for p in range(npair)], axis=1)


def _swa_attn_kernel(q_ref, kh_ref, km_ref, vh_ref, vm_ref, bias_ref, sink_ref, o_ref, *, kv, group, hd):
    gw = group * hd
    bias = bias_ref[...]
    for g in range(kv):
        def win(halo, main, off):
            sl = slice(g * 2 * LANES + off, g * 2 * LANES + off + LANES)
            return jnp.concatenate([halo[:, sl], main[:, sl]], axis=0)
        o = _swa_core(q_ref[:, g * gw:(g + 1) * gw],
                      win(kh_ref, km_ref, 0), win(kh_ref, km_ref, LANES),
                      win(vh_ref, vm_ref, 0), win(vh_ref, vm_ref, LANES),
                      bias, sink_ref[g], group=group)
        o_ref[:, g * gw:(g + 1) * gw] = o.astype(o_ref.dtype)


def _swa_attn(q, kk, vv, bias2, sink_rows, *, nb, s, kv, group, hd):
    t = 2 * CHUNK
    nt = s // t
    d = q.shape[1]
    kkw = kk.shape[1]
    main = pl.BlockSpec((t, kkw), lambda b, i: (b * nt + i, 0))
    halo = pl.BlockSpec((t, kkw), lambda b, i: (b * nt + jnp.maximum(i - 1, 0), 0))
    return pl.pallas_call(
        functools.partial(_swa_attn_kernel, kv=kv, group=group, hd=hd),
        grid=(nb, nt),
        in_specs=[pl.BlockSpec((t, d), lambda b, i: (b * nt + i, 0)),
                  halo, main, halo, main,
                  pl.BlockSpec((None,) + bias2.shape[1:], lambda b, i: (jnp.minimum(i, 1), 0, 0)),
                  pl.BlockSpec(sink_rows.shape, lambda b, i: (0, 0, 0))],
        out_specs=pl.BlockSpec((t, d), lambda b, i: (b * nt + i, 0)),
        out_shape=jax.ShapeDtypeStruct(q.shape, BF16),
        compiler_params=_cp(2), name="swa_attention",
    )(q, kk, kk, vv, vv, bias2, sink_rows)


def _swa_sample_kernel(q_ref, kc_ref, vc_ref, kkn_ref, vvn_ref, bias_ref, sink_ref, o_ref, *, kv, group, hd, t):
    gw = group * hd
    kkc = _dup_halves(kc_ref[...], hd).astype(BF16)
    vvc = _dup_halves(vc_ref[...], hd).astype(BF16)
    pad = jnp.zeros((kc_ref.shape[0] - t, LANES), BF16)
    for g in range(kv):
        def win(cache, new, off):
            sl = slice(g * 2 * LANES + off, g * 2 * LANES + off + LANES)
            return jnp.concatenate([cache[:, sl], new[:, sl], pad], axis=0)
        o = _swa_core(q_ref[:, g * gw:(g + 1) * gw],
                      win(kkc, kkn_ref, 0), win(kkc, kkn_ref, LANES),
                      win(vvc, vvn_ref, 0), win(vvc, vvn_ref, LANES),
                      bias_ref[...], sink_ref[g], group=group)
        o_ref[:, g * gw:(g + 1) * gw] = o.astype(o_ref.dtype)


def _swa_sample_attn(q, k_cache, v_cache, kk_new, vv_new, bias, sink_rows, *, kv, group, hd, t):
    nb, buf, kvd = k_cache.shape
    d = q.shape[1]
    full = lambda a: pl.BlockSpec(a.shape, lambda b: (0,) * a.ndim)
    cache_spec = pl.BlockSpec((None, buf, kvd), lambda b: (b, 0, 0))
    new_spec = pl.BlockSpec((t, kk_new.shape[1]), lambda b: (b, 0))
    return pl.pallas_call(
        functools.partial(_swa_sample_kernel, kv=kv, group=group, hd=hd, t=t),
        grid=(nb,),
        in_specs=[pl.BlockSpec((t, d), lambda b: (b, 0)), cache_spec, cache_spec, new_spec, new_spec,
                  full(bias), full(sink_rows)],
        out_specs=pl.BlockSpec((t, d), lambda b: (b, 0)),
        out_shape=jax.ShapeDtypeStruct(q.shape, BF16),
        compiler_params=_cp(1), name="swa_sample_attention",
    )(q, k_cache, v_cache, kk_new, vv_new, bias, sink_rows)


def _window_bias(valid, npair):
    b = np.where(valid, 0.0, NEG).astype(np.float32)
    return np.tile(b, (npair, 2))


def _prompt_bias(npair):
    t = 2 * CHUNK
    qc = np.arange(t)[:, None] // CHUNK
    kc = np.arange(t + SWA_WINDOW_CHUNKS * CHUNK)[None, :] // CHUNK
    valid = (kc >= qc) & (kc <= qc + SWA_WINDOW_CHUNKS)
    first = valid & (kc >= SWA_WINDOW_CHUNKS)
    return jnp.asarray(np.stack([_window_bias(first, npair), _window_bias(valid, npair)]))


def _sample_bias(past_len, buf, t, npair):
    q_pos = past_len + np.arange(t)
    k_pos = np.concatenate([past_len - buf + np.arange(buf), q_pos])
    qch, kch = q_pos // CHUNK, k_pos // CHUNK
    valid = np.zeros((t, 2 * buf), bool)
    valid[:, :buf + t] = (kch[None, :] <= qch[:, None]) & (kch[None, :] >= qch[:, None] - SWA_WINDOW_CHUNKS)
    return jnp.asarray(_window_bias(valid, npair))


def _sink_rows(sinks, kv, group, t):
    s = (sinks * LOG2E).reshape(kv, group // 2, 1, 2, 1)
    return jnp.broadcast_to(s, (kv, group // 2, t, 2, LANES)).reshape(kv, (group // 2) * t, 2 * LANES)


def kernel(x_prompt, x_sample, c_prompt, c_sample, cache_fox_k, cache_fox_v, cache_fox_logf, cache_swa_k,
           cache_swa_v, ada_w, ada_b, norm_mix_g, norm_ffn_g, fox_w_in, fox_b_f, fox_w_out, swa_w_in,
           swa_sinks, swa_w_out, ffn_w_up, ffn_w_down, final_g):
    bp, s, d = x_prompt.shape
    bs, t, _ = x_sample.shape
    depth = ada_w.shape[0]
    past_len = cache_fox_k.shape[2]
    nh_fox, hd_fox = cache_fox_k.shape[3], cache_fox_k.shape[4]
    buf, kv, hd_swa = cache_swa_k.shape[2], cache_swa_k.shape[3], cache_swa_k.shape[4]
    nh_swa = swa_sinks.shape[1]
    group = nh_swa // kv
    kvd = kv * hd_swa
    rope_dims = hd_swa // 4
    assert hd_fox == LANES and hd_swa == LANES // 2 and rope_dims == 16 and group % 2 == 0
    assert buf == SWA_WINDOW_CHUNKS * CHUNK and t <= buf and s % (2 * CHUNK) == 0

    mp, ms = bp * s, bs * t
    tm_p = min(512, s)
    tpb_p = s // tm_p
    tm_fox = min(1024, s)
    tk = min(FOX_TK, s)
    tq = min(FOX_NSUB * tk, s)

    mods = _ada(jnp.concatenate([c_prompt, c_sample], axis=0), ada_w, ada_b)

    def split_mods(i):
        six = jnp.split(mods[i], 6, axis=-1)
        prompt = [m[:bp, None, :] for m in six]
        sample = [jnp.repeat(m[bp:], t, axis=0)[None] for m in six]
        return prompt, sample

    row = lambda v: v.reshape(1, -1)
    xp = x_prompt.reshape(mp, d)
    xs = x_sample.reshape(ms, d)
    pos_p = jnp.arange(s)
    pos_s = past_len + jnp.arange(t)
    tabs_p = _rope_tables(pos_p, hd_swa, rope_dims)
    tabs_s = tuple(jnp.tile(a, (bs, 1)) for a in _rope_tables(pos_s, hd_swa, rope_dims))
    bias_p = _prompt_bias(group // 2)
    bias_s = _sample_bias(past_len, buf, t, group // 2)

    fox_w_in_b, fox_w_out_b = fox_w_in.astype(BF16), fox_w_out.astype(BF16)
    swa_w_in_b, swa_w_out_b = swa_w_in.astype(BF16), swa_w_out.astype(BF16)
    w_up_b, w_down_b = ffn_w_up.astype(BF16), ffn_w_down.astype(BF16)

    outs = {k: [] for k in ("fkp", "fvp", "flp", "fks", "fvs", "fls", "skp", "svp", "sks", "svs")}
    for i in range(depth):
        mod_p, mod_s = split_mods(i)
        j = i // 2
        if i % 2 == 0:
            w_f = jnp.pad(fox_w_in[j, :, 3 * d:], ((0, 0), (0, LANES - nh_fox))).astype(BF16)
            b_f = row(fox_b_f[j])
            qh, kh, vh, k32, v32, lf, qn2, kn2 = _fox_in(
                xp, row(norm_mix_g[i]), mod_p[0], mod_p[1], fox_w_in_b, w_f, b_f,
                layer=j, tm=tm_fox, tpb=s // tm_fox, head_major=True, nh=nh_fox, hd=hd_fox)
            lft = lf.reshape(bp, s, nh_fox).transpose(0, 2, 1)
            c = _cumsum_rows(lft.reshape(-1, LANES), s // LANES).reshape(bp * nh_fox, s)
            first = _fox_first_blocks(c, _head_norm_bound(qn2, bp, nh_fox, hd_fox),
                                      _head_norm_bound(kn2, bp, nh_fox, hd_fox), tk)
            op = _fox_attn(qh, kh, vh, c, first, tk=tk, nsub=tq // tk)
            xp = _proj_res(op.reshape(mp, d), fox_w_out_b, xp, mod_p[2], layer=j, tm=tm_p, tpb=tpb_p)
            outs["fkp"].append(k32.reshape(bp, s, nh_fox, hd_fox))
            outs["fvp"].append(v32.reshape(bp, s, nh_fox, hd_fox))
            outs["flp"].append(lf.reshape(bp, s, nh_fox))
            qs_, k32s, v32s, lfs = _fox_in(xs, row(norm_mix_g[i]), mod_s[0], mod_s[1], fox_w_in_b, w_f, b_f,
                                           layer=j, tm=ms, tpb=1, head_major=False, nh=nh_fox, hd=hd_fox)
            lf_all = jnp.concatenate([cache_fox_logf[j], lfs.reshape(bs, t, nh_fox)], axis=1)
            lc = LANES * int(2 ** np.ceil(np.log2(-(-(past_len + t) // LANES))))
            lf_all = jnp.pad(lf_all.transpose(0, 2, 1), ((0, 0), (0, 0), (0, lc - past_len - t)))
            c_s = _cumsum_rows(lf_all.reshape(-1, LANES), lc // LANES).reshape(bs, nh_fox, lc)
            os_ = _fox_sample_attn(qs_, cache_fox_k, cache_fox_v, k32s, v32s, c_s, layer=j, t=t)
            xs = _proj_res(os_, fox_w_out_b, xs, mod_s[2], layer=j, tm=ms, tpb=1)
            outs["fks"].append(k32s.reshape(bs, t, nh_fox, hd_fox))
            outs["fvs"].append(v32s.reshape(bs, t, nh_fox, hd_fox))
            outs["fls"].append(lfs.reshape(bs, t, nh_fox))
        else:
            q, kk, vv, k32, v32 = _swa_in(xp, row(norm_mix_g[i]), mod_p[0], mod_p[1], swa_w_in_b, tabs_p,
                                          layer=j, tm=tm_p, tpb=tpb_p, kvd=kvd, hd=hd_swa)
            op = _swa_attn(q, kk, vv, bias_p, _sink_rows(swa_sinks[j], kv, group, 2 * CHUNK),
                           nb=bp, s=s, kv=kv, group=group, hd=hd_swa)
            xp = _proj_res(op, swa_w_out_b, xp, mod_p[2], layer=j, tm=tm_p, tpb=tpb_p)
            outs["skp"].append(k32.reshape(bp, s, kvd)[:, -buf:].reshape(bp, buf, kv, hd_swa))
            outs["svp"].append(v32.reshape(bp, s, kvd)[:, -buf:].reshape(bp, buf, kv, hd_swa))
            q, kk, vv, k32, v32 = _swa_in(xs, row(norm_mix_g[i]), mod_s[0], mod_s[1], swa_w_in_b, tabs_s,
                                          layer=j, tm=ms, tpb=1, kvd=kvd, hd=hd_swa)
            os_ = _swa_sample_attn(q, cache_swa_k[j].reshape(bs, buf, kvd), cache_swa_v[j].reshape(bs, buf, kvd),
                                   kk, vv, bias_s, _sink_rows(swa_sinks[j], kv, group, t),
                                   kv=kv, group=group, hd=hd_swa, t=t)
            xs = _proj_res(os_, swa_w_out_b, xs, mod_s[2], layer=j, tm=ms, tpb=1)
            k_all = jnp.concatenate([cache_swa_k[j], k32.reshape(bs, t, kv, hd_swa)], axis=1)
            v_all = jnp.concatenate([cache_swa_v[j], v32.reshape(bs, t, kv, hd_swa)], axis=1)
            outs["sks"].append(k_all[:, -buf:])
            outs["svs"].append(v_all[:, -buf:])
        fg = row(final_g) if i == depth - 1 else None
        xp = _ffn(xp, row(norm_ffn_g[i]), mod_p[3], mod_p[4], mod_p[5], w_up_b, w_down_b, fg,
                  layer=i, tm=tm_fox, tpb=s // tm_fox)
        xs = _ffn(xs, row(norm_ffn_g[i]), mod_s[3], mod_s[4], mod_s[5], w_up_b, w_down_b, fg,
                  layer=i, tm=ms, tpb=1)

    st = lambda k: jnp.stack(outs[k])
    return (xp.reshape(bp, s, d), xs.reshape(bs, t, d),
            st("fkp"), st("fvp"), st("flp"), st("fks"), st("fvs"), st("fls"),
            st("skp"), st("svp"), st("sks"), st("svs"))
```

```python
import functools

import numpy as np
import jax
import jax.numpy as jnp
from jax import lax
from jax.experimental import pallas as pl
from jax.experimental.pallas import tpu as pltpu

F32 = jnp.float32
BF16 = jnp.bfloat16

RMS_EPS = 1e-6
CHUNK = 64
SWA_WINDOW_CHUNKS = 2
ROPE_THETA = 500000.0
LANES = 128
NEG = -1e30
NEG_BLOCK = -3e38
VMEM_LIMIT_BYTES = 56 * 1024 * 1024
LOG2E = 1.4426950408889634
FOX_SKIP_LOG2 = 150.0
FFN_STEP_ELEMS = 512 * 1024
FOX_TK = 512
FOX_NSUB = 8


def _cp(n_axes):
    return pltpu.CompilerParams(dimension_semantics=("arbitrary",) * n_axes,
                                vmem_limit_bytes=VMEM_LIMIT_BYTES)


def _dot(a, b):
    return jnp.dot(a, b, preferred_element_type=F32)


def _dot_nt(a, b):
    return lax.dot_general(a, b, (((1,), (1,)), ((), ())), preferred_element_type=F32)


def _rms(x):
    return x * lax.rsqrt(jnp.mean(x * x, axis=-1, keepdims=True) + RMS_EPS)


NORM_ROWS = 16


def _norm_mod_rows(h_ref, x_ref, g_ref, sh_ref, sc_ref, start, count, inline=False):
    per_token = sh_ref.shape[0] > 1
    g = g_ref[...]
    if not per_token:
        gain, shift = g * (1.0 + sc_ref[...]), sh_ref[...]

    def chunk(r0):
        rows = pl.ds(pl.multiple_of(r0, NORM_ROWS), NORM_ROWS)
        x = x_ref[rows, :]
        if per_token:
            gain_, shift_ = g * (1.0 + sc_ref[rows, :]), sh_ref[rows, :]
        else:
            gain_, shift_ = gain, shift
        h_ref[rows, :] = (_rms(x) * gain_ + shift_).astype(h_ref.dtype)

    if inline:
        for c in range(count // NORM_ROWS):
            chunk(start + c * NORM_ROWS)
    else:
        def body(i, carry):
            chunk(start + i * NORM_ROWS)
            return carry
        lax.fori_loop(0, count // NORM_ROWS, body, 0, unroll=4)


def _norm_mod_store(h_ref, x_ref, g_ref, sh_ref, sc_ref):
    _norm_mod_rows(h_ref, x_ref, g_ref, sh_ref, sc_ref, 0, x_ref.shape[0])


def _log_sigmoid(z):
    return jnp.minimum(z, 0.0) - jnp.log1p(jnp.exp(-jnp.abs(z)))


CAST_BLOCK_BYTES = 8 * 1024 * 1024


def _cast_kernel(x_ref, o_ref):
    o_ref[...] = x_ref[...].astype(o_ref.dtype)


def _to_bf16(w):
    shape = w.shape
    cols = shape[-1]
    x = w.reshape(-1, cols)
    rows = x.shape[0]
    tr = rows
    while tr * cols * 4 > CAST_BLOCK_BYTES and tr % 32 == 0:
        tr //= 2
    out = pl.pallas_call(
        _cast_kernel, grid=(rows // tr,),
        in_specs=[pl.BlockSpec((tr, cols), lambda i: (i, 0))],
        out_specs=pl.BlockSpec((tr, cols), lambda i: (i, 0)),
        out_shape=jax.ShapeDtypeStruct(x.shape, BF16),
        compiler_params=_cp(1), name="weight_to_bf16",
    )(x)
    return out.reshape(shape)


def _ada_kernel(c_ref, w_ref, b_ref, o_ref):
    c = c_ref[...]
    a = (c / (1.0 + jnp.exp(-c))).astype(BF16)
    o_ref[...] = _dot(a, w_ref[...].astype(BF16)) + b_ref[...]


def _ada(c_all, ada_w, ada_b):
    depth, d, n6 = ada_w.shape
    r = c_all.shape[0]
    tn = min(1024, n6)
    return pl.pallas_call(
        _ada_kernel,
        grid=(depth, n6 // tn),
        in_specs=[pl.BlockSpec((r, d), lambda l, n: (0, 0)),
                  pl.BlockSpec((None, d, tn), lambda l, n: (l, 0, n)),
                  pl.BlockSpec((None, 1, tn), lambda l, n: (l, 0, n))],
        out_specs=pl.BlockSpec((None, r, tn), lambda l, n: (l, 0, n)),
        out_shape=jax.ShapeDtypeStruct((depth, r, n6), F32),
        compiler_params=_cp(2), name="ada_params",
    )(c_all, ada_w, ada_b.reshape(depth, 1, n6))


def _fox_in_kernel(x_ref, g_ref, sh_ref, sc_ref, wq_ref, wk_ref, wv_ref, wf_ref, bf_ref, *rest,
                   head_major, hd, nh, q_scale):
    if head_major:
        qb_ref, kb_ref, vb_ref, k_ref, v_ref, lf_ref, qn_ref, kn_ref, h_ref = rest
    else:
        qb_ref, k_ref, v_ref, lf_ref, h_ref = rest

    @pl.when(pl.program_id(1) == 0)
    def _():
        _norm_mod_store(h_ref, x_ref, g_ref, sh_ref, sc_ref)
        z = _dot(h_ref[...], wf_ref[...])[:, :nh] + bf_ref[...]
        lf_ref[...] = _log_sigmoid(z)

    hb = h_ref[...]
    q = _dot(hb, wq_ref[...]) * q_scale
    k = _dot(hb, wk_ref[...])
    v = _dot(hb, wv_ref[...])
    k_ref[...] = k
    v_ref[...] = v
    if head_major:
        ones = jnp.ones((hd, hd), BF16)

        def max_row_norm2(x):
            n2 = _dot((x * x).astype(BF16), ones)
            return jnp.max(n2.reshape(x.shape[0] // 8, 8, hd), axis=0)

        qn, kn = [], []
        for j in range(q.shape[1] // hd):
            sl = slice(j * hd, (j + 1) * hd)
            qb_ref[j] = q[:, sl].astype(BF16)
            kb_ref[j] = k[:, sl].astype(BF16)
            vb_ref[j] = v[:, sl].astype(BF16)
            qn.append(max_row_norm2(q[:, sl]))
            kn.append(max_row_norm2(k[:, sl]))
        qn_ref[...] = jnp.concatenate(qn, axis=1)
        kn_ref[...] = jnp.concatenate(kn, axis=1)
    else:
        qb_ref[...] = q.astype(BF16)


def _mod_spec(mod, tpb, cols=None):
    _, rows, d = mod.shape
    if cols is None:
        return pl.BlockSpec((None, rows, d), lambda i, n: (i // tpb, 0, 0))
    return pl.BlockSpec((None, rows, cols), lambda i, n: (i // tpb, 0, n))


def _fox_in(x, g, shift, scale, w_in, w_f, b_f, *, layer, tm, tpb, head_major, nh, hd):
    m, d = x.shape
    tn = min(512, d)
    nt = d // tn
    hpt = tn // hd
    nb, s = m // (tm * tpb), tm * tpb
    in_specs = [pl.BlockSpec((tm, d), lambda i, n: (i, 0)),
                pl.BlockSpec((1, d), lambda i, n: (0, 0)),
                _mod_spec(shift, tpb), _mod_spec(scale, tpb),
                pl.BlockSpec((None, d, tn), lambda i, n: (layer, 0, n)),
                pl.BlockSpec((None, d, tn), lambda i, n: (layer, 0, n + nt)),
                pl.BlockSpec((None, d, tn), lambda i, n: (layer, 0, n + 2 * nt)),
                pl.BlockSpec((d, LANES), lambda i, n: (0, 0)),
                pl.BlockSpec((1, nh), lambda i, n: (0, 0))]
    rm_spec = pl.BlockSpec((tm, tn), lambda i, n: (i, n))
    lf_spec = pl.BlockSpec((tm, nh), lambda i, n: (i, 0))
    if head_major:
        hm_spec = pl.BlockSpec((None, hpt, tm, hd), lambda i, n: (i // tpb, n, i % tpb, 0))
        hm_shape = jax.ShapeDtypeStruct((nb, nh, s, hd), BF16)
        n2_spec = pl.BlockSpec((8, tn), lambda i, n: (i, n))
        out_specs = [hm_spec, hm_spec, hm_spec, rm_spec, rm_spec, lf_spec, n2_spec, n2_spec]
        out_shape = [hm_shape, hm_shape, hm_shape]
    else:
        out_specs = [rm_spec, rm_spec, rm_spec, lf_spec]
        out_shape = [jax.ShapeDtypeStruct((m, d), BF16)]
    out_shape += [jax.ShapeDtypeStruct((m, d), F32), jax.ShapeDtypeStruct((m, d), F32),
                  jax.ShapeDtypeStruct((m, nh), F32)]
    if head_major:
        out_shape += [jax.ShapeDtypeStruct((m // tm * 8, d), F32)] * 2
    return pl.pallas_call(
        functools.partial(_fox_in_kernel, head_major=head_major, hd=hd, nh=nh, q_scale=hd ** -0.5 * LOG2E),
        grid=(m // tm, nt), in_specs=in_specs, out_specs=out_specs, out_shape=out_shape,
        scratch_shapes=[pltpu.VMEM((tm, d), BF16)],
        compiler_params=_cp(2), name="fox_in_proj",
    )(x, g, shift, scale, w_in, w_in, w_in, w_f, b_f)


def _split3(x):
    x1 = x.astype(BF16)
    r1 = x - x1.astype(F32)
    x2 = r1.astype(BF16)
    x3 = (r1 - x2.astype(F32)).astype(BF16)
    return x1, x2, x3


def _cumsum_kernel(x_ref, u_ref, l_ref, o_ref):
    u = u_ref[...]
    lo = l_ref[...]
    y = sum(_dot(p, u) for p in _split3(x_ref[...]))
    tot = jnp.broadcast_to(y[:, LANES - 1:LANES], y.shape)
    o_ref[...] = (y + sum(_dot(lo, p) for p in _split3(tot))) * LOG2E


def _cumsum_rows(x, nc):
    rows = x.shape[0]
    assert LANES % nc == 0 or nc % LANES == 0
    n = max(nc, LANES)
    assert rows % n == 0
    u = jnp.asarray(np.triu(np.ones((LANES, LANES), np.float32)), BF16)
    idx = np.arange(n)
    same = (idx[:, None] // nc) == (idx[None, :] // nc)
    lo = jnp.asarray((same & (idx[None, :] < idx[:, None])).astype(np.float32), BF16)
    return pl.pallas_call(
        _cumsum_kernel, grid=(rows // n,),
        in_specs=[pl.BlockSpec((n, LANES), lambda i: (i, 0)),
                  pl.BlockSpec((LANES, LANES), lambda i: (0, 0)),
                  pl.BlockSpec((n, n), lambda i: (0, 0))],
        out_specs=pl.BlockSpec((n, LANES), lambda i: (i, 0)),
        out_shape=jax.ShapeDtypeStruct(x.shape, F32),
        compiler_params=_cp(1), name="logf_cumsum",
    )(x, u, lo)


NORM_SLACK = 1.01


def _head_norm_bound(n2, nb, nh, hd):
    n2 = n2.reshape(nb, -1, nh, hd)
    return (NORM_SLACK * jnp.sqrt(jnp.max(n2, axis=(1, 3)))).reshape(-1)


def _fox_first_blocks(c, qmax, kmax, tk):
    bh, s = c.shape
    cb = c.reshape(bh, s // tk, tk)
    c_start, c_end = cb[:, :, 0], cb[:, :, -1]
    budget = (2.0 * qmax * kmax + FOX_SKIP_LOG2)[:, None, None]
    skippable = (c_end[:, None, :] - c_start[:, :, None]) > budget
    earlier = np.tril(np.ones((s // tk, s // tk), bool), -1)
    return jnp.sum(skippable & earlier, axis=-1).astype(jnp.int32).reshape(-1)


def _fox_attn_kernel(first_ref, q_ref, k_ref, v_ref, c_ref, o_ref, m_ref, acc_ref, *, tk, nsub, hd):
    b, h, qi = pl.program_id(0), pl.program_id(1), pl.program_id(2)
    nkb = pl.num_programs(2) * nsub
    m_ref[...] = jnp.full(m_ref.shape, NEG, F32)
    acc_ref[...] = jnp.zeros(acc_ref.shape, F32)
    ones = jnp.ones((tk, hd), BF16)
    ncol = tk // LANES

    def step(r, j, mask=None, valid=None):
        rows = pl.ds(pl.multiple_of(j * tk, tk), tk)
        c_row = c_ref[j]
        if valid is not None:
            c_row = jnp.where(valid, c_row, -NEG_BLOCK)
        s = _dot_nt(q_ref[r * tk:(r + 1) * tk, :], k_ref[rows, :]) - c_row
        if mask is not None:
            s = jnp.where(mask, s, NEG)
        cols = [s[:, c * LANES:(c + 1) * LANES] for c in range(ncol)]
        m_cur = functools.reduce(jnp.maximum, cols)
        m_prev = m_ref[r]
        m_next = jnp.maximum(m_prev, jnp.max(m_cur, axis=-1, keepdims=True))
        alpha = jnp.exp2(m_prev - m_next)
        p = jnp.concatenate([jnp.exp2(col - m_next) for col in cols], axis=1).astype(BF16)
        pv = _dot(p, jnp.concatenate([v_ref[rows, :], ones], axis=1))
        acc_ref[r] = jnp.concatenate([alpha, alpha], axis=1) * acc_ref[r] + pv
        m_ref[r] = m_next

    base = qi * nsub
    off = (b * pl.num_programs(1) + h) * nkb + base
    n = functools.reduce(jnp.maximum, [base + r - first_ref[off + r] for r in range(nsub)])

    def body(t, carry):
        for r in range(nsub):
            j = base + r - n + t
            step(r, jnp.maximum(j, 0), valid=j >= 0)
        return carry

    lax.fori_loop(0, n, body, 0)
    row = lax.broadcasted_iota(jnp.int32, (tk, tk), 0)
    col = lax.broadcasted_iota(jnp.int32, (tk, tk), 1)
    causal = col <= row
    for r in range(nsub):
        step(r, base + r, mask=causal)
    for r in range(nsub):
        acc = acc_ref[r]
        o_ref[r * tk:(r + 1) * tk, :] = (acc[:, :hd] / acc[:, hd:]).astype(o_ref.dtype)


def _fox_attn(qh, kh, vh, c, first, *, tk, nsub):
    nb, nh, s, hd = qh.shape
    tq = tk * nsub
    nq = s // tq
    kv_spec = pl.BlockSpec((None, None, s, hd), lambda b, h, i, f: (b, h, 0, 0))
    return pl.pallas_call(
        functools.partial(_fox_attn_kernel, tk=tk, nsub=nsub, hd=hd),
        grid_spec=pltpu.PrefetchScalarGridSpec(
            num_scalar_prefetch=1, grid=(nb, nh, nq),
            in_specs=[pl.BlockSpec((None, None, tq, hd), lambda b, h, i, f: (b, h, i, 0)),
                      kv_spec, kv_spec,
                      pl.BlockSpec((None, s // tk, 1, tk), lambda b, h, i, f: (b * nh + h, 0, 0, 0))],
            out_specs=pl.BlockSpec((None, tq, hd), lambda b, h, i, f: (b, i, h)),
            scratch_shapes=[pltpu.VMEM((nsub, tk, LANES), F32), pltpu.VMEM((nsub, tk, 2 * hd), F32)]),
        out_shape=jax.ShapeDtypeStruct((nb, s, nh * hd), BF16),
        compiler_params=_cp(3), name="fox_attention",
    )(first, qh, kh, vh, c.reshape(nb * nh, s // tk, 1, tk))


def _fox_sample_kernel(q_ref, kc_ref, vc_ref, kn_ref, vn_ref, c_ref, cf_ref, o_ref, *, hb, hd, p_len, t):
    n = p_len * hb
    kf = kc_ref[...].reshape(n, hd).astype(BF16)
    vf = vc_ref[...].reshape(n, hd).astype(BF16)
    heads = [slice(j * hd, (j + 1) * hd) for j in range(hb)]
    q8 = jnp.concatenate([q_ref[:, sl] for sl in heads], axis=0)
    row_head = lax.shift_right_logical(lax.broadcasted_iota(jnp.int32, (hb * t, LANES), 0), t.bit_length() - 1)
    col_head = lax.broadcasted_iota(jnp.int32, (hb * t, LANES), 1) & (hb - 1)
    own = jnp.where(row_head == col_head, 0.0, NEG)
    s = _dot_nt(q8, kf)
    cf = cf_ref[...]
    cols = [s[:, g * LANES:(g + 1) * LANES] - cf[:, g * LANES:(g + 1) * LANES] + own for g in range(n // LANES)]
    tri = lax.broadcasted_iota(jnp.int32, (t, t), 1) <= lax.broadcasted_iota(jnp.int32, (t, t), 0)
    s_n = jnp.concatenate(
        [jnp.where(tri, _dot_nt(q_ref[:, sl], kn_ref[:, sl].astype(BF16)) - c_ref[j:j + 1, p_len:p_len + t], NEG)
         for j, sl in enumerate(heads)], axis=0)
    m = jnp.maximum(jnp.max(functools.reduce(jnp.maximum, cols), axis=-1, keepdims=True),
                    jnp.max(s_n, axis=-1, keepdims=True))
    ps = [jnp.exp2(col - m) for col in cols]
    p_n = jnp.exp2(s_n - m)
    den = jnp.sum(functools.reduce(jnp.add, ps), axis=-1, keepdims=True) + jnp.sum(p_n, axis=-1, keepdims=True)
    o = _dot(jnp.concatenate(ps, axis=1).astype(BF16), vf)
    o_n = jnp.concatenate([_dot(p_n[j * t:(j + 1) * t].astype(BF16), vn_ref[:, sl].astype(BF16))
                           for j, sl in enumerate(heads)], axis=0)
    o = (o + o_n) / den
    for j, sl in enumerate(heads):
        o_ref[:, sl] = o[j * t:(j + 1) * t].astype(o_ref.dtype)


def _fox_sample_attn(q, k_cache, v_cache, k_new, v_new, c_all, *, layer, t):
    _, nb, p_len, nh, hd = k_cache.shape
    hb = min(8, nh)
    assert t & (t - 1) == 0 and hb & (hb - 1) == 0 and (p_len * hb) % LANES == 0 and LANES % hb == 0
    w = hb * hd
    lc = c_all.shape[-1]
    c_flat = c_all[:, :, :p_len].reshape(nb, nh // hb, hb, p_len).transpose(0, 1, 3, 2)
    c_flat = c_flat.reshape(nb, nh // hb, 1, p_len * hb)
    cache_spec = pl.BlockSpec((None, None, p_len, hb, hd), lambda b, h: (layer, b, 0, h, 0))
    row_spec = pl.BlockSpec((t, w), lambda b, h: (b, h))
    return pl.pallas_call(
        functools.partial(_fox_sample_kernel, hb=hb, hd=hd, p_len=p_len, t=t),
        grid=(nb, nh // hb),
        in_specs=[row_spec, cache_spec, cache_spec, row_spec, row_spec,
                  pl.BlockSpec((None, hb, lc), lambda b, h: (b, h, 0)),
                  pl.BlockSpec((None, None, 1, p_len * hb), lambda b, h: (b, h, 0, 0))],
        out_specs=row_spec,
        out_shape=jax.ShapeDtypeStruct((nb * t, nh * hd), BF16),
        compiler_params=_cp(2), name="fox_sample_attention",
    )(q, k_cache, v_cache, k_new, v_new, c_all, c_flat)


def _proj_res_kernel(a_ref, w_ref, x_ref, gate_ref, o_ref):
    o_ref[...] = x_ref[...] + gate_ref[...] * _dot(a_ref[...], w_ref[...])


def _proj_res(a, w, x, gate, *, layer, tm, tpb):
    m, k = a.shape
    n = w.shape[2]
    return pl.pallas_call(
        _proj_res_kernel, grid=(m // tm, 1),
        in_specs=[pl.BlockSpec((tm, k), lambda i, j: (i, 0)),
                  pl.BlockSpec((None, k, n), lambda i, j: (layer, 0, 0)),
                  pl.BlockSpec((tm, n), lambda i, j: (i, 0)),
                  _mod_spec(gate, tpb)],
        out_specs=pl.BlockSpec((tm, n), lambda i, j: (i, 0)),
        out_shape=jax.ShapeDtypeStruct((m, n), F32),
        compiler_params=_cp(2), name="attn_out_proj",
    )(a, w, x, gate)


def _ffn_kernel(x_ref, g_ref, sh_ref, sc_ref, gate_ref, wu_ref, wd_ref, *rest, final, ahead, nf):
    rest = list(rest)
    xn_ref, shn_ref, scn_ref = (rest.pop(0), rest.pop(0), rest.pop(0)) if ahead else (None, None, None)
    fg_ref = rest.pop(0) if final else None
    o_ref, h_ref, acc_ref = rest
    i, f = pl.program_id(0), pl.program_id(1)
    tm = x_ref.shape[0]
    slot = i % 2 if ahead else 0

    @pl.when(jnp.logical_and(i == 0, f == 0) if ahead else f == 0)
    def _():
        _norm_mod_store(h_ref.at[0], x_ref, g_ref, sh_ref, sc_ref)

    @pl.when(f == 0)
    def _():
        acc_ref[...] = jnp.zeros(acc_ref.shape, F32)

    if ahead:
        _norm_mod_rows(h_ref.at[1 - slot], xn_ref, g_ref, shn_ref, scn_ref, f * (tm // nf), tm // nf, inline=True)
    a = jnp.maximum(_dot(h_ref[slot], wu_ref[...]), 0.0)
    acc_ref[...] += _dot((a * a).astype(BF16), wd_ref[...])

    @pl.when(f == nf - 1)
    def _():
        y = x_ref[...] + gate_ref[...] * acc_ref[...]
        if final:
            y = _rms(y) * fg_ref[...]
        o_ref[...] = y


def _ffn(x, g, shift, scale, gate, w_up, w_down, final_g, *, layer, tm, tpb):
    m, d = x.shape
    ff = w_up.shape[2]
    tf = min(ff, FFN_STEP_ELEMS // tm)
    nt, nf = m // tm, ff // tf
    ahead = nt > 1 and (tm // nf) % NORM_ROWS == 0
    vec = pl.BlockSpec((1, d), lambda i, f: (0, 0))
    in_specs = [pl.BlockSpec((tm, d), lambda i, f: (i, 0)), vec,
                _mod_spec(shift, tpb), _mod_spec(scale, tpb), _mod_spec(gate, tpb),
                pl.BlockSpec((None, d, tf), lambda i, f: (layer, 0, f)),
                pl.BlockSpec((None, tf, d), lambda i, f: (layer, f, 0))]
    args = [x, g, shift, scale, gate, w_up, w_down]
    if ahead:
        nxt = lambda i: jnp.minimum(i + 1, nt - 1)
        in_specs += [pl.BlockSpec((tm, d), lambda i, f: (nxt(i), 0)),
                     pl.BlockSpec((None,) + shift.shape[1:], lambda i, f: (nxt(i) // tpb, 0, 0)),
                     pl.BlockSpec((None,) + scale.shape[1:], lambda i, f: (nxt(i) // tpb, 0, 0))]
        args += [x, shift, scale]
    if final_g is not None:
        in_specs.append(vec)
        args.append(final_g)
    return pl.pallas_call(
        functools.partial(_ffn_kernel, final=final_g is not None, ahead=ahead, nf=nf),
        grid=(nt, nf), in_specs=in_specs,
        out_specs=pl.BlockSpec((tm, d), lambda i, f: (i, 0)),
        out_shape=jax.ShapeDtypeStruct((m, d), F32),
        scratch_shapes=[pltpu.VMEM((2 if ahead else 1, tm, d), BF16), pltpu.VMEM((tm, d), F32)],
        compiler_params=_cp(2), name="ffn_final" if final_g is not None else "ffn",
    )(*args)


def _rope(r, a, b, c):
    half_shift = (LANES // 8) // 2
    cols = []
    for j in range(r.shape[1] // LANES):
        x = r[:, j * LANES:(j + 1) * LANES]
        cols.append(x * a + pltpu.roll(x, LANES - half_shift, 1) * b + pltpu.roll(x, half_shift, 1) * c)
    return jnp.concatenate(cols, axis=1)


def _dup_halves(x, hd):
    lo = lax.broadcasted_iota(jnp.int32, (x.shape[0], LANES), 1) < hd
    zero = jnp.zeros((x.shape[0], LANES), x.dtype)
    out = []
    for j in range(x.shape[1] // LANES):
        p = x[:, j * LANES:(j + 1) * LANES]
        r = pltpu.roll(p, hd, 1)
        out += [jnp.where(lo, p, zero), jnp.where(lo, zero, r), jnp.where(lo, r, zero), jnp.where(lo, zero, p)]
    return jnp.concatenate(out, axis=1)


def _swa_in_kernel(x_ref, g_ref, sh_ref, sc_ref, w_ref, ra_ref, rb_ref, rc_ref,
                   q_ref, kk_ref, vv_ref, k_ref, v_ref, h_ref, *, kvd, hd, q_scale):
    _norm_mod_store(h_ref, x_ref, g_ref, sh_ref, sc_ref)
    h = h_ref[...]
    tabs = (ra_ref[...], rb_ref[...], rc_ref[...])
    d = q_ref.shape[1]
    tn = 2 * kvd
    for n in range(d // tn):
        r = _dot(h, w_ref[:, n * tn:(n + 1) * tn])
        q_ref[:, n * tn:(n + 1) * tn] = (_rope(r, *tabs) * q_scale).astype(BF16)
    r = _dot(h, w_ref[:, d:])
    k = _rope(r[:, :kvd], *tabs)
    v = r[:, kvd:]
    k_ref[...] = k
    v_ref[...] = v
    kk_ref[...] = _dup_halves(k, hd).astype(BF16)
    vv_ref[...] = _dup_halves(v, hd).astype(BF16)


def _swa_in(x, g, shift, scale, w, tabs, *, layer, tm, tpb, kvd, hd):
    m, d = x.shape
    ntab = tabs[0].shape[0] // tm
    kkw = (kvd // hd) * 2 * LANES
    tab_spec = pl.BlockSpec((tm, LANES), lambda i, n: (i % ntab, 0))
    const = lambda w_: pl.BlockSpec((tm, w_), lambda i, n: (i, 0))
    return pl.pallas_call(
        functools.partial(_swa_in_kernel, kvd=kvd, hd=hd, q_scale=hd ** -0.5 * LOG2E),
        grid=(m // tm, 1),
        in_specs=[pl.BlockSpec((tm, d), lambda i, n: (i, 0)),
                  pl.BlockSpec((1, d), lambda i, n: (0, 0)),
                  _mod_spec(shift, tpb), _mod_spec(scale, tpb),
                  pl.BlockSpec((None, d, d + 2 * kvd), lambda i, n: (layer, 0, 0)),
                  tab_spec, tab_spec, tab_spec],
        out_specs=[const(d), const(kkw), const(kkw), const(kvd), const(kvd)],
        out_shape=[jax.ShapeDtypeStruct((m, d), BF16),
                   jax.ShapeDtypeStruct((m, kkw), BF16), jax.ShapeDtypeStruct((m, kkw), BF16),
                   jax.ShapeDtypeStruct((m, kvd), F32), jax.ShapeDtypeStruct((m, kvd), F32)],
        scratch_shapes=[pltpu.VMEM((tm, d), BF16)],
        compiler_params=_cp(2), name="swa_in_proj",
    )(x, g, shift, scale, w, *tabs)


def _rope_tables(pos, hd, rope_dims):
    half = rope_dims // 2
    inv_freq = ROPE_THETA ** (-jnp.arange(half, dtype=F32) * 2.0 / rope_dims)
    ang = pos.astype(F32)[:, None] * inv_freq[None, :]
    cos, sin = jnp.cos(ang), jnp.sin(ang)
    lane = np.arange(LANES) % hd
    idx = lane % half
    a = jnp.where(lane < rope_dims, cos[:, idx], 1.0)
    b = jnp.where(lane < half, -sin[:, idx], 0.0)
    c = jnp.where((lane >= half) & (lane < rope_dims), sin[:, idx], 0.0)
    return a, b, c


def _swa_core(q, k_top, k_bot, v_top, v_bot, bias, sinks, *, group):
    t = q.shape[0]
    npair = group // 2
    win = k_top.shape[0]
    half_lanes = lax.broadcasted_iota(jnp.int32, (win, LANES), 1) < LANES // 2
    count_even = jnp.where(half_lanes, 1.0, 0.0).astype(BF16)
    count_odd = jnp.where(half_lanes, 0.0, 1.0).astype(BF16)
    v2 = jnp.concatenate([jnp.concatenate([v_top, count_even], axis=1),
                          jnp.concatenate([v_bot, count_odd], axis=1)], axis=0)
    qs = jnp.concatenate([q[:, p * LANES:(p + 1) * LANES] for p in range(npair)], axis=0)
    s = _dot_nt(qs, jnp.concatenate([k_top, k_bot], axis=0)) + bias
    ps, es = [], []
    for half in range(2):
        cols = [s[:, half * win + c * LANES: half * win + (c + 1) * LANES] for c in range(win // LANES)]
        sink = sinks[:, half * LANES:(half + 1) * LANES]
        m = jnp.maximum(jnp.max(functools.reduce(jnp.maximum, cols), axis=-1, keepdims=True), sink)
        ps += [jnp.exp2(col - m) for col in cols]
        es.append(jnp.exp2(sink - m))
    o = _dot(jnp.concatenate(ps, axis=1).astype(BF16), v2)
    lane = lax.broadcasted_iota(jnp.int32, (o.shape[0], LANES), 1)
    o = o[:, :LANES] / (o[:, LANES:] + jnp.where(lane < LANES // 2, es[0], es[1]))
    return jnp.concatenate([o[p * t:(p + 1) * t] for p in range(npair)], axis=1)


def _swa_attn_kernel(q_ref, kh_ref, km_ref, vh_ref, vm_ref, bias_ref, sink_ref, o_ref, *, kv, group, hd):
    gw = group * hd
    bias = bias_ref[...]
    for g in range(kv):
        def win(halo, main, off):
            sl = slice(g * 2 * LANES + off, g * 2 * LANES + off + LANES)
            return jnp.concatenate([halo[:, sl], main[:, sl]], axis=0)
        o = _swa_core(q_ref[:, g * gw:(g + 1) * gw],
                      win(kh_ref, km_ref, 0), win(kh_ref, km_ref, LANES),
                      win(vh_ref, vm_ref, 0), win(vh_ref, vm_ref, LANES),
                      bias, sink_ref[g], group=group)
        o_ref[:, g * gw:(g + 1) * gw] = o.astype(o_ref.dtype)


def _swa_attn(q, kk, vv, bias2, sink_rows, *, nb, s, kv, group, hd):
    t = 2 * CHUNK
    nt = s // t
    d = q.shape[1]
    kkw = kk.shape[1]
    main = pl.BlockSpec((t, kkw), lambda b, i: (b * nt + i, 0))
    halo = pl.BlockSpec((t, kkw), lambda b, i: (b * nt + jnp.maximum(i - 1, 0), 0))
    return pl.pallas_call(
        functools.partial(_swa_attn_kernel, kv=kv, group=group, hd=hd),
        grid=(nb, nt),
        in_specs=[pl.BlockSpec((t, d), lambda b, i: (b * nt + i, 0)),
                  halo, main, halo, main,
                  pl.BlockSpec((None,) + bias2.shape[1:], lambda b, i: (jnp.minimum(i, 1), 0, 0)),
                  pl.BlockSpec(sink_rows.shape, lambda b, i: (0, 0, 0))],
        out_specs=pl.BlockSpec((t, d), lambda b, i: (b * nt + i, 0)),
        out_shape=jax.ShapeDtypeStruct(q.shape, BF16),
        compiler_params=_cp(2), name="swa_attention",
    )(q, kk, kk, vv, vv, bias2, sink_rows)


def _swa_sample_kernel(q_ref, kc_ref, vc_ref, kkn_ref, vvn_ref, bias_ref, sink_ref, o_ref, *, kv, group, hd, t):
    gw = group * hd
    kkc = _dup_halves(kc_ref[...], hd).astype(BF16)
    vvc = _dup_halves(vc_ref[...], hd).astype(BF16)
    pad = jnp.zeros((kc_ref.shape[0] - t, LANES), BF16)
    for g in range(kv):
        def win(cache, new, off):
            sl = slice(g * 2 * LANES + off, g * 2 * LANES + off + LANES)
            return jnp.concatenate([cache[:, sl], new[:, sl], pad], axis=0)
        o = _swa_core(q_ref[:, g * gw:(g + 1) * gw],
                      win(kkc, kkn_ref, 0), win(kkc, kkn_ref, LANES),
                      win(vvc, vvn_ref, 0), win(vvc, vvn_ref, LANES),
                      bias_ref[...], sink_ref[g], group=group)
        o_ref[:, g * gw:(g + 1) * gw] = o.astype(o_ref.dtype)


def _swa_sample_attn(q, k_cache, v_cache, kk_new, vv_new, bias, sink_rows, *, kv, group, hd, t):
    nb, buf, kvd = k_cache.shape
    d = q.shape[1]
    full = lambda a: pl.BlockSpec(a.shape, lambda b: (0,) * a.ndim)
    cache_spec = pl.BlockSpec((None, buf, kvd), lambda b: (b, 0, 0))
    new_spec = pl.BlockSpec((t, kk_new.shape[1]), lambda b: (b, 0))
    return pl.pallas_call(
        functools.partial(_swa_sample_kernel, kv=kv, group=group, hd=hd, t=t),
        grid=(nb,),
        in_specs=[pl.BlockSpec((t, d), lambda b: (b, 0)), cache_spec, cache_spec, new_spec, new_spec,
                  full(bias), full(sink_rows)],
        out_specs=pl.BlockSpec((t, d), lambda b: (b, 0)),
        out_shape=jax.ShapeDtypeStruct(q.shape, BF16),
        compiler_params=_cp(1), name="swa_sample_attention",
    )(q, k_cache, v_cache, kk_new, vv_new, bias, sink_rows)


def _window_bias(valid, npair):
    b = np.where(valid, 0.0, NEG).astype(np.float32)
    return np.tile(b, (npair, 2))


def _prompt_bias(npair):
    t = 2 * CHUNK
    qc = np.arange(t)[:, None] // CHUNK
    kc = np.arange(t + SWA_WINDOW_CHUNKS * CHUNK)[None, :] // CHUNK
    valid = (kc >= qc) & (kc <= qc + SWA_WINDOW_CHUNKS)
    first = valid & (kc >= SWA_WINDOW_CHUNKS)
    return jnp.asarray(np.stack([_window_bias(first, npair), _window_bias(valid, npair)]))


def _sample_bias(past_len, buf, t, npair):
    q_pos = past_len + np.arange(t)
    k_pos = np.concatenate([past_len - buf + np.arange(buf), q_pos])
    qch, kch = q_pos // CHUNK, k_pos // CHUNK
    valid = np.zeros((t, 2 * buf), bool)
    valid[:, :buf + t] = (kch[None, :] <= qch[:, None]) & (kch[None, :] >= qch[:, None] - SWA_WINDOW_CHUNKS)
    return jnp.asarray(_window_bias(valid, npair))


def _sink_rows(sinks, kv, group, t):
    s = (sinks * LOG2E).reshape(kv, group // 2, 1, 2, 1)
    return jnp.broadcast_to(s, (kv, group // 2, t, 2, LANES)).reshape(kv, (group // 2) * t, 2 * LANES)


def kernel(x_prompt, x_sample, c_prompt, c_sample, cache_fox_k, cache_fox_v, cache_fox_logf, cache_swa_k,
           cache_swa_v, ada_w, ada_b, norm_mix_g, norm_ffn_g, fox_w_in, fox_b_f, fox_w_out, swa_w_in,
           swa_sinks, swa_w_out, ffn_w_up, ffn_w_down, final_g):
    bp, s, d = x_prompt.shape
    bs, t, _ = x_sample.shape
    depth = ada_w.shape[0]
    past_len = cache_fox_k.shape[2]
    nh_fox, hd_fox = cache_fox_k.shape[3], cache_fox_k.shape[4]
    buf, kv, hd_swa = cache_swa_k.shape[2], cache_swa_k.shape[3], cache_swa_k.shape[4]
    nh_swa = swa_sinks.shape[1]
    group = nh_swa // kv
    kvd = kv * hd_swa
    rope_dims = hd_swa // 4
    assert hd_fox == LANES and hd_swa == LANES // 2 and rope_dims == 16 and group % 2 == 0
    assert buf == SWA_WINDOW_CHUNKS * CHUNK and t <= buf and s % (2 * CHUNK) == 0

    mp, ms = bp * s, bs * t
    tm_p = min(512, s)
    tpb_p = s // tm_p
    tm_fox = min(1024, s)
    tk = min(FOX_TK, s)
    tq = min(FOX_NSUB * tk, s)

    mods = _ada(jnp.concatenate([c_prompt, c_sample], axis=0), ada_w, ada_b)

    def split_mods(i):
        six = jnp.split(mods[i], 6, axis=-1)
        prompt = [m[:bp, None, :] for m in six]
        sample = [jnp.repeat(m[bp:], t, axis=0)[None] for m in six]
        return prompt, sample

    row = lambda v: v.reshape(1, -1)
    xp = x_prompt.reshape(mp, d)
    xs = x_sample.reshape(ms, d)
    pos_p = jnp.arange(s)
    pos_s = past_len + jnp.arange(t)
    tabs_p = _rope_tables(pos_p, hd_swa, rope_dims)
    tabs_s = tuple(jnp.tile(a, (bs, 1)) for a in _rope_tables(pos_s, hd_swa, rope_dims))
    bias_p = _prompt_bias(group // 2)
    bias_s = _sample_bias(past_len, buf, t, group // 2)

    fox_w_in_b, fox_w_out_b = _to_bf16(fox_w_in), _to_bf16(fox_w_out)
    swa_w_in_b, swa_w_out_b = _to_bf16(swa_w_in), _to_bf16(swa_w_out)
    w_up_b, w_down_b = _to_bf16(ffn_w_up), _to_bf16(ffn_w_down)

    outs = {k: [] for k in ("fkp", "fvp", "flp", "fks", "fvs", "fls", "skp", "svp", "sks", "svs")}
    for i in range(depth):
        mod_p, mod_s = split_mods(i)
        j = i // 2
        if i % 2 == 0:
            w_f = jnp.pad(fox_w_in[j, :, 3 * d:], ((0, 0), (0, LANES - nh_fox))).astype(BF16)
            b_f = row(fox_b_f[j])
            qh, kh, vh, k32, v32, lf, qn2, kn2 = _fox_in(
                xp, row(norm_mix_g[i]), mod_p[0], mod_p[1], fox_w_in_b, w_f, b_f,
                layer=j, tm=tm_fox, tpb=s // tm_fox, head_major=True, nh=nh_fox, hd=hd_fox)
            lft = lf.reshape(bp, s, nh_fox).transpose(0, 2, 1)
            c = _cumsum_rows(lft.reshape(-1, LANES), s // LANES).reshape(bp * nh_fox, s)
            first = _fox_first_blocks(c, _head_norm_bound(qn2, bp, nh_fox, hd_fox),
                                      _head_norm_bound(kn2, bp, nh_fox, hd_fox), tk)
            op = _fox_attn(qh, kh, vh, c, first, tk=tk, nsub=tq // tk)
            xp = _proj_res(op.reshape(mp, d), fox_w_out_b, xp, mod_p[2], layer=j, tm=tm_p, tpb=tpb_p)
            outs["fkp"].append(k32.reshape(bp, s, nh_fox, hd_fox))
            outs["fvp"].append(v32.reshape(bp, s, nh_fox, hd_fox))
            outs["flp"].append(lf.reshape(bp, s, nh_fox))
            qs_, k32s, v32s, lfs = _fox_in(xs, row(norm_mix_g[i]), mod_s[0], mod_s[1], fox_w_in_b, w_f, b_f,
                                           layer=j, tm=ms, tpb=1, head_major=False, nh=nh_fox, hd=hd_fox)
            lf_all = jnp.concatenate([cache_fox_logf[j], lfs.reshape(bs, t, nh_fox)], axis=1)
            lc = LANES * int(2 ** np.ceil(np.log2(-(-(past_len + t) // LANES))))
            lf_all = jnp.pad(lf_all.transpose(0, 2, 1), ((0, 0), (0, 0), (0, lc - past_len - t)))
            c_s = _cumsum_rows(lf_all.reshape(-1, LANES), lc // LANES).reshape(bs, nh_fox, lc)
            os_ = _fox_sample_attn(qs_, cache_fox_k, cache_fox_v, k32s, v32s, c_s, layer=j, t=t)
            xs = _proj_res(os_, fox_w_out_b, xs, mod_s[2], layer=j, tm=ms, tpb=1)
            outs["fks"].append(k32s.reshape(bs, t, nh_fox, hd_fox))
            outs["fvs"].append(v32s.reshape(bs, t, nh_fox, hd_fox))
            outs["fls"].append(lfs.reshape(bs, t, nh_fox))
        else:
            q, kk, vv, k32, v32 = _swa_in(xp, row(norm_mix_g[i]), mod_p[0], mod_p[1], swa_w_in_b, tabs_p,
                                          layer=j, tm=tm_p, tpb=tpb_p, kvd=kvd, hd=hd_swa)
            op = _swa_attn(q, kk, vv, bias_p, _sink_rows(swa_sinks[j], kv, group, 2 * CHUNK),
                           nb=bp, s=s, kv=kv, group=group, hd=hd_swa)
            xp = _proj_res(op, swa_w_out_b, xp, mod_p[2], layer=j, tm=tm_p, tpb=tpb_p)
            outs["skp"].append(k32.reshape(bp, s, kvd)[:, -buf:].reshape(bp, buf, kv, hd_swa))
            outs["svp"].append(v32.reshape(bp, s, kvd)[:, -buf:].reshape(bp, buf, kv, hd_swa))
            q, kk, vv, k32, v32 = _swa_in(xs, row(norm_mix_g[i]), mod_s[0], mod_s[1], swa_w_in_b, tabs_s,
                                          layer=j, tm=ms, tpb=1, kvd=kvd, hd=hd_swa)
            os_ = _swa_sample_attn(q, cache_swa_k[j].reshape(bs, buf, kvd), cache_swa_v[j].reshape(bs, buf, kvd),
                                   kk, vv, bias_s, _sink_rows(swa_sinks[j], kv, group, t),
                                   kv=kv, group=group, hd=hd_swa, t=t)
            xs = _proj_res(os_, swa_w_out_b, xs, mod_s[2], layer=j, tm=ms, tpb=1)
            k_all = jnp.concatenate([cache_swa_k[j], k32.reshape(bs, t, kv, hd_swa)], axis=1)
            v_all = jnp.concatenate([cache_swa_v[j], v32.reshape(bs, t, kv, hd_swa)], axis=1)
            outs["sks"].append(k_all[:, -buf:])
            outs["svs"].append(v_all[:, -buf:])
        fg = row(final_g) if i == depth - 1 else None
        xp = _ffn(xp, row(norm_ffn_g[i]), mod_p[3], mod_p[4], mod_p[5], w_up_b, w_down_b, fg,
                  layer=i, tm=tm_p, tpb=tpb_p)
        xs = _ffn(xs, row(norm_ffn_g[i]), mod_s[3], mod_s[4], mod_s[5], w_up_b, w_down_b, fg,
                  layer=i, tm=ms, tpb=1)

    st = lambda k: jnp.stack(outs[k])
    return (xp.reshape(bp, s, d), xs.reshape(bs, t, d),
            st("fkp"), st("fvp"), st("flp"), st("fks"), st("fvs"), st("fls"),
            st("skp"), st("svp"), st("sks"), st("svs"))
```

```python
import functools

import numpy as np
import jax
import jax.numpy as jnp
from jax import lax
from jax.experimental import pallas as pl
from jax.experimental.pallas import tpu as pltpu

F32 = jnp.float32
BF16 = jnp.bfloat16

RMS_EPS = 1e-6
CHUNK = 64
SWA_WINDOW_CHUNKS = 2
ROPE_THETA = 500000.0
LANES = 128
NEG = -1e30
NEG_BLOCK = -3e38
VMEM_LIMIT_BYTES = 56 * 1024 * 1024
LOG2E = 1.4426950408889634
FOX_SKIP_LOG2 = 150.0
FFN_STEP_ELEMS = 512 * 1024
FOX_IN_COLS = 512
FOX_TK = 512
FOX_NSUB = 8


def _cp(n_axes):
    return pltpu.CompilerParams(dimension_semantics=("arbitrary",) * n_axes,
                                vmem_limit_bytes=VMEM_LIMIT_BYTES)


def _dot(a, b):
    return jnp.dot(a, b, preferred_element_type=F32)


def _dot_nt(a, b):
    return lax.dot_general(a, b, (((1,), (1,)), ((), ())), preferred_element_type=F32)


def _rms(x):
    return x * lax.rsqrt(jnp.mean(x * x, axis=-1, keepdims=True) + RMS_EPS)


NORM_ROWS = 16


def _norm_mod_store(h_ref, x_ref, g_ref, sh_ref, sc_ref):
    per_token = sh_ref.shape[0] > 1
    g = g_ref[...]
    if not per_token:
        gain, shift = g * (1.0 + sc_ref[...]), sh_ref[...]

    def body(i, carry):
        rows = pl.ds(pl.multiple_of(i * NORM_ROWS, NORM_ROWS), NORM_ROWS)
        x = x_ref[rows, :]
        if per_token:
            gain_, shift_ = g * (1.0 + sc_ref[rows, :]), sh_ref[rows, :]
        else:
            gain_, shift_ = gain, shift
        h_ref[rows, :] = (_rms(x) * gain_ + shift_).astype(h_ref.dtype)
        return carry

    lax.fori_loop(0, x_ref.shape[0] // NORM_ROWS, body, 0, unroll=4)


def _log_sigmoid(z):
    return jnp.minimum(z, 0.0) - jnp.log1p(jnp.exp(-jnp.abs(z)))


CAST_BLOCK_BYTES = 8 * 1024 * 1024


def _cast_kernel(x_ref, o_ref):
    o_ref[...] = x_ref[...].astype(o_ref.dtype)


CAST_MAX_COLS = 2048


def _to_bf16(w):
    layers, rows, cols = w.shape
    tc = CAST_MAX_COLS if cols % CAST_MAX_COLS == 0 else cols
    tr = rows
    while tr * tc * 4 > CAST_BLOCK_BYTES and tr % 32 == 0:
        tr //= 2
    spec = pl.BlockSpec((None, tr, tc), lambda l, i, j: (l, i, j))
    return pl.pallas_call(
        _cast_kernel, grid=(layers, rows // tr, cols // tc),
        in_specs=[spec], out_specs=spec,
        out_shape=jax.ShapeDtypeStruct(w.shape, BF16),
        compiler_params=_cp(3), name="weight_to_bf16",
    )(w)


def _ada_kernel(c_ref, w_ref, b_ref, o_ref):
    c = c_ref[...]
    a = (c / (1.0 + jnp.exp(-c))).astype(BF16)
    o_ref[...] = _dot(a, w_ref[...].astype(BF16)) + b_ref[...]


def _ada(c_all, ada_w, ada_b):
    depth, d, n6 = ada_w.shape
    r = c_all.shape[0]
    tn = min(1024, n6)
    return pl.pallas_call(
        _ada_kernel,
        grid=(depth, n6 // tn),
        in_specs=[pl.BlockSpec((r, d), lambda l, n: (0, 0)),
                  pl.BlockSpec((None, d, tn), lambda l, n: (l, 0, n)),
                  pl.BlockSpec((None, 1, tn), lambda l, n: (l, 0, n))],
        out_specs=pl.BlockSpec((None, r, tn), lambda l, n: (l, 0, n)),
        out_shape=jax.ShapeDtypeStruct((depth, r, n6), F32),
        compiler_params=_cp(2), name="ada_params",
    )(c_all, ada_w, ada_b.reshape(depth, 1, n6))


def _fox_in_kernel(x_ref, g_ref, sh_ref, sc_ref, wq_ref, wk_ref, wv_ref, wf_ref, bf_ref, *rest,
                   head_major, hd, nh, q_scale):
    if head_major:
        qb_ref, kb_ref, vb_ref, k_ref, v_ref, lf_ref, n2_ref, h_ref = rest
    else:
        qb_ref, k_ref, v_ref, lf_ref, h_ref = rest

    @pl.when(pl.program_id(1) == 0)
    def _():
        _norm_mod_store(h_ref, x_ref, g_ref, sh_ref, sc_ref)
        z = _dot(h_ref[...], wf_ref[...])[:, :nh] + bf_ref[...]
        lf_ref[...] = _log_sigmoid(z)

    hb = h_ref[...]
    q = _dot(hb, wq_ref[...]) * q_scale
    k = _dot(hb, wk_ref[...])
    v = _dot(hb, wv_ref[...])
    k_ref[...] = k
    v_ref[...] = v
    if head_major:
        for j in range(q.shape[1] // hd):
            sl = slice(j * hd, (j + 1) * hd)
            qb_ref[j] = q[:, sl].astype(BF16)
            kb_ref[j] = k[:, sl].astype(BF16)
            vb_ref[j] = v[:, sl].astype(BF16)
        tm, tn = q.shape
        grp = lax.shift_right_logical(lax.broadcasted_iota(jnp.int32, (2 * tn, LANES), 0), hd.bit_length() - 1)
        sel = jnp.where(grp == lax.broadcasted_iota(jnp.int32, (2 * tn, LANES), 1), 1.0, 0.0).astype(BF16)
        n2 = _dot(jnp.concatenate([q * q, k * k], axis=1).astype(BF16), sel)
        n2_ref[...] = jnp.max(n2.reshape(tm // 8, 8, LANES), axis=0)
    else:
        qb_ref[...] = q.astype(BF16)


def _mod_spec(mod, tpb, cols=None):
    _, rows, d = mod.shape
    if cols is None:
        return pl.BlockSpec((None, rows, d), lambda i, n: (i // tpb, 0, 0))
    return pl.BlockSpec((None, rows, cols), lambda i, n: (i // tpb, 0, n))


def _fox_in(x, g, shift, scale, w_in, w_f, b_f, *, layer, tm, tpb, head_major, nh, hd):
    m, d = x.shape
    tn = min(FOX_IN_COLS, d)
    nt = d // tn
    hpt = tn // hd
    nb, s = m // (tm * tpb), tm * tpb
    in_specs = [pl.BlockSpec((tm, d), lambda i, n: (i, 0)),
                pl.BlockSpec((1, d), lambda i, n: (0, 0)),
                _mod_spec(shift, tpb), _mod_spec(scale, tpb),
                pl.BlockSpec((None, d, tn), lambda i, n: (layer, 0, n)),
                pl.BlockSpec((None, d, tn), lambda i, n: (layer, 0, n + nt)),
                pl.BlockSpec((None, d, tn), lambda i, n: (layer, 0, n + 2 * nt)),
                pl.BlockSpec((d, LANES), lambda i, n: (0, 0)),
                pl.BlockSpec((1, nh), lambda i, n: (0, 0))]
    rm_spec = pl.BlockSpec((tm, tn), lambda i, n: (i, n))
    lf_spec = pl.BlockSpec((tm, nh), lambda i, n: (i, 0))
    if head_major:
        hm_spec = pl.BlockSpec((None, hpt, tm, hd), lambda i, n: (i // tpb, n, i % tpb, 0))
        hm_shape = jax.ShapeDtypeStruct((nb, nh, s, hd), BF16)
        assert hd & (hd - 1) == 0 and 2 * hpt <= LANES
        n2_spec = pl.BlockSpec((8, LANES), lambda i, n: (i, n))
        out_specs = [hm_spec, hm_spec, hm_spec, rm_spec, rm_spec, lf_spec, n2_spec]
        out_shape = [hm_shape, hm_shape, hm_shape]
    else:
        out_specs = [rm_spec, rm_spec, rm_spec, lf_spec]
        out_shape = [jax.ShapeDtypeStruct((m, d), BF16)]
    out_shape += [jax.ShapeDtypeStruct((m, d), F32), jax.ShapeDtypeStruct((m, d), F32),
                  jax.ShapeDtypeStruct((m, nh), F32)]
    if head_major:
        out_shape += [jax.ShapeDtypeStruct((m // tm * 8, nt * LANES), F32)]
    return pl.pallas_call(
        functools.partial(_fox_in_kernel, head_major=head_major, hd=hd, nh=nh, q_scale=hd ** -0.5 * LOG2E),
        grid=(m // tm, nt), in_specs=in_specs, out_specs=out_specs, out_shape=out_shape,
        scratch_shapes=[pltpu.VMEM((tm, d), BF16)],
        compiler_params=_cp(2), name="fox_in_proj",
    )(x, g, shift, scale, w_in, w_in, w_in, w_f, b_f)


def _split3(x):
    x1 = x.astype(BF16)
    r1 = x - x1.astype(F32)
    x2 = r1.astype(BF16)
    x3 = (r1 - x2.astype(F32)).astype(BF16)
    return x1, x2, x3


def _cumsum_kernel(x_ref, u_ref, l_ref, o_ref):
    u = u_ref[...]
    lo = l_ref[...]
    y = sum(_dot(p, u) for p in _split3(x_ref[...]))
    tot = jnp.broadcast_to(y[:, LANES - 1:LANES], y.shape)
    o_ref[...] = (y + sum(_dot(lo, p) for p in _split3(tot))) * LOG2E


def _cumsum_rows(x, nc):
    rows = x.shape[0]
    assert LANES % nc == 0 or nc % LANES == 0
    n = max(nc, LANES)
    assert rows % n == 0
    u = jnp.asarray(np.triu(np.ones((LANES, LANES), np.float32)), BF16)
    idx = np.arange(n)
    same = (idx[:, None] // nc) == (idx[None, :] // nc)
    lo = jnp.asarray((same & (idx[None, :] < idx[:, None])).astype(np.float32), BF16)
    return pl.pallas_call(
        _cumsum_kernel, grid=(rows // n,),
        in_specs=[pl.BlockSpec((n, LANES), lambda i: (i, 0)),
                  pl.BlockSpec((LANES, LANES), lambda i: (0, 0)),
                  pl.BlockSpec((n, n), lambda i: (0, 0))],
        out_specs=pl.BlockSpec((n, LANES), lambda i: (i, 0)),
        out_shape=jax.ShapeDtypeStruct(x.shape, F32),
        compiler_params=_cp(1), name="logf_cumsum",
    )(x, u, lo)


NORM_SLACK = 1.01


def _head_norm_bounds(n2, nb, nh, hpt):
    n2 = jnp.max(n2.reshape(nb, -1, nh // hpt, LANES), axis=1)
    bound = lambda part: (NORM_SLACK * jnp.sqrt(part)).reshape(-1)
    return bound(n2[:, :, :hpt]), bound(n2[:, :, hpt:2 * hpt])


def _fox_first_blocks(c, qmax, kmax, tk):
    bh, s = c.shape
    cb = c.reshape(bh, s // tk, tk)
    c_start, c_end = cb[:, :, 0], cb[:, :, -1]
    budget = (2.0 * qmax * kmax + FOX_SKIP_LOG2)[:, None, None]
    skippable = (c_end[:, None, :] - c_start[:, :, None]) > budget
    earlier = np.tril(np.ones((s // tk, s // tk), bool), -1)
    return jnp.sum(skippable & earlier, axis=-1).astype(jnp.int32).reshape(-1)


def _fox_attn_kernel(first_ref, q_ref, k_ref, v_ref, c_ref, o_ref, m_ref, acc_ref, *, tk, nsub, hd):
    b, h, qi = pl.program_id(0), pl.program_id(1), pl.program_id(2)
    nkb = pl.num_programs(2) * nsub
    m_ref[...] = jnp.full(m_ref.shape, NEG, F32)
    acc_ref[...] = jnp.zeros(acc_ref.shape, F32)
    ones = jnp.ones((tk, hd), BF16)
    ncol = tk // LANES

    def step(r, j, mask=None, valid=None):
        rows = pl.ds(pl.multiple_of(j * tk, tk), tk)
        c_row = c_ref[j]
        if valid is not None:
            c_row = jnp.where(valid, c_row, -NEG_BLOCK)
        s = _dot_nt(q_ref[r * tk:(r + 1) * tk, :], k_ref[rows, :]) - c_row
        if mask is not None:
            s = jnp.where(mask, s, NEG)
        cols = [s[:, c * LANES:(c + 1) * LANES] for c in range(ncol)]
        m_cur = functools.reduce(jnp.maximum, cols)
        m_prev = m_ref[r]
        m_next = jnp.maximum(m_prev, jnp.max(m_cur, axis=-1, keepdims=True))
        alpha = jnp.exp2(m_prev - m_next)
        p = jnp.concatenate([jnp.exp2(col - m_next) for col in cols], axis=1).astype(BF16)
        pv = _dot(p, jnp.concatenate([v_ref[rows, :], ones], axis=1))
        acc_ref[r] = jnp.concatenate([alpha, alpha], axis=1) * acc_ref[r] + pv
        m_ref[r] = m_next

    base = qi * nsub
    off = (b * pl.num_programs(1) + h) * nkb + base
    n = functools.reduce(jnp.maximum, [base + r - first_ref[off + r] for r in range(nsub)])

    def body(t, carry):
        for r in range(nsub):
            j = base + r - n + t
            step(r, jnp.maximum(j, 0), valid=j >= 0)
        return carry

    lax.fori_loop(0, n, body, 0)
    row = lax.broadcasted_iota(jnp.int32, (tk, tk), 0)
    col = lax.broadcasted_iota(jnp.int32, (tk, tk), 1)
    causal = col <= row
    for r in range(nsub):
        step(r, base + r, mask=causal)
    for r in range(nsub):
        acc = acc_ref[r]
        o_ref[r * tk:(r + 1) * tk, :] = (acc[:, :hd] / acc[:, hd:]).astype(o_ref.dtype)


def _fox_attn(qh, kh, vh, c, first, *, tk, nsub):
    nb, nh, s, hd = qh.shape
    tq = tk * nsub
    nq = s // tq
    kv_spec = pl.BlockSpec((None, None, s, hd), lambda b, h, i, f: (b, h, 0, 0))
    return pl.pallas_call(
        functools.partial(_fox_attn_kernel, tk=tk, nsub=nsub, hd=hd),
        grid_spec=pltpu.PrefetchScalarGridSpec(
            num_scalar_prefetch=1, grid=(nb, nh, nq),
            in_specs=[pl.BlockSpec((None, None, tq, hd), lambda b, h, i, f: (b, h, i, 0)),
                      kv_spec, kv_spec,
                      pl.BlockSpec((None, s // tk, 1, tk), lambda b, h, i, f: (b * nh + h, 0, 0, 0))],
            out_specs=pl.BlockSpec((None, tq, hd), lambda b, h, i, f: (b, i, h)),
            scratch_shapes=[pltpu.VMEM((nsub, tk, LANES), F32), pltpu.VMEM((nsub, tk, 2 * hd), F32)]),
        out_shape=jax.ShapeDtypeStruct((nb, s, nh * hd), BF16),
        compiler_params=_cp(3), name="fox_attention",
    )(first, qh, kh, vh, c.reshape(nb * nh, s // tk, 1, tk))


def _fox_sample_kernel(q_ref, kc_ref, vc_ref, kn_ref, vn_ref, c_ref, cf_ref, o_ref, *, hb, hd, p_len, t):
    n = p_len * hb
    kf = kc_ref[...].reshape(n, hd).astype(BF16)
    vf = vc_ref[...].reshape(n, hd).astype(BF16)
    heads = [slice(j * hd, (j + 1) * hd) for j in range(hb)]
    q8 = jnp.concatenate([q_ref[:, sl] for sl in heads], axis=0)
    row_head = lax.shift_right_logical(lax.broadcasted_iota(jnp.int32, (hb * t, LANES), 0), t.bit_length() - 1)
    col_head = lax.broadcasted_iota(jnp.int32, (hb * t, LANES), 1) & (hb - 1)
    own = jnp.where(row_head == col_head, 0.0, NEG)
    s = _dot_nt(q8, kf)
    cf = cf_ref[...]
    cols = [s[:, g * LANES:(g + 1) * LANES] - cf[:, g * LANES:(g + 1) * LANES] + own for g in range(n // LANES)]
    tri = lax.broadcasted_iota(jnp.int32, (t, t), 1) <= lax.broadcasted_iota(jnp.int32, (t, t), 0)
    s_n = jnp.concatenate(
        [jnp.where(tri, _dot_nt(q_ref[:, sl], kn_ref[:, sl].astype(BF16)) - c_ref[j:j + 1, p_len:p_len + t], NEG)
         for j, sl in enumerate(heads)], axis=0)
    m = jnp.maximum(jnp.max(functools.reduce(jnp.maximum, cols), axis=-1, keepdims=True),
                    jnp.max(s_n, axis=-1, keepdims=True))
    ps = [jnp.exp2(col - m) for col in cols]
    p_n = jnp.exp2(s_n - m)
    den = jnp.sum(functools.reduce(jnp.add, ps), axis=-1, keepdims=True) + jnp.sum(p_n, axis=-1, keepdims=True)
    o = _dot(jnp.concatenate(ps, axis=1).astype(BF16), vf)
    o_n = jnp.concatenate([_dot(p_n[j * t:(j + 1) * t].astype(BF16), vn_ref[:, sl].astype(BF16))
                           for j, sl in enumerate(heads)], axis=0)
    o = (o + o_n) / den
    for j, sl in enumerate(heads):
        o_ref[:, sl] = o[j * t:(j + 1) * t].astype(o_ref.dtype)


def _fox_sample_attn(q, k_cache, v_cache, k_new, v_new, c_all, *, layer, t):
    _, nb, p_len, nh, hd = k_cache.shape
    hb = min(8, nh)
    assert t & (t - 1) == 0 and hb & (hb - 1) == 0 and (p_len * hb) % LANES == 0 and LANES % hb == 0
    w = hb * hd
    lc = c_all.shape[-1]
    c_flat = c_all[:, :, :p_len].reshape(nb, nh // hb, hb, p_len).transpose(0, 1, 3, 2)
    c_flat = c_flat.reshape(nb, nh // hb, 1, p_len * hb)
    cache_spec = pl.BlockSpec((None, None, p_len, hb, hd), lambda b, h: (layer, b, 0, h, 0))
    row_spec = pl.BlockSpec((t, w), lambda b, h: (b, h))
    return pl.pallas_call(
        functools.partial(_fox_sample_kernel, hb=hb, hd=hd, p_len=p_len, t=t),
        grid=(nb, nh // hb),
        in_specs=[row_spec, cache_spec, cache_spec, row_spec, row_spec,
                  pl.BlockSpec((None, hb, lc), lambda b, h: (b, h, 0)),
                  pl.BlockSpec((None, None, 1, p_len * hb), lambda b, h: (b, h, 0, 0))],
        out_specs=row_spec,
        out_shape=jax.ShapeDtypeStruct((nb * t, nh * hd), BF16),
        compiler_params=_cp(2), name="fox_sample_attention",
    )(q, k_cache, v_cache, k_new, v_new, c_all, c_flat)


def _proj_res_kernel(a_ref, w_ref, x_ref, gate_ref, o_ref):
    o_ref[...] = x_ref[...] + gate_ref[...] * _dot(a_ref[...], w_ref[...])


def _proj_res(a, w, x, gate, *, layer, tm, tpb):
    m, k = a.shape
    n = w.shape[2]
    return pl.pallas_call(
        _proj_res_kernel, grid=(m // tm, 1),
        in_specs=[pl.BlockSpec((tm, k), lambda i, j: (i, 0)),
                  pl.BlockSpec((None, k, n), lambda i, j: (layer, 0, 0)),
                  pl.BlockSpec((tm, n), lambda i, j: (i, 0)),
                  _mod_spec(gate, tpb)],
        out_specs=pl.BlockSpec((tm, n), lambda i, j: (i, 0)),
        out_shape=jax.ShapeDtypeStruct((m, n), F32),
        compiler_params=_cp(2), name="attn_out_proj",
    )(a, w, x, gate)


def _ffn_kernel(x_ref, g_ref, sh_ref, sc_ref, gate_ref, wu_ref, wd_ref, *rest, final):
    if final:
        fg_ref, o_ref, h_ref, acc_ref = rest
    else:
        o_ref, h_ref, acc_ref = rest
    f = pl.program_id(1)

    @pl.when(f == 0)
    def _():
        _norm_mod_store(h_ref, x_ref, g_ref, sh_ref, sc_ref)
        acc_ref[...] = jnp.zeros(acc_ref.shape, F32)

    a = jnp.maximum(_dot(h_ref[...], wu_ref[...]), 0.0)
    acc_ref[...] += _dot((a * a).astype(BF16), wd_ref[...])

    @pl.when(f == pl.num_programs(1) - 1)
    def _():
        y = x_ref[...] + gate_ref[...] * acc_ref[...]
        if final:
            y = _rms(y) * fg_ref[...]
        o_ref[...] = y


def _ffn(x, g, shift, scale, gate, w_up, w_down, final_g, *, layer, tm, tpb):
    m, d = x.shape
    ff = w_up.shape[2]
    tf = min(ff, FFN_STEP_ELEMS // tm)
    vec = pl.BlockSpec((1, d), lambda i, f: (0, 0))
    in_specs = [pl.BlockSpec((tm, d), lambda i, f: (i, 0)), vec,
                _mod_spec(shift, tpb), _mod_spec(scale, tpb), _mod_spec(gate, tpb),
                pl.BlockSpec((None, d, tf), lambda i, f: (layer, 0, f)),
                pl.BlockSpec((None, tf, d), lambda i, f: (layer, f, 0))]
    args = [x, g, shift, scale, gate, w_up, w_down]
    if final_g is not None:
        in_specs.append(vec)
        args.append(final_g)
    return pl.pallas_call(
        functools.partial(_ffn_kernel, final=final_g is not None),
        grid=(m // tm, ff // tf), in_specs=in_specs,
        out_specs=pl.BlockSpec((tm, d), lambda i, f: (i, 0)),
        out_shape=jax.ShapeDtypeStruct((m, d), F32),
        scratch_shapes=[pltpu.VMEM((tm, d), BF16), pltpu.VMEM((tm, d), F32)],
        compiler_params=_cp(2), name="ffn_final" if final_g is not None else "ffn",
    )(*args)


def _rope(r, a, b, c):
    half_shift = (LANES // 8) // 2
    cols = []
    for j in range(r.shape[1] // LANES):
        x = r[:, j * LANES:(j + 1) * LANES]
        cols.append(x * a + pltpu.roll(x, LANES - half_shift, 1) * b + pltpu.roll(x, half_shift, 1) * c)
    return jnp.concatenate(cols, axis=1)


def _dup_halves(x, hd):
    lo = lax.broadcasted_iota(jnp.int32, (x.shape[0], LANES), 1) < hd
    zero = jnp.zeros((x.shape[0], LANES), x.dtype)
    out = []
    for j in range(x.shape[1] // LANES):
        p = x[:, j * LANES:(j + 1) * LANES]
        r = pltpu.roll(p, hd, 1)
        out += [jnp.where(lo, p, zero), jnp.where(lo, zero, r), jnp.where(lo, r, zero), jnp.where(lo, zero, p)]
    return jnp.concatenate(out, axis=1)


def _swa_in_kernel(x_ref, g_ref, sh_ref, sc_ref, w_ref, ra_ref, rb_ref, rc_ref,
                   q_ref, kk_ref, vv_ref, k_ref, v_ref, h_ref, *, kvd, hd, q_scale):
    _norm_mod_store(h_ref, x_ref, g_ref, sh_ref, sc_ref)
    h = h_ref[...]
    tabs = (ra_ref[...], rb_ref[...], rc_ref[...])
    d = q_ref.shape[1]
    tn = 2 * kvd
    for n in range(d // tn):
        r = _dot(h, w_ref[:, n * tn:(n + 1) * tn])
        q_ref[:, n * tn:(n + 1) * tn] = (_rope(r, *tabs) * q_scale).astype(BF16)
    r = _dot(h, w_ref[:, d:])
    k = _rope(r[:, :kvd], *tabs)
    v = r[:, kvd:]
    k_ref[...] = k
    v_ref[...] = v
    kk_ref[...] = _dup_halves(k, hd).astype(BF16)
    vv_ref[...] = _dup_halves(v, hd).astype(BF16)


def _swa_in(x, g, shift, scale, w, tabs, *, layer, tm, tpb, kvd, hd):
    m, d = x.shape
    ntab = tabs[0].shape[0] // tm
    kkw = (kvd // hd) * 2 * LANES
    tab_spec = pl.BlockSpec((tm, LANES), lambda i, n: (i % ntab, 0))
    const = lambda w_: pl.BlockSpec((tm, w_), lambda i, n: (i, 0))
    return pl.pallas_call(
        functools.partial(_swa_in_kernel, kvd=kvd, hd=hd, q_scale=hd ** -0.5 * LOG2E),
        grid=(m // tm, 1),
        in_specs=[pl.BlockSpec((tm, d), lambda i, n: (i, 0)),
                  pl.BlockSpec((1, d), lambda i, n: (0, 0)),
                  _mod_spec(shift, tpb), _mod_spec(scale, tpb),
                  pl.BlockSpec((None, d, d + 2 * kvd), lambda i, n: (layer, 0, 0)),
                  tab_spec, tab_spec, tab_spec],
        out_specs=[const(d), const(kkw), const(kkw), const(kvd), const(kvd)],
        out_shape=[jax.ShapeDtypeStruct((m, d), BF16),
                   jax.ShapeDtypeStruct((m, kkw), BF16), jax.ShapeDtypeStruct((m, kkw), BF16),
                   jax.ShapeDtypeStruct((m, kvd), F32), jax.ShapeDtypeStruct((m, kvd), F32)],
        scratch_shapes=[pltpu.VMEM((tm, d), BF16)],
        compiler_params=_cp(2), name="swa_in_proj",
    )(x, g, shift, scale, w, *tabs)


def _rope_tables(pos, hd, rope_dims):
    half = rope_dims // 2
    inv_freq = ROPE_THETA ** (-jnp.arange(half, dtype=F32) * 2.0 / rope_dims)
    ang = pos.astype(F32)[:, None] * inv_freq[None, :]
    cos, sin = jnp.cos(ang), jnp.sin(ang)
    lane = np.arange(LANES) % hd
    idx = lane % half
    a = jnp.where(lane < rope_dims, cos[:, idx], 1.0)
    b = jnp.where(lane < half, -sin[:, idx], 0.0)
    c = jnp.where((lane >= half) & (lane < rope_dims), sin[:, idx], 0.0)
    return a, b, c


def _swa_core(q, k_top, k_bot, v_top, v_bot, bias, sinks, *, group):
    t = q.shape[0]
    npair = group // 2
    win = k_top.shape[0]
    half_lanes = lax.broadcasted_iota(jnp.int32, (win, LANES), 1) < LANES // 2
    count_even = jnp.where(half_lanes, 1.0, 0.0).astype(BF16)
    count_odd = jnp.where(half_lanes, 0.0, 1.0).astype(BF16)
    v2 = jnp.concatenate([jnp.concatenate([v_top, count_even], axis=1),
                          jnp.concatenate([v_bot, count_odd], axis=1)], axis=0)
    qs = jnp.concatenate([q[:, p * LANES:(p + 1) * LANES] for p in range(npair)], axis=0)
    s = _dot_nt(qs, jnp.concatenate([k_top, k_bot], axis=0)) + bias
    ps, es = [], []
    for half in range(2):
        cols = [s[:, half * win + c * LANES: half * win + (c + 1) * LANES] for c in range(win // LANES)]
        sink = sinks[:, half * LANES:(half + 1) * LANES]
        m = jnp.maximum(jnp.max(functools.reduce(jnp.maximum, cols), axis=-1, keepdims=True), sink)
        ps += [jnp.exp2(col - m) for col in cols]
        es.append(jnp.exp2(sink - m))
    o = _dot(jnp.concatenate(ps, axis=1).astype(BF16), v2)
    lane = lax.broadcasted_iota(jnp.int32, (o.shape[0], LANES), 1)
    o = o[:, :LANES] / (o[:, LANES:] + jnp.where(lane < LANES // 2, es[0], es[1]))
    return jnp.concatenate([o[p * t:(p + 1) * t] for p in range(npair)], axis=1)


def _swa_attn_kernel(q_ref, kh_ref, km_ref, vh_ref, vm_ref, bias_ref, sink_ref, o_ref, *, kv, group, hd):
    gw = group * hd
    bias = bias_ref[...]
    for g in range(kv):
        def win(halo, main, off):
            sl = slice(g * 2 * LANES + off, g * 2 * LANES + off + LANES)
            return jnp.concatenate([halo[:, sl], main[:, sl]], axis=0)
        o = _swa_core(q_ref[:, g * gw:(g + 1) * gw],
                      win(kh_ref, km_ref, 0), win(kh_ref, km_ref, LANES),
                      win(vh_ref, vm_ref, 0), win(vh_ref, vm_ref, LANES),
                      bias, sink_ref[g], group=group)
        o_ref[:, g * gw:(g + 1) * gw] = o.astype(o_ref.dtype)


def _swa_attn(q, kk, vv, bias2, sink_rows, *, nb, s, kv, group, hd):
    t = 2 * CHUNK
    nt = s // t
    d = q.shape[1]
    kkw = kk.shape[1]
    main = pl.BlockSpec((t, kkw), lambda b, i: (b * nt + i, 0))
    halo = pl.BlockSpec((t, kkw), lambda b, i: (b * nt + jnp.maximum(i - 1, 0), 0))
    return pl.pallas_call(
        functools.partial(_swa_attn_kernel, kv=kv, group=group, hd=hd),
        grid=(nb, nt),
        in_specs=[pl.BlockSpec((t, d), lambda b, i: (b * nt + i, 0)),
                  halo, main, halo, main,
                  pl.BlockSpec((None,) + bias2.shape[1:], lambda b, i: (jnp.minimum(i, 1), 0, 0)),
                  pl.BlockSpec(sink_rows.shape, lambda b, i: (0, 0, 0))],
        out_specs=pl.BlockSpec((t, d), lambda b, i: (b * nt + i, 0)),
        out_shape=jax.ShapeDtypeStruct(q.shape, BF16),
        compiler_params=_cp(2), name="swa_attention",
    )(q, kk, kk, vv, vv, bias2, sink_rows)


def _swa_sample_kernel(q_ref, kc_ref, vc_ref, kkn_ref, vvn_ref, bias_ref, sink_ref, o_ref, *, kv, group, hd, t):
    gw = group * hd
    kkc = _dup_halves(kc_ref[...], hd).astype(BF16)
    vvc = _dup_halves(vc_ref[...], hd).astype(BF16)
    pad = jnp.zeros((kc_ref.shape[0] - t, LANES), BF16)
    for g in range(kv):
        def win(cache, new, off):
            sl = slice(g * 2 * LANES + off, g * 2 * LANES + off + LANES)
            return jnp.concatenate([cache[:, sl], new[:, sl], pad], axis=0)
        o = _swa_core(q_ref[:, g * gw:(g + 1) * gw],
                      win(kkc, kkn_ref, 0), win(kkc, kkn_ref, LANES),
                      win(vvc, vvn_ref, 0), win(vvc, vvn_ref, LANES),
                      bias_ref[...], sink_ref[g], group=group)
        o_ref[:, g * gw:(g + 1) * gw] = o.astype(o_ref.dtype)


def _swa_sample_attn(q, k_cache, v_cache, kk_new, vv_new, bias, sink_rows, *, kv, group, hd, t):
    nb, buf, kvd = k_cache.shape
    d = q.shape[1]
    full = lambda a: pl.BlockSpec(a.shape, lambda b: (0,) * a.ndim)
    cache_spec = pl.BlockSpec((None, buf, kvd), lambda b: (b, 0, 0))
    new_spec = pl.BlockSpec((t, kk_new.shape[1]), lambda b: (b, 0))
    return pl.pallas_call(
        functools.partial(_swa_sample_kernel, kv=kv, group=group, hd=hd, t=t),
        grid=(nb,),
        in_specs=[pl.BlockSpec((t, d), lambda b: (b, 0)), cache_spec, cache_spec, new_spec, new_spec,
                  full(bias), full(sink_rows)],
        out_specs=pl.BlockSpec((t, d), lambda b: (b, 0)),
        out_shape=jax.ShapeDtypeStruct(q.shape, BF16),
        compiler_params=_cp(1), name="swa_sample_attention",
    )(q, k_cache, v_cache, kk_new, vv_new, bias, sink_rows)


def _window_bias(valid, npair):
    b = np.where(valid, 0.0, NEG).astype(np.float32)
    return np.tile(b, (npair, 2))


def _prompt_bias(npair):
    t = 2 * CHUNK
    qc = np.arange(t)[:, None] // CHUNK
    kc = np.arange(t + SWA_WINDOW_CHUNKS * CHUNK)[None, :] // CHUNK
    valid = (kc >= qc) & (kc <= qc + SWA_WINDOW_CHUNKS)
    first = valid & (kc >= SWA_WINDOW_CHUNKS)
    return jnp.asarray(np.stack([_window_bias(first, npair), _window_bias(valid, npair)]))


def _sample_bias(past_len, buf, t, npair):
    q_pos = past_len + np.arange(t)
    k_pos = np.concatenate([past_len - buf + np.arange(buf), q_pos])
    qch, kch = q_pos // CHUNK, k_pos // CHUNK
    valid = np.zeros((t, 2 * buf), bool)
    valid[:, :buf + t] = (kch[None, :] <= qch[:, None]) & (kch[None, :] >= qch[:, None] - SWA_WINDOW_CHUNKS)
    return jnp.asarray(_window_bias(valid, npair))


def _sink_rows(sinks, kv, group, t):
    s = (sinks * LOG2E).reshape(kv, group // 2, 1, 2, 1)
    return jnp.broadcast_to(s, (kv, group // 2, t, 2, LANES)).reshape(kv, (group // 2) * t, 2 * LANES)


def kernel(x_prompt, x_sample, c_prompt, c_sample, cache_fox_k, cache_fox_v, cache_fox_logf, cache_swa_k,
           cache_swa_v, ada_w, ada_b, norm_mix_g, norm_ffn_g, fox_w_in, fox_b_f, fox_w_out, swa_w_in,
           swa_sinks, swa_w_out, ffn_w_up, ffn_w_down, final_g):
    bp, s, d = x_prompt.shape
    bs, t, _ = x_sample.shape
    depth = ada_w.shape[0]
    past_len = cache_fox_k.shape[2]
    nh_fox, hd_fox = cache_fox_k.shape[3], cache_fox_k.shape[4]
    buf, kv, hd_swa = cache_swa_k.shape[2], cache_swa_k.shape[3], cache_swa_k.shape[4]
    nh_swa = swa_sinks.shape[1]
    group = nh_swa // kv
    kvd = kv * hd_swa
    rope_dims = hd_swa // 4
    assert hd_fox == LANES and hd_swa == LANES // 2 and rope_dims == 16 and group % 2 == 0
    assert buf == SWA_WINDOW_CHUNKS * CHUNK and t <= buf and s % (2 * CHUNK) == 0

    mp, ms = bp * s, bs * t
    tm_p = min(512, s)
    tpb_p = s // tm_p
    tm_fox = min(1024, s)
    tk = min(FOX_TK, s)
    tq = min(FOX_NSUB * tk, s)

    mods = _ada(jnp.concatenate([c_prompt, c_sample], axis=0), ada_w, ada_b)

    def split_mods(i):
        six = jnp.split(mods[i], 6, axis=-1)
        prompt = [m[:bp, None, :] for m in six]
        sample = [jnp.repeat(m[bp:], t, axis=0)[None] for m in six]
        return prompt, sample

    row = lambda v: v.reshape(1, -1)
    xp = x_prompt.reshape(mp, d)
    xs = x_sample.reshape(ms, d)
    pos_p = jnp.arange(s)
    pos_s = past_len + jnp.arange(t)
    tabs_p = _rope_tables(pos_p, hd_swa, rope_dims)
    tabs_s = tuple(jnp.tile(a, (bs, 1)) for a in _rope_tables(pos_s, hd_swa, rope_dims))
    bias_p = _prompt_bias(group // 2)
    bias_s = _sample_bias(past_len, buf, t, group // 2)

    fox_w_in_b, fox_w_out_b = _to_bf16(fox_w_in), _to_bf16(fox_w_out)
    swa_w_in_b, swa_w_out_b = _to_bf16(swa_w_in), _to_bf16(swa_w_out)
    w_up_b, w_down_b = _to_bf16(ffn_w_up), _to_bf16(ffn_w_down)

    outs = {k: [] for k in ("fkp", "fvp", "flp", "fks", "fvs", "fls", "skp", "svp", "sks", "svs")}
    for i in range(depth):
        mod_p, mod_s = split_mods(i)
        j = i // 2
        if i % 2 == 0:
            w_f = jnp.pad(fox_w_in[j, :, 3 * d:], ((0, 0), (0, LANES - nh_fox))).astype(BF16)
            b_f = row(fox_b_f[j])
            qh, kh, vh, k32, v32, lf, n2 = _fox_in(
                xp, row(norm_mix_g[i]), mod_p[0], mod_p[1], fox_w_in_b, w_f, b_f,
                layer=j, tm=tm_fox, tpb=s // tm_fox, head_major=True, nh=nh_fox, hd=hd_fox)
            lft = lf.reshape(bp, s, nh_fox).transpose(0, 2, 1)
            c = _cumsum_rows(lft.reshape(-1, LANES), s // LANES).reshape(bp * nh_fox, s)
            qmax, kmax = _head_norm_bounds(n2, bp, nh_fox, min(FOX_IN_COLS, d) // hd_fox)
            first = _fox_first_blocks(c, qmax, kmax, tk)
            op = _fox_attn(qh, kh, vh, c, first, tk=tk, nsub=tq // tk)
            xp = _proj_res(op.reshape(mp, d), fox_w_out_b, xp, mod_p[2], layer=j, tm=tm_p, tpb=tpb_p)
            outs["fkp"].append(k32.reshape(bp, s, nh_fox, hd_fox))
            outs["fvp"].append(v32.reshape(bp, s, nh_fox, hd_fox))
            outs["flp"].append(lf.reshape(bp, s, nh_fox))
            qs_, k32s, v32s, lfs = _fox_in(xs, row(norm_mix_g[i]), mod_s[0], mod_s[1], fox_w_in_b, w_f, b_f,
                                           layer=j, tm=ms, tpb=1, head_major=False, nh=nh_fox, hd=hd_fox)
            lf_all = jnp.concatenate([cache_fox_logf[j], lfs.reshape(bs, t, nh_fox)], axis=1)
            lc = LANES * int(2 ** np.ceil(np.log2(-(-(past_len + t) // LANES))))
            lf_all = jnp.pad(lf_all.transpose(0, 2, 1), ((0, 0), (0, 0), (0, lc - past_len - t)))
            c_s = _cumsum_rows(lf_all.reshape(-1, LANES), lc // LANES).reshape(bs, nh_fox, lc)
            os_ = _fox_sample_attn(qs_, cache_fox_k, cache_fox_v, k32s, v32s, c_s, layer=j, t=t)
            xs = _proj_res(os_, fox_w_out_b, xs, mod_s[2], layer=j, tm=ms, tpb=1)
            outs["fks"].append(k32s.reshape(bs, t, nh_fox, hd_fox))
            outs["fvs"].append(v32s.reshape(bs, t, nh_fox, hd_fox))
            outs["fls"].append(lfs.reshape(bs, t, nh_fox))
        else:
            q, kk, vv, k32, v32 = _swa_in(xp, row(norm_mix_g[i]), mod_p[0], mod_p[1], swa_w_in_b, tabs_p,
                                          layer=j, tm=tm_p, tpb=tpb_p, kvd=kvd, hd=hd_swa)
            op = _swa_attn(q, kk, vv, bias_p, _sink_rows(swa_sinks[j], kv, group, 2 * CHUNK),
                           nb=bp, s=s, kv=kv, group=group, hd=hd_swa)
            xp = _proj_res(op, swa_w_out_b, xp, mod_p[2], layer=j, tm=tm_p, tpb=tpb_p)
            outs["skp"].append(k32.reshape(bp, s, kvd)[:, -buf:].reshape(bp, buf, kv, hd_swa))
            outs["svp"].append(v32.reshape(bp, s, kvd)[:, -buf:].reshape(bp, buf, kv, hd_swa))
            q, kk, vv, k32, v32 = _swa_in(xs, row(norm_mix_g[i]), mod_s[0], mod_s[1], swa_w_in_b, tabs_s,
                                          layer=j, tm=ms, tpb=1, kvd=kvd, hd=hd_swa)
            os_ = _swa_sample_attn(q, cache_swa_k[j].reshape(bs, buf, kvd), cache_swa_v[j].reshape(bs, buf, kvd),
                                   kk, vv, bias_s, _sink_rows(swa_sinks[j], kv, group, t),
                                   kv=kv, group=group, hd=hd_swa, t=t)
            xs = _proj_res(os_, swa_w_out_b, xs, mod_s[2], layer=j, tm=ms, tpb=1)
            k_all = jnp.concatenate([cache_swa_k[j], k32.reshape(bs, t, kv, hd_swa)], axis=1)
            v_all = jnp.concatenate([cache_swa_v[j], v32.reshape(bs, t, kv, hd_swa)], axis=1)
            outs["sks"].append(k_all[:, -buf:])
            outs["svs"].append(v_all[:, -buf:])
        fg = row(final_g) if i == depth - 1 else None
        xp = _ffn(xp, row(norm_ffn_g[i]), mod_p[3], mod_p[4], mod_p[5], w_up_b, w_down_b, fg,
                  layer=i, tm=tm_p, tpb=tpb_p)
        xs = _ffn(xs, row(norm_ffn_g[i]), mod_s[3], mod_s[4], mod_s[5], w_up_b, w_down_b, fg,
                  layer=i, tm=ms, tpb=1)

    st = lambda k: jnp.stack(outs[k])
    return (xp.reshape(bp, s, d), xs.reshape(bs, t, d),
            st("fkp"), st("fvp"), st("flp"), st("fks"), st("fvs"), st("fls"),
            st("skp"), st("svp"), st("sks"), st("svs"))
```

```python
import functools

import numpy as np
import jax
import jax.numpy as jnp
from jax import lax
from jax.experimental import pallas as pl
from jax.experimental.pallas import tpu as pltpu

F32 = jnp.float32
BF16 = jnp.bfloat16

RMS_EPS = 1e-6
CHUNK = 64
SWA_WINDOW_CHUNKS = 2
ROPE_THETA = 500000.0
LANES = 128
NEG = -1e30
NEG_BLOCK = -3e38
VMEM_LIMIT_BYTES = 56 * 1024 * 1024
LOG2E = 1.4426950408889634
FOX_SKIP_LOG2 = 150.0
FFN_STEP_ELEMS = 512 * 1024
FOX_IN_COLS = 512
FOX_TK = 512
FOX_NSUB = 8


def _cp(n_axes):
    return pltpu.CompilerParams(dimension_semantics=("arbitrary",) * n_axes,
                                vmem_limit_bytes=VMEM_LIMIT_BYTES)


def _dot(a, b):
    return jnp.dot(a, b, preferred_element_type=F32)


def _dot_nt(a, b):
    return lax.dot_general(a, b, (((1,), (1,)), ((), ())), preferred_element_type=F32)


def _rms(x):
    return x * lax.rsqrt(jnp.mean(x * x, axis=-1, keepdims=True) + RMS_EPS)


NORM_ROWS = 16


def _norm_mod_store(h_ref, x_ref, g_ref, sh_ref, sc_ref):
    per_token = sh_ref.shape[0] > 1
    g = g_ref[...]
    if not per_token:
        gain, shift = g * (1.0 + sc_ref[...]), sh_ref[...]

    def body(i, carry):
        rows = pl.ds(pl.multiple_of(i * NORM_ROWS, NORM_ROWS), NORM_ROWS)
        x = x_ref[rows, :]
        if per_token:
            gain_, shift_ = g * (1.0 + sc_ref[rows, :]), sh_ref[rows, :]
        else:
            gain_, shift_ = gain, shift
        h_ref[rows, :] = (_rms(x) * gain_ + shift_).astype(h_ref.dtype)
        return carry

    lax.fori_loop(0, x_ref.shape[0] // NORM_ROWS, body, 0, unroll=4)


def _log_sigmoid(z):
    return jnp.minimum(z, 0.0) - jnp.log1p(jnp.exp(-jnp.abs(z)))


CAST_BLOCK_BYTES = 8 * 1024 * 1024


def _cast_kernel(*refs):
    n = len(refs) // 2
    for x_ref, o_ref in zip(refs[:n], refs[n:]):
        o_ref[...] = x_ref[...].astype(o_ref.dtype)


def _to_bf16(*ws):
    steps = 1
    while any(w.size * 4 // steps > CAST_BLOCK_BYTES for w in ws):
        steps *= 2
    specs = []
    for w in ws:
        layers, rows, cols = w.shape
        per_layer = steps // layers
        assert per_layer >= 1 and rows % (16 * per_layer) == 0
        specs.append(pl.BlockSpec((None, rows // per_layer, cols),
                                  lambda i, per_layer=per_layer: (i // per_layer, i % per_layer, 0)))
    return pl.pallas_call(
        _cast_kernel, grid=(steps,), in_specs=specs, out_specs=specs,
        out_shape=[jax.ShapeDtypeStruct(w.shape, BF16) for w in ws],
        compiler_params=_cp(1), name="weight_to_bf16",
    )(*ws)


def _ada_kernel(c_ref, w_ref, b_ref, o_ref):
    c = c_ref[...]
    a = (c / (1.0 + jnp.exp(-c))).astype(BF16)
    o_ref[...] = _dot(a, w_ref[...].astype(BF16)) + b_ref[...]


def _ada(c_all, ada_w, ada_b):
    depth, d, n6 = ada_w.shape
    r = c_all.shape[0]
    tn = min(1024, n6)
    return pl.pallas_call(
        _ada_kernel,
        grid=(depth, n6 // tn),
        in_specs=[pl.BlockSpec((r, d), lambda l, n: (0, 0)),
                  pl.BlockSpec((None, d, tn), lambda l, n: (l, 0, n)),
                  pl.BlockSpec((None, 1, tn), lambda l, n: (l, 0, n))],
        out_specs=pl.BlockSpec((None, r, tn), lambda l, n: (l, 0, n)),
        out_shape=jax.ShapeDtypeStruct((depth, r, n6), F32),
        compiler_params=_cp(2), name="ada_params",
    )(c_all, ada_w, ada_b.reshape(depth, 1, n6))


def _fox_in_kernel(x_ref, g_ref, sh_ref, sc_ref, wq_ref, wk_ref, wv_ref, wf_ref, bf_ref, *rest,
                   head_major, hd, nh, q_scale):
    if head_major:
        qb_ref, kb_ref, vb_ref, k_ref, v_ref, lf_ref, n2_ref, h_ref = rest
    else:
        qb_ref, k_ref, v_ref, lf_ref, h_ref = rest

    @pl.when(pl.program_id(1) == 0)
    def _():
        _norm_mod_store(h_ref, x_ref, g_ref, sh_ref, sc_ref)
        z = _dot(h_ref[...], wf_ref[...])[:, :nh] + bf_ref[...]
        lf_ref[...] = _log_sigmoid(z)

    hb = h_ref[...]
    q = _dot(hb, wq_ref[...]) * q_scale
    k = _dot(hb, wk_ref[...])
    v = _dot(hb, wv_ref[...])
    k_ref[...] = k
    v_ref[...] = v
    if head_major:
        for j in range(q.shape[1] // hd):
            sl = slice(j * hd, (j + 1) * hd)
            qb_ref[j] = q[:, sl].astype(BF16)
            kb_ref[j] = k[:, sl].astype(BF16)
            vb_ref[j] = v[:, sl].astype(BF16)
        tm, tn = q.shape
        grp = lax.shift_right_logical(lax.broadcasted_iota(jnp.int32, (2 * tn, LANES), 0), hd.bit_length() - 1)
        sel = jnp.where(grp == lax.broadcasted_iota(jnp.int32, (2 * tn, LANES), 1), 1.0, 0.0).astype(BF16)
        n2 = _dot(jnp.concatenate([q * q, k * k], axis=1).astype(BF16), sel)
        n2_ref[...] = jnp.max(n2.reshape(tm // 8, 8, LANES), axis=0)
    else:
        qb_ref[...] = q.astype(BF16)


def _mod_spec(mod, tpb, cols=None):
    _, rows, d = mod.shape
    if cols is None:
        return pl.BlockSpec((None, rows, d), lambda i, n: (i // tpb, 0, 0))
    return pl.BlockSpec((None, rows, cols), lambda i, n: (i // tpb, 0, n))


def _fox_in(x, g, shift, scale, w_in, w_f, b_f, *, layer, tm, tpb, head_major, nh, hd):
    m, d = x.shape
    tn = min(FOX_IN_COLS, d)
    nt = d // tn
    hpt = tn // hd
    nb, s = m // (tm * tpb), tm * tpb
    in_specs = [pl.BlockSpec((tm, d), lambda i, n: (i, 0)),
                pl.BlockSpec((1, d), lambda i, n: (0, 0)),
                _mod_spec(shift, tpb), _mod_spec(scale, tpb),
                pl.BlockSpec((None, d, tn), lambda i, n: (layer, 0, n)),
                pl.BlockSpec((None, d, tn), lambda i, n: (layer, 0, n + nt)),
                pl.BlockSpec((None, d, tn), lambda i, n: (layer, 0, n + 2 * nt)),
                pl.BlockSpec((d, LANES), lambda i, n: (0, 0)),
                pl.BlockSpec((1, nh), lambda i, n: (0, 0))]
    rm_spec = pl.BlockSpec((tm, tn), lambda i, n: (i, n))
    lf_spec = pl.BlockSpec((tm, nh), lambda i, n: (i, 0))
    if head_major:
        hm_spec = pl.BlockSpec((None, hpt, tm, hd), lambda i, n: (i // tpb, n, i % tpb, 0))
        hm_shape = jax.ShapeDtypeStruct((nb, nh, s, hd), BF16)
        assert hd & (hd - 1) == 0 and 2 * hpt <= LANES
        n2_spec = pl.BlockSpec((8, LANES), lambda i, n: (i, n))
        out_specs = [hm_spec, hm_spec, hm_spec, rm_spec, rm_spec, lf_spec, n2_spec]
        out_shape = [hm_shape, hm_shape, hm_shape]
    else:
        out_specs = [rm_spec, rm_spec, rm_spec, lf_spec]
        out_shape = [jax.ShapeDtypeStruct((m, d), BF16)]
    out_shape += [jax.ShapeDtypeStruct((m, d), F32), jax.ShapeDtypeStruct((m, d), F32),
                  jax.ShapeDtypeStruct((m, nh), F32)]
    if head_major:
        out_shape += [jax.ShapeDtypeStruct((m // tm * 8, nt * LANES), F32)]
    return pl.pallas_call(
        functools.partial(_fox_in_kernel, head_major=head_major, hd=hd, nh=nh, q_scale=hd ** -0.5 * LOG2E),
        grid=(m // tm, nt), in_specs=in_specs, out_specs=out_specs, out_shape=out_shape,
        scratch_shapes=[pltpu.VMEM((tm, d), BF16)],
        compiler_params=_cp(2), name="fox_in_proj",
    )(x, g, shift, scale, w_in, w_in, w_in, w_f, b_f)


def _split3(x):
    x1 = x.astype(BF16)
    r1 = x - x1.astype(F32)
    x2 = r1.astype(BF16)
    x3 = (r1 - x2.astype(F32)).astype(BF16)
    return x1, x2, x3


def _cumsum_kernel(x_ref, u_ref, l_ref, o_ref):
    u = u_ref[...]
    lo = l_ref[...]
    y = sum(_dot(p, u) for p in _split3(x_ref[...]))
    tot = jnp.broadcast_to(y[:, LANES - 1:LANES], y.shape)
    o_ref[...] = (y + sum(_dot(lo, p) for p in _split3(tot))) * LOG2E


def _cumsum_rows(x, nc):
    rows = x.shape[0]
    assert LANES % nc == 0 or nc % LANES == 0
    n = max(nc, LANES)
    assert rows % n == 0
    u = jnp.asarray(np.triu(np.ones((LANES, LANES), np.float32)), BF16)
    idx = np.arange(n)
    same = (idx[:, None] // nc) == (idx[None, :] // nc)
    lo = jnp.asarray((same & (idx[None, :] < idx[:, None])).astype(np.float32), BF16)
    return pl.pallas_call(
        _cumsum_kernel, grid=(rows // n,),
        in_specs=[pl.BlockSpec((n, LANES), lambda i: (i, 0)),
                  pl.BlockSpec((LANES, LANES), lambda i: (0, 0)),
                  pl.BlockSpec((n, n), lambda i: (0, 0))],
        out_specs=pl.BlockSpec((n, LANES), lambda i: (i, 0)),
        out_shape=jax.ShapeDtypeStruct(x.shape, F32),
        compiler_params=_cp(1), name="logf_cumsum",
    )(x, u, lo)


NORM_SLACK = 1.01


def _head_norm_bounds(n2, nb, nh, hpt):
    n2 = jnp.max(n2.reshape(nb, -1, nh // hpt, LANES), axis=1)
    bound = lambda part: (NORM_SLACK * jnp.sqrt(part)).reshape(-1)
    return bound(n2[:, :, :hpt]), bound(n2[:, :, hpt:2 * hpt])


def _fox_first_blocks(c, qmax, kmax, tk):
    bh, s = c.shape
    cb = c.reshape(bh, s // tk, tk)
    c_start, c_end = cb[:, :, 0], cb[:, :, -1]
    budget = (2.0 * qmax * kmax + FOX_SKIP_LOG2)[:, None, None]
    skippable = (c_end[:, None, :] - c_start[:, :, None]) > budget
    earlier = np.tril(np.ones((s // tk, s // tk), bool), -1)
    return jnp.sum(skippable & earlier, axis=-1).astype(jnp.int32).reshape(-1)


def _fox_attn_kernel(first_ref, q_ref, k_ref, v_ref, c_ref, o_ref, m_ref, acc_ref, *, tk, nsub, hd):
    b, h, qi = pl.program_id(0), pl.program_id(1), pl.program_id(2)
    nkb = pl.num_programs(2) * nsub
    base = qi * nsub
    m_ref[...] = jnp.full(m_ref.shape, NEG, F32)
    acc_ref[...] = jnp.zeros(acc_ref.shape, F32)
    ones = jnp.ones((tk, hd), BF16)
    ncol = tk // LANES
    c_first = [c_ref[base + r][:, 0:1] for r in range(nsub)]

    def step(r, j, mask=None, valid=None):
        rows = pl.ds(pl.multiple_of(j * tk, tk), tk)
        c_row = c_ref[j]
        if valid is not None:
            c_row = jnp.where(valid, c_row, -NEG_BLOCK)
        s = _dot_nt(q_ref[r * tk:(r + 1) * tk, :], k_ref[rows, :]) + (c_first[r] - c_row)
        if mask is not None:
            s = jnp.where(mask, s, NEG)
        cols = [s[:, c * LANES:(c + 1) * LANES] for c in range(ncol)]
        m_cur = functools.reduce(jnp.maximum, cols)
        m_prev = m_ref[r]
        m_next = jnp.maximum(m_prev, jnp.max(m_cur, axis=-1, keepdims=True))
        alpha = jnp.exp2(m_prev - m_next)
        p = jnp.concatenate([jnp.exp2(col - m_next) for col in cols], axis=1).astype(BF16)
        pv = _dot(p, jnp.concatenate([v_ref[rows, :], ones], axis=1))
        acc_ref[r] = jnp.concatenate([alpha, alpha], axis=1) * acc_ref[r] + pv
        m_ref[r] = m_next

    off = (b * pl.num_programs(1) + h) * nkb + base
    n = functools.reduce(jnp.maximum, [base + r - first_ref[off + r] for r in range(nsub)])

    def body(t, carry):
        for r in range(nsub):
            j = base + r - n + t
            step(r, jnp.maximum(j, 0), valid=j >= 0)
        return carry

    lax.fori_loop(0, n, body, 0)
    row = lax.broadcasted_iota(jnp.int32, (tk, tk), 0)
    col = lax.broadcasted_iota(jnp.int32, (tk, tk), 1)
    causal = col <= row
    for r in range(nsub):
        step(r, base + r, mask=causal)
    for r in range(nsub):
        acc = acc_ref[r]
        o_ref[r * tk:(r + 1) * tk, :] = (acc[:, :hd] / acc[:, hd:]).astype(o_ref.dtype)


def _fox_attn(qh, kh, vh, c, first, *, tk, nsub):
    nb, nh, s, hd = qh.shape
    tq = tk * nsub
    nq = s // tq
    kv_spec = pl.BlockSpec((None, None, s, hd), lambda b, h, i, f: (b, h, 0, 0))
    return pl.pallas_call(
        functools.partial(_fox_attn_kernel, tk=tk, nsub=nsub, hd=hd),
        grid_spec=pltpu.PrefetchScalarGridSpec(
            num_scalar_prefetch=1, grid=(nb, nh, nq),
            in_specs=[pl.BlockSpec((None, None, tq, hd), lambda b, h, i, f: (b, h, i, 0)),
                      kv_spec, kv_spec,
                      pl.BlockSpec((None, s // tk, 1, tk), lambda b, h, i, f: (b * nh + h, 0, 0, 0))],
            out_specs=pl.BlockSpec((None, tq, hd), lambda b, h, i, f: (b, i, h)),
            scratch_shapes=[pltpu.VMEM((nsub, tk, LANES), F32), pltpu.VMEM((nsub, tk, 2 * hd), F32)]),
        out_shape=jax.ShapeDtypeStruct((nb, s, nh * hd), BF16),
        compiler_params=_cp(3), name="fox_attention",
    )(first, qh, kh, vh, c.reshape(nb * nh, s // tk, 1, tk))


def _fox_sample_kernel(q_ref, kc_ref, vc_ref, kn_ref, vn_ref, c_ref, cf_ref, o_ref, *, hb, hd, p_len, t):
    n = p_len * hb
    kf = kc_ref[...].reshape(n, hd).astype(BF16)
    vf = vc_ref[...].reshape(n, hd).astype(BF16)
    heads = [slice(j * hd, (j + 1) * hd) for j in range(hb)]
    q8 = jnp.concatenate([q_ref[:, sl] for sl in heads], axis=0)
    row_head = lax.shift_right_logical(lax.broadcasted_iota(jnp.int32, (hb * t, LANES), 0), t.bit_length() - 1)
    col_head = lax.broadcasted_iota(jnp.int32, (hb * t, LANES), 1) & (hb - 1)
    own = jnp.where(row_head == col_head, 0.0, NEG)
    s = _dot_nt(q8, kf)
    cf = cf_ref[...]
    cols = [s[:, g * LANES:(g + 1) * LANES] - cf[:, g * LANES:(g + 1) * LANES] + own for g in range(n // LANES)]
    tri = lax.broadcasted_iota(jnp.int32, (t, t), 1) <= lax.broadcasted_iota(jnp.int32, (t, t), 0)
    s_n = jnp.concatenate(
        [jnp.where(tri, _dot_nt(q_ref[:, sl], kn_ref[:, sl].astype(BF16)) - c_ref[j:j + 1, p_len:p_len + t], NEG)
         for j, sl in enumerate(heads)], axis=0)
    m = jnp.maximum(jnp.max(functools.reduce(jnp.maximum, cols), axis=-1, keepdims=True),
                    jnp.max(s_n, axis=-1, keepdims=True))
    ps = [jnp.exp2(col - m) for col in cols]
    p_n = jnp.exp2(s_n - m)
    den = jnp.sum(functools.reduce(jnp.add, ps), axis=-1, keepdims=True) + jnp.sum(p_n, axis=-1, keepdims=True)
    o = _dot(jnp.concatenate(ps, axis=1).astype(BF16), vf)
    o_n = jnp.concatenate([_dot(p_n[j * t:(j + 1) * t].astype(BF16), vn_ref[:, sl].astype(BF16))
                           for j, sl in enumerate(heads)], axis=0)
    o = (o + o_n) / den
    for j, sl in enumerate(heads):
        o_ref[:, sl] = o[j * t:(j + 1) * t].astype(o_ref.dtype)


def _fox_sample_attn(q, k_cache, v_cache, k_new, v_new, c_all, *, layer, t):
    _, nb, p_len, nh, hd = k_cache.shape
    hb = min(8, nh)
    assert t & (t - 1) == 0 and hb & (hb - 1) == 0 and (p_len * hb) % LANES == 0 and LANES % hb == 0
    w = hb * hd
    lc = c_all.shape[-1]
    c_flat = c_all[:, :, :p_len].reshape(nb, nh // hb, hb, p_len).transpose(0, 1, 3, 2)
    c_flat = c_flat.reshape(nb, nh // hb, 1, p_len * hb)
    cache_spec = pl.BlockSpec((None, None, p_len, hb, hd), lambda b, h: (layer, b, 0, h, 0))
    row_spec = pl.BlockSpec((t, w), lambda b, h: (b, h))
    return pl.pallas_call(
        functools.partial(_fox_sample_kernel, hb=hb, hd=hd, p_len=p_len, t=t),
        grid=(nb, nh // hb),
        in_specs=[row_spec, cache_spec, cache_spec, row_spec, row_spec,
                  pl.BlockSpec((None, hb, lc), lambda b, h: (b, h, 0)),
                  pl.BlockSpec((None, None, 1, p_len * hb), lambda b, h: (b, h, 0, 0))],
        out_specs=row_spec,
        out_shape=jax.ShapeDtypeStruct((nb * t, nh * hd), BF16),
        compiler_params=_cp(2), name="fox_sample_attention",
    )(q, k_cache, v_cache, k_new, v_new, c_all, c_flat)


def _proj_res_kernel(a_ref, w_ref, x_ref, gate_ref, o_ref):
    o_ref[...] = x_ref[...] + gate_ref[...] * _dot(a_ref[...], w_ref[...])


def _proj_res(a, w, x, gate, *, layer, tm, tpb):
    m, k = a.shape
    n = w.shape[2]
    return pl.pallas_call(
        _proj_res_kernel, grid=(m // tm, 1),
        in_specs=[pl.BlockSpec((tm, k), lambda i, j: (i, 0)),
                  pl.BlockSpec((None, k, n), lambda i, j: (layer, 0, 0)),
                  pl.BlockSpec((tm, n), lambda i, j: (i, 0)),
                  _mod_spec(gate, tpb)],
        out_specs=pl.BlockSpec((tm, n), lambda i, j: (i, 0)),
        out_shape=jax.ShapeDtypeStruct((m, n), F32),
        compiler_params=_cp(2), name="attn_out_proj",
    )(a, w, x, gate)


def _ffn_kernel(x_ref, g_ref, sh_ref, sc_ref, gate_ref, wu_ref, wd_ref, *rest, final):
    if final:
        fg_ref, o_ref, h_ref, acc_ref = rest
    else:
        o_ref, h_ref, acc_ref = rest
    f = pl.program_id(1)

    @pl.when(f == 0)
    def _():
        _norm_mod_store(h_ref, x_ref, g_ref, sh_ref, sc_ref)
        acc_ref[...] = jnp.zeros(acc_ref.shape, F32)

    a = jnp.maximum(_dot(h_ref[...], wu_ref[...]), 0.0)
    acc_ref[...] += _dot((a * a).astype(BF16), wd_ref[...])

    @pl.when(f == pl.num_programs(1) - 1)
    def _():
        y = x_ref[...] + gate_ref[...] * acc_ref[...]
        if final:
            y = _rms(y) * fg_ref[...]
        o_ref[...] = y


def _ffn(x, g, shift, scale, gate, w_up, w_down, final_g, *, layer, tm, tpb):
    m, d = x.shape
    ff = w_up.shape[2]
    tf = min(ff, FFN_STEP_ELEMS // tm)
    vec = pl.BlockSpec((1, d), lambda i, f: (0, 0))
    in_specs = [pl.BlockSpec((tm, d), lambda i, f: (i, 0)), vec,
                _mod_spec(shift, tpb), _mod_spec(scale, tpb), _mod_spec(gate, tpb),
                pl.BlockSpec((None, d, tf), lambda i, f: (layer, 0, f)),
                pl.BlockSpec((None, tf, d), lambda i, f: (layer, f, 0))]
    args = [x, g, shift, scale, gate, w_up, w_down]
    if final_g is not None:
        in_specs.append(vec)
        args.append(final_g)
    return pl.pallas_call(
        functools.partial(_ffn_kernel, final=final_g is not None),
        grid=(m // tm, ff // tf), in_specs=in_specs,
        out_specs=pl.BlockSpec((tm, d), lambda i, f: (i, 0)),
        out_shape=jax.ShapeDtypeStruct((m, d), F32),
        scratch_shapes=[pltpu.VMEM((tm, d), BF16), pltpu.VMEM((tm, d), F32)],
        compiler_params=_cp(2), name="ffn_final" if final_g is not None else "ffn",
    )(*args)


def _rope(r, a, b, c):
    half_shift = (LANES // 8) // 2
    cols = []
    for j in range(r.shape[1] // LANES):
        x = r[:, j * LANES:(j + 1) * LANES]
        cols.append(x * a + pltpu.roll(x, LANES - half_shift, 1) * b + pltpu.roll(x, half_shift, 1) * c)
    return jnp.concatenate(cols, axis=1)


def _dup_halves(x, hd):
    lo = lax.broadcasted_iota(jnp.int32, (x.shape[0], LANES), 1) < hd
    zero = jnp.zeros((x.shape[0], LANES), x.dtype)
    out = []
    for j in range(x.shape[1] // LANES):
        p = x[:, j * LANES:(j + 1) * LANES]
        r = pltpu.roll(p, hd, 1)
        out += [jnp.where(lo, p, zero), jnp.where(lo, zero, r), jnp.where(lo, r, zero), jnp.where(lo, zero, p)]
    return jnp.concatenate(out, axis=1)


def _swa_in_kernel(x_ref, g_ref, sh_ref, sc_ref, w_ref, ra_ref, rb_ref, rc_ref,
                   q_ref, kk_ref, vv_ref, k_ref, v_ref, h_ref, *, kvd, hd, q_scale):
    _norm_mod_store(h_ref, x_ref, g_ref, sh_ref, sc_ref)
    h = h_ref[...]
    tabs = (ra_ref[...], rb_ref[...], rc_ref[...])
    d = q_ref.shape[1]
    tn = 2 * kvd
    for n in range(d // tn):
        r = _dot(h, w_ref[:, n * tn:(n + 1) * tn])
        q_ref[:, n * tn:(n + 1) * tn] = (_rope(r, *tabs) * q_scale).astype(BF16)
    r = _dot(h, w_ref[:, d:])
    k = _rope(r[:, :kvd], *tabs)
    v = r[:, kvd:]
    k_ref[...] = k
    v_ref[...] = v
    kk_ref[...] = _dup_halves(k, hd).astype(BF16)
    vv_ref[...] = _dup_halves(v, hd).astype(BF16)


def _swa_in(x, g, shift, scale, w, tabs, *, layer, tm, tpb, kvd, hd):
    m, d = x.shape
    ntab = tabs[0].shape[0] // tm
    kkw = (kvd // hd) * 2 * LANES
    tab_spec = pl.BlockSpec((tm, LANES), lambda i, n: (i % ntab, 0))
    const = lambda w_: pl.BlockSpec((tm, w_), lambda i, n: (i, 0))
    return pl.pallas_call(
        functools.partial(_swa_in_kernel, kvd=kvd, hd=hd, q_scale=hd ** -0.5 * LOG2E),
        grid=(m // tm, 1),
        in_specs=[pl.BlockSpec((tm, d), lambda i, n: (i, 0)),
                  pl.BlockSpec((1, d), lambda i, n: (0, 0)),
                  _mod_spec(shift, tpb), _mod_spec(scale, tpb),
                  pl.BlockSpec((None, d, d + 2 * kvd), lambda i, n: (layer, 0, 0)),
                  tab_spec, tab_spec, tab_spec],
        out_specs=[const(d), const(kkw), const(kkw), const(kvd), const(kvd)],
        out_shape=[jax.ShapeDtypeStruct((m, d), BF16),
                   jax.ShapeDtypeStruct((m, kkw), BF16), jax.ShapeDtypeStruct((m, kkw), BF16),
                   jax.ShapeDtypeStruct((m, kvd), F32), jax.ShapeDtypeStruct((m, kvd), F32)],
        scratch_shapes=[pltpu.VMEM((tm, d), BF16)],
        compiler_params=_cp(2), name="swa_in_proj",
    )(x, g, shift, scale, w, *tabs)


def _rope_tables(pos, hd, rope_dims):
    half = rope_dims // 2
    inv_freq = ROPE_THETA ** (-jnp.arange(half, dtype=F32) * 2.0 / rope_dims)
    ang = pos.astype(F32)[:, None] * inv_freq[None, :]
    cos, sin = jnp.cos(ang), jnp.sin(ang)
    lane = np.arange(LANES) % hd
    idx = lane % half
    a = jnp.where(lane < rope_dims, cos[:, idx], 1.0)
    b = jnp.where(lane < half, -sin[:, idx], 0.0)
    c = jnp.where((lane >= half) & (lane < rope_dims), sin[:, idx], 0.0)
    return a, b, c


def _swa_core(q, k_top, k_bot, v_top, v_bot, bias, sinks, *, group):
    t = q.shape[0]
    npair = group // 2
    win = k_top.shape[0]
    half_lanes = lax.broadcasted_iota(jnp.int32, (win, LANES), 1) < LANES // 2
    count_even = jnp.where(half_lanes, 1.0, 0.0).astype(BF16)
    count_odd = jnp.where(half_lanes, 0.0, 1.0).astype(BF16)
    v2 = jnp.concatenate([jnp.concatenate([v_top, count_even], axis=1),
                          jnp.concatenate([v_bot, count_odd], axis=1)], axis=0)
    qs = jnp.concatenate([q[:, p * LANES:(p + 1) * LANES] for p in range(npair)], axis=0)
    s = _dot_nt(qs, jnp.concatenate([k_top, k_bot], axis=0)) + bias
    ps, es = [], []
    for half in range(2):
        cols = [s[:, half * win + c * LANES: half * win + (c + 1) * LANES] for c in range(win // LANES)]
        sink = sinks[:, half * LANES:(half + 1) * LANES]
        m = jnp.maximum(jnp.max(functools.reduce(jnp.maximum, cols), axis=-1, keepdims=True), sink)
        ps += [jnp.exp2(col - m) for col in cols]
        es.append(jnp.exp2(sink - m))
    o = _dot(jnp.concatenate(ps, axis=1).astype(BF16), v2)
    lane = lax.broadcasted_iota(jnp.int32, (o.shape[0], LANES), 1)
    o = o[:, :LANES] / (o[:, LANES:] + jnp.where(lane < LANES // 2, es[0], es[1]))
    return jnp.concatenate([o[p * t:(p + 1) * t] for p in range(npair)], axis=1)


def _swa_attn_kernel(q_ref, kh_ref, km_ref, vh_ref, vm_ref, bias_ref, sink_ref, o_ref, *, kv, group, hd):
    gw = group * hd
    bias = bias_ref[...]
    for g in range(kv):
        def win(halo, main, off):
            sl = slice(g * 2 * LANES + off, g * 2 * LANES + off + LANES)
            return jnp.concatenate([halo[:, sl], main[:, sl]], axis=0)
        o = _swa_core(q_ref[:, g * gw:(g + 1) * gw],
                      win(kh_ref, km_ref, 0), win(kh_ref, km_ref, LANES),
                      win(vh_ref, vm_ref, 0), win(vh_ref, vm_ref, LANES),
                      bias, sink_ref[g], group=group)
        o_ref[:, g * gw:(g + 1) * gw] = o.astype(o_ref.dtype)


def _swa_attn(q, kk, vv, bias2, sink_rows, *, nb, s, kv, group, hd):
    t = 2 * CHUNK
    nt = s // t
    d = q.shape[1]
    kkw = kk.shape[1]
    main = pl.BlockSpec((t, kkw), lambda b, i: (b * nt + i, 0))
    halo = pl.BlockSpec((t, kkw), lambda b, i: (b * nt + jnp.maximum(i - 1, 0), 0))
    return pl.pallas_call(
        functools.partial(_swa_attn_kernel, kv=kv, group=group, hd=hd),
        grid=(nb, nt),
        in_specs=[pl.BlockSpec((t, d), lambda b, i: (b * nt + i, 0)),
                  halo, main, halo, main,
                  pl.BlockSpec((None,) + bias2.shape[1:], lambda b, i: (jnp.minimum(i, 1), 0, 0)),
                  pl.BlockSpec(sink_rows.shape, lambda b, i: (0, 0, 0))],
        out_specs=pl.BlockSpec((t, d), lambda b, i: (b * nt + i, 0)),
        out_shape=jax.ShapeDtypeStruct(q.shape, BF16),
        compiler_params=_cp(2), name="swa_attention",
    )(q, kk, kk, vv, vv, bias2, sink_rows)


def _swa_sample_kernel(q_ref, kc_ref, vc_ref, kkn_ref, vvn_ref, bias_ref, sink_ref, o_ref, *, kv, group, hd, t):
    gw = group * hd
    kkc = _dup_halves(kc_ref[...], hd).astype(BF16)
    vvc = _dup_halves(vc_ref[...], hd).astype(BF16)
    pad = jnp.zeros((kc_ref.shape[0] - t, LANES), BF16)
    for g in range(kv):
        def win(cache, new, off):
            sl = slice(g * 2 * LANES + off, g * 2 * LANES + off + LANES)
            return jnp.concatenate([cache[:, sl], new[:, sl], pad], axis=0)
        o = _swa_core(q_ref[:, g * gw:(g + 1) * gw],
                      win(kkc, kkn_ref, 0), win(kkc, kkn_ref, LANES),
                      win(vvc, vvn_ref, 0), win(vvc, vvn_ref, LANES),
                      bias_ref[...], sink_ref[g], group=group)
        o_ref[:, g * gw:(g + 1) * gw] = o.astype(o_ref.dtype)


def _swa_sample_attn(q, k_cache, v_cache, kk_new, vv_new, bias, sink_rows, *, kv, group, hd, t):
    nb, buf, kvd = k_cache.shape
    d = q.shape[1]
    full = lambda a: pl.BlockSpec(a.shape, lambda b: (0,) * a.ndim)
    cache_spec = pl.BlockSpec((None, buf, kvd), lambda b: (b, 0, 0))
    new_spec = pl.BlockSpec((t, kk_new.shape[1]), lambda b: (b, 0))
    return pl.pallas_call(
        functools.partial(_swa_sample_kernel, kv=kv, group=group, hd=hd, t=t),
        grid=(nb,),
        in_specs=[pl.BlockSpec((t, d), lambda b: (b, 0)), cache_spec, cache_spec, new_spec, new_spec,
                  full(bias), full(sink_rows)],
        out_specs=pl.BlockSpec((t, d), lambda b: (b, 0)),
        out_shape=jax.ShapeDtypeStruct(q.shape, BF16),
        compiler_params=_cp(1), name="swa_sample_attention",
    )(q, k_cache, v_cache, kk_new, vv_new, bias, sink_rows)


def _window_bias(valid, npair):
    b = np.where(valid, 0.0, NEG).astype(np.float32)
    return np.tile(b, (npair, 2))


def _prompt_bias(npair):
    t = 2 * CHUNK
    qc = np.arange(t)[:, None] // CHUNK
    kc = np.arange(t + SWA_WINDOW_CHUNKS * CHUNK)[None, :] // CHUNK
    valid = (kc >= qc) & (kc <= qc + SWA_WINDOW_CHUNKS)
    first = valid & (kc >= SWA_WINDOW_CHUNKS)
    return jnp.asarray(np.stack([_window_bias(first, npair), _window_bias(valid, npair)]))


def _sample_bias(past_len, buf, t, npair):
    q_pos = past_len + np.arange(t)
    k_pos = np.concatenate([past_len - buf + np.arange(buf), q_pos])
    qch, kch = q_pos // CHUNK, k_pos // CHUNK
    valid = np.zeros((t, 2 * buf), bool)
    valid[:, :buf + t] = (kch[None, :] <= qch[:, None]) & (kch[None, :] >= qch[:, None] - SWA_WINDOW_CHUNKS)
    return jnp.asarray(_window_bias(valid, npair))


def _sink_rows(sinks, kv, group, t):
    s = (sinks * LOG2E).reshape(kv, group // 2, 1, 2, 1)
    return jnp.broadcast_to(s, (kv, group // 2, t, 2, LANES)).reshape(kv, (group // 2) * t, 2 * LANES)


def kernel(x_prompt, x_sample, c_prompt, c_sample, cache_fox_k, cache_fox_v, cache_fox_logf, cache_swa_k,
           cache_swa_v, ada_w, ada_b, norm_mix_g, norm_ffn_g, fox_w_in, fox_b_f, fox_w_out, swa_w_in,
           swa_sinks, swa_w_out, ffn_w_up, ffn_w_down, final_g):
    bp, s, d = x_prompt.shape
    bs, t, _ = x_sample.shape
    depth = ada_w.shape[0]
    past_len = cache_fox_k.shape[2]
    nh_fox, hd_fox = cache_fox_k.shape[3], cache_fox_k.shape[4]
    buf, kv, hd_swa = cache_swa_k.shape[2], cache_swa_k.shape[3], cache_swa_k.shape[4]
    nh_swa = swa_sinks.shape[1]
    group = nh_swa // kv
    kvd = kv * hd_swa
    rope_dims = hd_swa // 4
    assert hd_fox == LANES and hd_swa == LANES // 2 and rope_dims == 16 and group % 2 == 0
    assert buf == SWA_WINDOW_CHUNKS * CHUNK and t <= buf and s % (2 * CHUNK) == 0

    mp, ms = bp * s, bs * t
    tm_p = min(512, s)
    tpb_p = s // tm_p
    tm_fox = min(1024, s)
    tk = min(FOX_TK, s)
    tq = min(FOX_NSUB * tk, s)

    mods = _ada(jnp.concatenate([c_prompt, c_sample], axis=0), ada_w, ada_b)

    def split_mods(i):
        six = jnp.split(mods[i], 6, axis=-1)
        prompt = [m[:bp, None, :] for m in six]
        sample = [jnp.repeat(m[bp:], t, axis=0)[None] for m in six]
        return prompt, sample

    row = lambda v: v.reshape(1, -1)
    xp = x_prompt.reshape(mp, d)
    xs = x_sample.reshape(ms, d)
    pos_p = jnp.arange(s)
    pos_s = past_len + jnp.arange(t)
    tabs_p = _rope_tables(pos_p, hd_swa, rope_dims)
    tabs_s = tuple(jnp.tile(a, (bs, 1)) for a in _rope_tables(pos_s, hd_swa, rope_dims))
    bias_p = _prompt_bias(group // 2)
    bias_s = _sample_bias(past_len, buf, t, group // 2)

    fox_w_in_b, fox_w_out_b, swa_w_in_b, swa_w_out_b = _to_bf16(fox_w_in, fox_w_out, swa_w_in, swa_w_out)
    w_up_b, w_down_b = _to_bf16(ffn_w_up, ffn_w_down)

    outs = {k: [] for k in ("fkp", "fvp", "flp", "fks", "fvs", "fls", "skp", "svp", "sks", "svs")}
    for i in range(depth):
        mod_p, mod_s = split_mods(i)
        j = i // 2
        if i % 2 == 0:
            w_f = jnp.pad(fox_w_in_b[j, :, 3 * d:], ((0, 0), (0, LANES - nh_fox)))
            b_f = row(fox_b_f[j])
            qh, kh, vh, k32, v32, lf, n2 = _fox_in(
                xp, row(norm_mix_g[i]), mod_p[0], mod_p[1], fox_w_in_b, w_f, b_f,
                layer=j, tm=tm_fox, tpb=s // tm_fox, head_major=True, nh=nh_fox, hd=hd_fox)
            lft = lf.reshape(bp, s, nh_fox).transpose(0, 2, 1)
            c = _cumsum_rows(lft.reshape(-1, LANES), s // LANES).reshape(bp * nh_fox, s)
            qmax, kmax = _head_norm_bounds(n2, bp, nh_fox, min(FOX_IN_COLS, d) // hd_fox)
            first = _fox_first_blocks(c, qmax, kmax, tk)
            op = _fox_attn(qh, kh, vh, c, first, tk=tk, nsub=tq // tk)
            xp = _proj_res(op.reshape(mp, d), fox_w_out_b, xp, mod_p[2], layer=j, tm=tm_p, tpb=tpb_p)
            outs["fkp"].append(k32.reshape(bp, s, nh_fox, hd_fox))
            outs["fvp"].append(v32.reshape(bp, s, nh_fox, hd_fox))
            outs["flp"].append(lf.reshape(bp, s, nh_fox))
            qs_, k32s, v32s, lfs = _fox_in(xs, row(norm_mix_g[i]), mod_s[0], mod_s[1], fox_w_in_b, w_f, b_f,
                                           layer=j, tm=ms, tpb=1, head_major=False, nh=nh_fox, hd=hd_fox)
            lf_all = jnp.concatenate([cache_fox_logf[j], lfs.reshape(bs, t, nh_fox)], axis=1)
            lc = LANES * int(2 ** np.ceil(np.log2(-(-(past_len + t) // LANES))))
            lf_all = jnp.pad(lf_all.transpose(0, 2, 1), ((0, 0), (0, 0), (0, lc - past_len - t)))
            c_s = _cumsum_rows(lf_all.reshape(-1, LANES), lc // LANES).reshape(bs, nh_fox, lc)
            os_ = _fox_sample_attn(qs_, cache_fox_k, cache_fox_v, k32s, v32s, c_s, layer=j, t=t)
            xs = _proj_res(os_, fox_w_out_b, xs, mod_s[2], layer=j, tm=ms, tpb=1)
            outs["fks"].append(k32s.reshape(bs, t, nh_fox, hd_fox))
            outs["fvs"].append(v32s.reshape(bs, t, nh_fox, hd_fox))
            outs["fls"].append(lfs.reshape(bs, t, nh_fox))
        else:
            q, kk, vv, k32, v32 = _swa_in(xp, row(norm_mix_g[i]), mod_p[0], mod_p[1], swa_w_in_b, tabs_p,
                                          layer=j, tm=tm_p, tpb=tpb_p, kvd=kvd, hd=hd_swa)
            op = _swa_attn(q, kk, vv, bias_p, _sink_rows(swa_sinks[j], kv, group, 2 * CHUNK),
                           nb=bp, s=s, kv=kv, group=group, hd=hd_swa)
            xp = _proj_res(op, swa_w_out_b, xp, mod_p[2], layer=j, tm=tm_p, tpb=tpb_p)
            outs["skp"].append(k32.reshape(bp, s, kvd)[:, -buf:].reshape(bp, buf, kv, hd_swa))
            outs["svp"].append(v32.reshape(bp, s, kvd)[:, -buf:].reshape(bp, buf, kv, hd_swa))
            q, kk, vv, k32, v32 = _swa_in(xs, row(norm_mix_g[i]), mod_s[0], mod_s[1], swa_w_in_b, tabs_s,
                                          layer=j, tm=ms, tpb=1, kvd=kvd, hd=hd_swa)
            os_ = _swa_sample_attn(q, cache_swa_k[j].reshape(bs, buf, kvd), cache_swa_v[j].reshape(bs, buf, kvd),
                                   kk, vv, bias_s, _sink_rows(swa_sinks[j], kv, group, t),
                                   kv=kv, group=group, hd=hd_swa, t=t)
            xs = _proj_res(os_, swa_w_out_b, xs, mod_s[2], layer=j, tm=ms, tpb=1)
            k_all = jnp.concatenate([cache_swa_k[j], k32.reshape(bs, t, kv, hd_swa)], axis=1)
            v_all = jnp.concatenate([cache_swa_v[j], v32.reshape(bs, t, kv, hd_swa)], axis=1)
            outs["sks"].append(k_all[:, -buf:])
            outs["svs"].append(v_all[:, -buf:])
        fg = row(final_g) if i == depth - 1 else None
        xp = _ffn(xp, row(norm_ffn_g[i]), mod_p[3], mod_p[4], mod_p[5], w_up_b, w_down_b, fg,
                  layer=i, tm=tm_p, tpb=tpb_p)
        xs = _ffn(xs, row(norm_ffn_g[i]), mod_s[3], mod_s[4], mod_s[5], w_up_b, w_down_b, fg,
                  layer=i, tm=ms, tpb=1)

    st = lambda k: jnp.stack(outs[k])
    return (xp.reshape(bp, s, d), xs.reshape(bs, t, d),
            st("fkp"), st("fvp"), st("flp"), st("fks"), st("fvs"), st("fls"),
            st("skp"), st("svp"), st("sks"), st("svs"))
```

```python
import functools

import numpy as np
import jax
import jax.numpy as jnp
from jax import lax
from jax.experimental import pallas as pl
from jax.experimental.pallas import tpu as pltpu

F32 = jnp.float32
BF16 = jnp.bfloat16

RMS_EPS = 1e-6
CHUNK = 64
SWA_WINDOW_CHUNKS = 2
ROPE_THETA = 500000.0
LANES = 128
NEG = -1e30
NEG_BLOCK = -3e38
VMEM_LIMIT_BYTES = 56 * 1024 * 1024
LOG2E = 1.4426950408889634
FOX_SKIP_LOG2 = 150.0
FFN_STEP_ELEMS = 512 * 1024
FOX_IN_COLS = 512
FOX_TK = 512
FOX_NSUB = 16


def _cp(n_axes):
    return pltpu.CompilerParams(dimension_semantics=("arbitrary",) * n_axes,
                                vmem_limit_bytes=VMEM_LIMIT_BYTES)


def _dot(a, b):
    return jnp.dot(a, b, preferred_element_type=F32)


def _dot_nt(a, b):
    return lax.dot_general(a, b, (((1,), (1,)), ((), ())), preferred_element_type=F32)


def _rms(x):
    return x * lax.rsqrt(jnp.mean(x * x, axis=-1, keepdims=True) + RMS_EPS)


NORM_ROWS = 16


def _norm_mod_store(h_ref, x_ref, g_ref, sh_ref, sc_ref):
    per_token = sh_ref.shape[0] > 1
    g = g_ref[...]
    if not per_token:
        gain, shift = g * (1.0 + sc_ref[...]), sh_ref[...]

    def body(i, carry):
        rows = pl.ds(pl.multiple_of(i * NORM_ROWS, NORM_ROWS), NORM_ROWS)
        x = x_ref[rows, :]
        if per_token:
            gain_, shift_ = g * (1.0 + sc_ref[rows, :]), sh_ref[rows, :]
        else:
            gain_, shift_ = gain, shift
        h_ref[rows, :] = (_rms(x) * gain_ + shift_).astype(h_ref.dtype)
        return carry

    lax.fori_loop(0, x_ref.shape[0] // NORM_ROWS, body, 0, unroll=4)


def _log_sigmoid(z):
    return jnp.minimum(z, 0.0) - jnp.log1p(jnp.exp(-jnp.abs(z)))


CAST_BLOCK_BYTES = 8 * 1024 * 1024


def _cast_kernel(*refs):
    n = len(refs) // 2
    for x_ref, o_ref in zip(refs[:n], refs[n:]):
        o_ref[...] = x_ref[...].astype(o_ref.dtype)


def _to_bf16(*ws):
    steps = 1
    while any(w.size * 4 // steps > CAST_BLOCK_BYTES for w in ws):
        steps *= 2
    specs = []
    for w in ws:
        layers, rows, cols = w.shape
        per_layer = steps // layers
        assert per_layer >= 1 and rows % (16 * per_layer) == 0
        specs.append(pl.BlockSpec((None, rows // per_layer, cols),
                                  lambda i, per_layer=per_layer: (i // per_layer, i % per_layer, 0)))
    return pl.pallas_call(
        _cast_kernel, grid=(steps,), in_specs=specs, out_specs=specs,
        out_shape=[jax.ShapeDtypeStruct(w.shape, BF16) for w in ws],
        compiler_params=_cp(1), name="weight_to_bf16",
    )(*ws)


def _ada_kernel(c_ref, w_ref, b_ref, o_ref):
    c = c_ref[...]
    a = (c / (1.0 + jnp.exp(-c))).astype(BF16)
    o_ref[...] = _dot(a, w_ref[...].astype(BF16)) + b_ref[...]


def _ada(c_all, ada_w, ada_b):
    depth, d, n6 = ada_w.shape
    r = c_all.shape[0]
    tn = min(1024, n6)
    return pl.pallas_call(
        _ada_kernel,
        grid=(depth, n6 // tn),
        in_specs=[pl.BlockSpec((r, d), lambda l, n: (0, 0)),
                  pl.BlockSpec((None, d, tn), lambda l, n: (l, 0, n)),
                  pl.BlockSpec((None, 1, tn), lambda l, n: (l, 0, n))],
        out_specs=pl.BlockSpec((None, r, tn), lambda l, n: (l, 0, n)),
        out_shape=jax.ShapeDtypeStruct((depth, r, n6), F32),
        compiler_params=_cp(2), name="ada_params",
    )(c_all, ada_w, ada_b.reshape(depth, 1, n6))


def _fox_in_kernel(x_ref, g_ref, sh_ref, sc_ref, wq_ref, wk_ref, wv_ref, wf_ref, bf_ref, *rest,
                   head_major, hd, nh, q_scale):
    if head_major:
        qb_ref, kb_ref, vb_ref, k_ref, v_ref, lf_ref, n2_ref, h_ref = rest
    else:
        qb_ref, k_ref, v_ref, lf_ref, h_ref = rest

    @pl.when(pl.program_id(1) == 0)
    def _():
        _norm_mod_store(h_ref, x_ref, g_ref, sh_ref, sc_ref)
        z = _dot(h_ref[...], wf_ref[...])[:, :nh] + bf_ref[...]
        lf_ref[...] = _log_sigmoid(z)

    hb = h_ref[...]
    q = _dot(hb, wq_ref[...]) * q_scale
    k = _dot(hb, wk_ref[...])
    v = _dot(hb, wv_ref[...])
    k_ref[...] = k
    v_ref[...] = v
    if head_major:
        for j in range(q.shape[1] // hd):
            sl = slice(j * hd, (j + 1) * hd)
            qb_ref[j] = q[:, sl].astype(BF16)
            kb_ref[j] = k[:, sl].astype(BF16)
            vb_ref[j] = v[:, sl].astype(BF16)
        tm, tn = q.shape
        grp = lax.shift_right_logical(lax.broadcasted_iota(jnp.int32, (2 * tn, LANES), 0), hd.bit_length() - 1)
        sel = jnp.where(grp == lax.broadcasted_iota(jnp.int32, (2 * tn, LANES), 1), 1.0, 0.0).astype(BF16)
        n2 = _dot(jnp.concatenate([q * q, k * k], axis=1).astype(BF16), sel)
        n2_ref[...] = jnp.max(n2.reshape(tm // 8, 8, LANES), axis=0)
    else:
        qb_ref[...] = q.astype(BF16)


def _mod_spec(mod, tpb, cols=None):
    _, rows, d = mod.shape
    if cols is None:
        return pl.BlockSpec((None, rows, d), lambda i, n: (i // tpb, 0, 0))
    return pl.BlockSpec((None, rows, cols), lambda i, n: (i // tpb, 0, n))


def _fox_in(x, g, shift, scale, w_in, w_f, b_f, *, layer, tm, tpb, head_major, nh, hd):
    m, d = x.shape
    tn = min(FOX_IN_COLS, d)
    nt = d // tn
    hpt = tn // hd
    nb, s = m // (tm * tpb), tm * tpb
    in_specs = [pl.BlockSpec((tm, d), lambda i, n: (i, 0)),
                pl.BlockSpec((1, d), lambda i, n: (0, 0)),
                _mod_spec(shift, tpb), _mod_spec(scale, tpb),
                pl.BlockSpec((None, d, tn), lambda i, n: (layer, 0, n)),
                pl.BlockSpec((None, d, tn), lambda i, n: (layer, 0, n + nt)),
                pl.BlockSpec((None, d, tn), lambda i, n: (layer, 0, n + 2 * nt)),
                pl.BlockSpec((d, LANES), lambda i, n: (0, 0)),
                pl.BlockSpec((1, nh), lambda i, n: (0, 0))]
    rm_spec = pl.BlockSpec((tm, tn), lambda i, n: (i, n))
    lf_spec = pl.BlockSpec((tm, nh), lambda i, n: (i, 0))
    if head_major:
        hm_spec = pl.BlockSpec((None, hpt, tm, hd), lambda i, n: (i // tpb, n, i % tpb, 0))
        hm_shape = jax.ShapeDtypeStruct((nb, nh, s, hd), BF16)
        assert hd & (hd - 1) == 0 and 2 * hpt <= LANES
        n2_spec = pl.BlockSpec((8, LANES), lambda i, n: (i, n))
        out_specs = [hm_spec, hm_spec, hm_spec, rm_spec, rm_spec, lf_spec, n2_spec]
        out_shape = [hm_shape, hm_shape, hm_shape]
    else:
        out_specs = [rm_spec, rm_spec, rm_spec, lf_spec]
        out_shape = [jax.ShapeDtypeStruct((m, d), BF16)]
    out_shape += [jax.ShapeDtypeStruct((m, d), F32), jax.ShapeDtypeStruct((m, d), F32),
                  jax.ShapeDtypeStruct((m, nh), F32)]
    if head_major:
        out_shape += [jax.ShapeDtypeStruct((m // tm * 8, nt * LANES), F32)]
    return pl.pallas_call(
        functools.partial(_fox_in_kernel, head_major=head_major, hd=hd, nh=nh, q_scale=hd ** -0.5 * LOG2E),
        grid=(m // tm, nt), in_specs=in_specs, out_specs=out_specs, out_shape=out_shape,
        scratch_shapes=[pltpu.VMEM((tm, d), BF16)],
        compiler_params=_cp(2), name="fox_in_proj",
    )(x, g, shift, scale, w_in, w_in, w_in, w_f, b_f)


def _split3(x):
    x1 = x.astype(BF16)
    r1 = x - x1.astype(F32)
    x2 = r1.astype(BF16)
    x3 = (r1 - x2.astype(F32)).astype(BF16)
    return x1, x2, x3


def _cumsum_kernel(x_ref, u_ref, l_ref, o_ref):
    u = u_ref[...]
    lo = l_ref[...]
    y = sum(_dot(p, u) for p in _split3(x_ref[...]))
    tot = jnp.broadcast_to(y[:, LANES - 1:LANES], y.shape)
    o_ref[...] = (y + sum(_dot(lo, p) for p in _split3(tot))) * LOG2E


def _cumsum_rows(x, nc):
    rows = x.shape[0]
    assert LANES % nc == 0 or nc % LANES == 0
    n = max(nc, LANES)
    assert rows % n == 0
    u = jnp.asarray(np.triu(np.ones((LANES, LANES), np.float32)), BF16)
    idx = np.arange(n)
    same = (idx[:, None] // nc) == (idx[None, :] // nc)
    lo = jnp.asarray((same & (idx[None, :] < idx[:, None])).astype(np.float32), BF16)
    return pl.pallas_call(
        _cumsum_kernel, grid=(rows // n,),
        in_specs=[pl.BlockSpec((n, LANES), lambda i: (i, 0)),
                  pl.BlockSpec((LANES, LANES), lambda i: (0, 0)),
                  pl.BlockSpec((n, n), lambda i: (0, 0))],
        out_specs=pl.BlockSpec((n, LANES), lambda i: (i, 0)),
        out_shape=jax.ShapeDtypeStruct(x.shape, F32),
        compiler_params=_cp(1), name="logf_cumsum",
    )(x, u, lo)


NORM_SLACK = 1.01


def _head_norm_bounds(n2, nb, nh, hpt):
    n2 = jnp.max(n2.reshape(nb, -1, nh // hpt, LANES), axis=1)
    bound = lambda part: (NORM_SLACK * jnp.sqrt(part)).reshape(-1)
    return bound(n2[:, :, :hpt]), bound(n2[:, :, hpt:2 * hpt])


def _fox_first_blocks(c, qmax, kmax, tk):
    bh, s = c.shape
    cb = c.reshape(bh, s // tk, tk)
    c_start, c_end = cb[:, :, 0], cb[:, :, -1]
    budget = (2.0 * qmax * kmax + FOX_SKIP_LOG2)[:, None, None]
    skippable = (c_end[:, None, :] - c_start[:, :, None]) > budget
    earlier = np.tril(np.ones((s // tk, s // tk), bool), -1)
    return jnp.sum(skippable & earlier, axis=-1).astype(jnp.int32).reshape(-1)


def _fox_attn_kernel(first_ref, q_ref, k_ref, v_ref, c_ref, o_ref, m_ref, acc_ref, *, tk, nsub, hd):
    b, h, qi = pl.program_id(0), pl.program_id(1), pl.program_id(2)
    nkb = pl.num_programs(2) * nsub
    base = qi * nsub
    m_ref[...] = jnp.full(m_ref.shape, NEG, F32)
    acc_ref[...] = jnp.zeros(acc_ref.shape, F32)
    ones = jnp.ones((tk, hd), BF16)
    ncol = tk // LANES
    c_first = [c_ref[base + r][:, 0:1] for r in range(nsub)]

    def step(r, j, mask=None, valid=None):
        rows = pl.ds(pl.multiple_of(j * tk, tk), tk)
        c_row = c_ref[j]
        if valid is not None:
            c_row = jnp.where(valid, c_row, -NEG_BLOCK)
        s = _dot_nt(q_ref[r * tk:(r + 1) * tk, :], k_ref[rows, :]) + (c_first[r] - c_row)
        if mask is not None:
            s = jnp.where(mask, s, NEG)
        cols = [s[:, c * LANES:(c + 1) * LANES] for c in range(ncol)]
        m_cur = functools.reduce(jnp.maximum, cols)
        m_prev = m_ref[r]
        m_next = jnp.maximum(m_prev, jnp.max(m_cur, axis=-1, keepdims=True))
        alpha = jnp.exp2(m_prev - m_next)
        p = jnp.concatenate([jnp.exp2(col - m_next) for col in cols], axis=1).astype(BF16)
        pv = _dot(p, jnp.concatenate([v_ref[rows, :], ones], axis=1))
        acc_ref[r] = jnp.concatenate([alpha, alpha], axis=1) * acc_ref[r] + pv
        m_ref[r] = m_next

    off = (b * pl.num_programs(1) + h) * nkb + base
    n = functools.reduce(jnp.maximum, [base + r - first_ref[off + r] for r in range(nsub)])

    def body(t, carry):
        for r in range(nsub):
            j = base + r - n + t
            step(r, jnp.maximum(j, 0), valid=j >= 0)
        return carry

    lax.fori_loop(0, n, body, 0)
    row = lax.broadcasted_iota(jnp.int32, (tk, tk), 0)
    col = lax.broadcasted_iota(jnp.int32, (tk, tk), 1)
    causal = col <= row
    for r in range(nsub):
        step(r, base + r, mask=causal)
    for r in range(nsub):
        acc = acc_ref[r]
        o_ref[r * tk:(r + 1) * tk, :] = (acc[:, :hd] / acc[:, hd:]).astype(o_ref.dtype)


def _fox_attn(qh, kh, vh, c, first, *, tk, nsub):
    nb, nh, s, hd = qh.shape
    tq = tk * nsub
    nq = s // tq
    kv_spec = pl.BlockSpec((None, None, s, hd), lambda b, h, i, f: (b, h, 0, 0))
    return pl.pallas_call(
        functools.partial(_fox_attn_kernel, tk=tk, nsub=nsub, hd=hd),
        grid_spec=pltpu.PrefetchScalarGridSpec(
            num_scalar_prefetch=1, grid=(nb, nh, nq),
            in_specs=[pl.BlockSpec((None, None, tq, hd), lambda b, h, i, f: (b, h, i, 0)),
                      kv_spec, kv_spec,
                      pl.BlockSpec((None, s // tk, 1, tk), lambda b, h, i, f: (b * nh + h, 0, 0, 0))],
            out_specs=pl.BlockSpec((None, tq, hd), lambda b, h, i, f: (b, i, h)),
            scratch_shapes=[pltpu.VMEM((nsub, tk, LANES), F32), pltpu.VMEM((nsub, tk, 2 * hd), F32)]),
        out_shape=jax.ShapeDtypeStruct((nb, s, nh * hd), BF16),
        compiler_params=_cp(3), name="fox_attention",
    )(first, qh, kh, vh, c.reshape(nb * nh, s // tk, 1, tk))


def _fox_sample_kernel(q_ref, kc_ref, vc_ref, kn_ref, vn_ref, c_ref, cf_ref, o_ref, *, hb, hd, p_len, t):
    n = p_len * hb
    kf = kc_ref[...].reshape(n, hd).astype(BF16)
    vf = vc_ref[...].reshape(n, hd).astype(BF16)
    heads = [slice(j * hd, (j + 1) * hd) for j in range(hb)]
    q8 = jnp.concatenate([q_ref[:, sl] for sl in heads], axis=0)
    row_head = lax.shift_right_logical(lax.broadcasted_iota(jnp.int32, (hb * t, LANES), 0), t.bit_length() - 1)
    col_head = lax.broadcasted_iota(jnp.int32, (hb * t, LANES), 1) & (hb - 1)
    own = jnp.where(row_head == col_head, 0.0, NEG)
    s = _dot_nt(q8, kf)
    cf = cf_ref[...]
    cols = [s[:, g * LANES:(g + 1) * LANES] - cf[:, g * LANES:(g + 1) * LANES] + own for g in range(n // LANES)]
    tri = lax.broadcasted_iota(jnp.int32, (t, t), 1) <= lax.broadcasted_iota(jnp.int32, (t, t), 0)
    s_n = jnp.concatenate(
        [jnp.where(tri, _dot_nt(q_ref[:, sl], kn_ref[:, sl].astype(BF16)) - c_ref[j:j + 1, p_len:p_len + t], NEG)
         for j, sl in enumerate(heads)], axis=0)
    m = jnp.maximum(jnp.max(functools.reduce(jnp.maximum, cols), axis=-1, keepdims=True),
                    jnp.max(s_n, axis=-1, keepdims=True))
    ps = [jnp.exp2(col - m) for col in cols]
    p_n = jnp.exp2(s_n - m)
    den = jnp.sum(functools.reduce(jnp.add, ps), axis=-1, keepdims=True) + jnp.sum(p_n, axis=-1, keepdims=True)
    o = _dot(jnp.concatenate(ps, axis=1).astype(BF16), vf)
    o_n = jnp.concatenate([_dot(p_n[j * t:(j + 1) * t].astype(BF16), vn_ref[:, sl].astype(BF16))
                           for j, sl in enumerate(heads)], axis=0)
    o = (o + o_n) / den
    for j, sl in enumerate(heads):
        o_ref[:, sl] = o[j * t:(j + 1) * t].astype(o_ref.dtype)


def _fox_sample_attn(q, k_cache, v_cache, k_new, v_new, c_all, *, layer, t):
    _, nb, p_len, nh, hd = k_cache.shape
    hb = min(8, nh)
    assert t & (t - 1) == 0 and hb & (hb - 1) == 0 and (p_len * hb) % LANES == 0 and LANES % hb == 0
    w = hb * hd
    lc = c_all.shape[-1]
    c_flat = c_all[:, :, :p_len].reshape(nb, nh // hb, hb, p_len).transpose(0, 1, 3, 2)
    c_flat = c_flat.reshape(nb, nh // hb, 1, p_len * hb)
    cache_spec = pl.BlockSpec((None, None, p_len, hb, hd), lambda b, h: (layer, b, 0, h, 0))
    row_spec = pl.BlockSpec((t, w), lambda b, h: (b, h))
    return pl.pallas_call(
        functools.partial(_fox_sample_kernel, hb=hb, hd=hd, p_len=p_len, t=t),
        grid=(nb, nh // hb),
        in_specs=[row_spec, cache_spec, cache_spec, row_spec, row_spec,
                  pl.BlockSpec((None, hb, lc), lambda b, h: (b, h, 0)),
                  pl.BlockSpec((None, None, 1, p_len * hb), lambda b, h: (b, h, 0, 0))],
        out_specs=row_spec,
        out_shape=jax.ShapeDtypeStruct((nb * t, nh * hd), BF16),
        compiler_params=_cp(2), name="fox_sample_attention",
    )(q, k_cache, v_cache, k_new, v_new, c_all, c_flat)


def _proj_res_kernel(a_ref, w_ref, x_ref, gate_ref, o_ref):
    o_ref[...] = x_ref[...] + gate_ref[...] * _dot(a_ref[...], w_ref[...])


def _proj_res(a, w, x, gate, *, layer, tm, tpb):
    m, k = a.shape
    n = w.shape[2]
    return pl.pallas_call(
        _proj_res_kernel, grid=(m // tm, 1),
        in_specs=[pl.BlockSpec((tm, k), lambda i, j: (i, 0)),
                  pl.BlockSpec((None, k, n), lambda i, j: (layer, 0, 0)),
                  pl.BlockSpec((tm, n), lambda i, j: (i, 0)),
                  _mod_spec(gate, tpb)],
        out_specs=pl.BlockSpec((tm, n), lambda i, j: (i, 0)),
        out_shape=jax.ShapeDtypeStruct((m, n), F32),
        compiler_params=_cp(2), name="attn_out_proj",
    )(a, w, x, gate)


def _ffn_kernel(x_ref, g_ref, sh_ref, sc_ref, gate_ref, wu_ref, wd_ref, *rest, final):
    if final:
        fg_ref, o_ref, h_ref, acc_ref = rest
    else:
        o_ref, h_ref, acc_ref = rest
    f = pl.program_id(1)

    @pl.when(f == 0)
    def _():
        _norm_mod_store(h_ref, x_ref, g_ref, sh_ref, sc_ref)
        acc_ref[...] = jnp.zeros(acc_ref.shape, F32)

    a = jnp.maximum(_dot(h_ref[...], wu_ref[...]), 0.0)
    acc_ref[...] += _dot((a * a).astype(BF16), wd_ref[...])

    @pl.when(f == pl.num_programs(1) - 1)
    def _():
        y = x_ref[...] + gate_ref[...] * acc_ref[...]
        if final:
            y = _rms(y) * fg_ref[...]
        o_ref[...] = y


def _ffn(x, g, shift, scale, gate, w_up, w_down, final_g, *, layer, tm, tpb):
    m, d = x.shape
    ff = w_up.shape[2]
    tf = min(ff, FFN_STEP_ELEMS // tm)
    vec = pl.BlockSpec((1, d), lambda i, f: (0, 0))
    in_specs = [pl.BlockSpec((tm, d), lambda i, f: (i, 0)), vec,
                _mod_spec(shift, tpb), _mod_spec(scale, tpb), _mod_spec(gate, tpb),
                pl.BlockSpec((None, d, tf), lambda i, f: (layer, 0, f)),
                pl.BlockSpec((None, tf, d), lambda i, f: (layer, f, 0))]
    args = [x, g, shift, scale, gate, w_up, w_down]
    if final_g is not None:
        in_specs.append(vec)
        args.append(final_g)
    return pl.pallas_call(
        functools.partial(_ffn_kernel, final=final_g is not None),
        grid=(m // tm, ff // tf), in_specs=in_specs,
        out_specs=pl.BlockSpec((tm, d), lambda i, f: (i, 0)),
        out_shape=jax.ShapeDtypeStruct((m, d), F32),
        scratch_shapes=[pltpu.VMEM((tm, d), BF16), pltpu.VMEM((tm, d), F32)],
        compiler_params=_cp(2), name="ffn_final" if final_g is not None else "ffn",
    )(*args)


def _rope(r, a, b, c):
    half_shift = (LANES // 8) // 2
    cols = []
    for j in range(r.shape[1] // LANES):
        x = r[:, j * LANES:(j + 1) * LANES]
        cols.append(x * a + pltpu.roll(x, LANES - half_shift, 1) * b + pltpu.roll(x, half_shift, 1) * c)
    return jnp.concatenate(cols, axis=1)


def _dup_halves(x, hd):
    lo = lax.broadcasted_iota(jnp.int32, (x.shape[0], LANES), 1) < hd
    zero = jnp.zeros((x.shape[0], LANES), x.dtype)
    out = []
    for j in range(x.shape[1] // LANES):
        p = x[:, j * LANES:(j + 1) * LANES]
        r = pltpu.roll(p, hd, 1)
        out += [jnp.where(lo, p, zero), jnp.where(lo, zero, r), jnp.where(lo, r, zero), jnp.where(lo, zero, p)]
    return jnp.concatenate(out, axis=1)


def _swa_in_kernel(x_ref, g_ref, sh_ref, sc_ref, w_ref, ra_ref, rb_ref, rc_ref,
                   q_ref, kk_ref, vv_ref, k_ref, v_ref, h_ref, *, kvd, hd, q_scale):
    _norm_mod_store(h_ref, x_ref, g_ref, sh_ref, sc_ref)
    h = h_ref[...]
    tabs = (ra_ref[...], rb_ref[...], rc_ref[...])
    d = q_ref.shape[1]
    tn = 2 * kvd
    for n in range(d // tn):
        r = _dot(h, w_ref[:, n * tn:(n + 1) * tn])
        q_ref[:, n * tn:(n + 1) * tn] = (_rope(r, *tabs) * q_scale).astype(BF16)
    r = _dot(h, w_ref[:, d:])
    k = _rope(r[:, :kvd], *tabs)
    v = r[:, kvd:]
    k_ref[...] = k
    v_ref[...] = v
    kk_ref[...] = _dup_halves(k, hd).astype(BF16)
    vv_ref[...] = _dup_halves(v, hd).astype(BF16)


def _swa_in(x, g, shift, scale, w, tabs, *, layer, tm, tpb, kvd, hd):
    m, d = x.shape
    ntab = tabs[0].shape[0] // tm
    kkw = (kvd // hd) * 2 * LANES
    tab_spec = pl.BlockSpec((tm, LANES), lambda i, n: (i % ntab, 0))
    const = lambda w_: pl.BlockSpec((tm, w_), lambda i, n: (i, 0))
    return pl.pallas_call(
        functools.partial(_swa_in_kernel, kvd=kvd, hd=hd, q_scale=hd ** -0.5 * LOG2E),
        grid=(m // tm, 1),
        in_specs=[pl.BlockSpec((tm, d), lambda i, n: (i, 0)),
                  pl.BlockSpec((1, d), lambda i, n: (0, 0)),
                  _mod_spec(shift, tpb), _mod_spec(scale, tpb),
                  pl.BlockSpec((None, d, d + 2 * kvd), lambda i, n: (layer, 0, 0)),
                  tab_spec, tab_spec, tab_spec],
        out_specs=[const(d), const(kkw), const(kkw), const(kvd), const(kvd)],
        out_shape=[jax.ShapeDtypeStruct((m, d), BF16),
                   jax.ShapeDtypeStruct((m, kkw), BF16), jax.ShapeDtypeStruct((m, kkw), BF16),
                   jax.ShapeDtypeStruct((m, kvd), F32), jax.ShapeDtypeStruct((m, kvd), F32)],
        scratch_shapes=[pltpu.VMEM((tm, d), BF16)],
        compiler_params=_cp(2), name="swa_in_proj",
    )(x, g, shift, scale, w, *tabs)


def _rope_tables(pos, hd, rope_dims):
    half = rope_dims // 2
    inv_freq = ROPE_THETA ** (-jnp.arange(half, dtype=F32) * 2.0 / rope_dims)
    ang = pos.astype(F32)[:, None] * inv_freq[None, :]
    cos, sin = jnp.cos(ang), jnp.sin(ang)
    lane = np.arange(LANES) % hd
    idx = lane % half
    a = jnp.where(lane < rope_dims, cos[:, idx], 1.0)
    b = jnp.where(lane < half, -sin[:, idx], 0.0)
    c = jnp.where((lane >= half) & (lane < rope_dims), sin[:, idx], 0.0)
    return a, b, c


def _swa_core(q, k_top, k_bot, v_top, v_bot, bias, sinks, *, group):
    t = q.shape[0]
    npair = group // 2
    win = k_top.shape[0]
    half_lanes = lax.broadcasted_iota(jnp.int32, (win, LANES), 1) < LANES // 2
    count_even = jnp.where(half_lanes, 1.0, 0.0).astype(BF16)
    count_odd = jnp.where(half_lanes, 0.0, 1.0).astype(BF16)
    v2 = jnp.concatenate([jnp.concatenate([v_top, count_even], axis=1),
                          jnp.concatenate([v_bot, count_odd], axis=1)], axis=0)
    qs = jnp.concatenate([q[:, p * LANES:(p + 1) * LANES] for p in range(npair)], axis=0)
    s = _dot_nt(qs, jnp.concatenate([k_top, k_bot], axis=0)) + bias
    ps, es = [], []
    for half in range(2):
        cols = [s[:, half * win + c * LANES: half * win + (c + 1) * LANES] for c in range(win // LANES)]
        sink = sinks[:, half * LANES:(half + 1) * LANES]
        m = jnp.maximum(jnp.max(functools.reduce(jnp.maximum, cols), axis=-1, keepdims=True), sink)
        ps += [jnp.exp2(col - m) for col in cols]
        es.append(jnp.exp2(sink - m))
    o = _dot(jnp.concatenate(ps, axis=1).astype(BF16), v2)
    lane = lax.broadcasted_iota(jnp.int32, (o.shape[0], LANES), 1)
    o = o[:, :LANES] / (o[:, LANES:] + jnp.where(lane < LANES // 2, es[0], es[1]))
    return jnp.concatenate([o[p * t:(p + 1) * t] for p in range(npair)], axis=1)


def _swa_attn_kernel(q_ref, kh_ref, km_ref, vh_ref, vm_ref, bias_ref, sink_ref, o_ref, *, kv, group, hd):
    gw = group * hd
    bias = bias_ref[...]
    for g in range(kv):
        def win(halo, main, off):
            sl = slice(g * 2 * LANES + off, g * 2 * LANES + off + LANES)
            return jnp.concatenate([halo[:, sl], main[:, sl]], axis=0)
        o = _swa_core(q_ref[:, g * gw:(g + 1) * gw],
                      win(kh_ref, km_ref, 0), win(kh_ref, km_ref, LANES),
                      win(vh_ref, vm_ref, 0), win(vh_ref, vm_ref, LANES),
                      bias, sink_ref[g], group=group)
        o_ref[:, g * gw:(g + 1) * gw] = o.astype(o_ref.dtype)


def _swa_attn(q, kk, vv, bias2, sink_rows, *, nb, s, kv, group, hd):
    t = 2 * CHUNK
    nt = s // t
    d = q.shape[1]
    kkw = kk.shape[1]
    main = pl.BlockSpec((t, kkw), lambda b, i: (b * nt + i, 0))
    halo = pl.BlockSpec((t, kkw), lambda b, i: (b * nt + jnp.maximum(i - 1, 0), 0))
    return pl.pallas_call(
        functools.partial(_swa_attn_kernel, kv=kv, group=group, hd=hd),
        grid=(nb, nt),
        in_specs=[pl.BlockSpec((t, d), lambda b, i: (b * nt + i, 0)),
                  halo, main, halo, main,
                  pl.BlockSpec((None,) + bias2.shape[1:], lambda b, i: (jnp.minimum(i, 1), 0, 0)),
                  pl.BlockSpec(sink_rows.shape, lambda b, i: (0, 0, 0))],
        out_specs=pl.BlockSpec((t, d), lambda b, i: (b * nt + i, 0)),
        out_shape=jax.ShapeDtypeStruct(q.shape, BF16),
        compiler_params=_cp(2), name="swa_attention",
    )(q, kk, kk, vv, vv, bias2, sink_rows)


def _swa_sample_kernel(q_ref, kc_ref, vc_ref, kkn_ref, vvn_ref, bias_ref, sink_ref, o_ref, *, kv, group, hd, t):
    gw = group * hd
    kkc = _dup_halves(kc_ref[...], hd).astype(BF16)
    vvc = _dup_halves(vc_ref[...], hd).astype(BF16)
    pad = jnp.zeros((kc_ref.shape[0] - t, LANES), BF16)
    for g in range(kv):
        def win(cache, new, off):
            sl = slice(g * 2 * LANES + off, g * 2 * LANES + off + LANES)
            return jnp.concatenate([cache[:, sl], new[:, sl], pad], axis=0)
        o = _swa_core(q_ref[:, g * gw:(g + 1) * gw],
                      win(kkc, kkn_ref, 0), win(kkc, kkn_ref, LANES),
                      win(vvc, vvn_ref, 0), win(vvc, vvn_ref, LANES),
                      bias_ref[...], sink_ref[g], group=group)
        o_ref[:, g * gw:(g + 1) * gw] = o.astype(o_ref.dtype)


def _swa_sample_attn(q, k_cache, v_cache, kk_new, vv_new, bias, sink_rows, *, kv, group, hd, t):
    nb, buf, kvd = k_cache.shape
    d = q.shape[1]
    full = lambda a: pl.BlockSpec(a.shape, lambda b: (0,) * a.ndim)
    cache_spec = pl.BlockSpec((None, buf, kvd), lambda b: (b, 0, 0))
    new_spec = pl.BlockSpec((t, kk_new.shape[1]), lambda b: (b, 0))
    return pl.pallas_call(
        functools.partial(_swa_sample_kernel, kv=kv, group=group, hd=hd, t=t),
        grid=(nb,),
        in_specs=[pl.BlockSpec((t, d), lambda b: (b, 0)), cache_spec, cache_spec, new_spec, new_spec,
                  full(bias), full(sink_rows)],
        out_specs=pl.BlockSpec((t, d), lambda b: (b, 0)),
        out_shape=jax.ShapeDtypeStruct(q.shape, BF16),
        compiler_params=_cp(1), name="swa_sample_attention",
    )(q, k_cache, v_cache, kk_new, vv_new, bias, sink_rows)


def _window_bias(valid, npair):
    b = np.where(valid, 0.0, NEG).astype(np.float32)
    return np.tile(b, (npair, 2))


def _prompt_bias(npair):
    t = 2 * CHUNK
    qc = np.arange(t)[:, None] // CHUNK
    kc = np.arange(t + SWA_WINDOW_CHUNKS * CHUNK)[None, :] // CHUNK
    valid = (kc >= qc) & (kc <= qc + SWA_WINDOW_CHUNKS)
    first = valid & (kc >= SWA_WINDOW_CHUNKS)
    return jnp.asarray(np.stack([_window_bias(first, npair), _window_bias(valid, npair)]))


def _sample_bias(past_len, buf, t, npair):
    q_pos = past_len + np.arange(t)
    k_pos = np.concatenate([past_len - buf + np.arange(buf), q_pos])
    qch, kch = q_pos // CHUNK, k_pos // CHUNK
    valid = np.zeros((t, 2 * buf), bool)
    valid[:, :buf + t] = (kch[None, :] <= qch[:, None]) & (kch[None, :] >= qch[:, None] - SWA_WINDOW_CHUNKS)
    return jnp.asarray(_window_bias(valid, npair))


def _sink_rows(sinks, kv, group, t):
    s = (sinks * LOG2E).reshape(kv, group // 2, 1, 2, 1)
    return jnp.broadcast_to(s, (kv, group // 2, t, 2, LANES)).reshape(kv, (group // 2) * t, 2 * LANES)


def kernel(x_prompt, x_sample, c_prompt, c_sample, cache_fox_k, cache_fox_v, cache_fox_logf, cache_swa_k,
           cache_swa_v, ada_w, ada_b, norm_mix_g, norm_ffn_g, fox_w_in, fox_b_f, fox_w_out, swa_w_in,
           swa_sinks, swa_w_out, ffn_w_up, ffn_w_down, final_g):
    bp, s, d = x_prompt.shape
    bs, t, _ = x_sample.shape
    depth = ada_w.shape[0]
    past_len = cache_fox_k.shape[2]
    nh_fox, hd_fox = cache_fox_k.shape[3], cache_fox_k.shape[4]
    buf, kv, hd_swa = cache_swa_k.shape[2], cache_swa_k.shape[3], cache_swa_k.shape[4]
    nh_swa = swa_sinks.shape[1]
    group = nh_swa // kv
    kvd = kv * hd_swa
    rope_dims = hd_swa // 4
    assert hd_fox == LANES and hd_swa == LANES // 2 and rope_dims == 16 and group % 2 == 0
    assert buf == SWA_WINDOW_CHUNKS * CHUNK and t <= buf and s % (2 * CHUNK) == 0

    mp, ms = bp * s, bs * t
    tm_p = min(512, s)
    tpb_p = s // tm_p
    tm_fox = min(1024, s)
    tk = min(FOX_TK, s)
    tq = min(FOX_NSUB * tk, s)

    mods = _ada(jnp.concatenate([c_prompt, c_sample], axis=0), ada_w, ada_b)

    def split_mods(i):
        six = jnp.split(mods[i], 6, axis=-1)
        prompt = [m[:bp, None, :] for m in six]
        sample = [jnp.repeat(m[bp:], t, axis=0)[None] for m in six]
        return prompt, sample

    row = lambda v: v.reshape(1, -1)
    xp = x_prompt.reshape(mp, d)
    xs = x_sample.reshape(ms, d)
    pos_p = jnp.arange(s)
    pos_s = past_len + jnp.arange(t)
    tabs_p = _rope_tables(pos_p, hd_swa, rope_dims)
    tabs_s = tuple(jnp.tile(a, (bs, 1)) for a in _rope_tables(pos_s, hd_swa, rope_dims))
    bias_p = _prompt_bias(group // 2)
    bias_s = _sample_bias(past_len, buf, t, group // 2)

    fox_w_in_b, fox_w_out_b, swa_w_in_b, swa_w_out_b = _to_bf16(fox_w_in, fox_w_out, swa_w_in, swa_w_out)
    (w_up_b,), (w_down_b,) = _to_bf16(ffn_w_up), _to_bf16(ffn_w_down)

    outs = {k: [] for k in ("fkp", "fvp", "flp", "fks", "fvs", "fls", "skp", "svp", "sks", "svs")}
    for i in range(depth):
        mod_p, mod_s = split_mods(i)
        j = i // 2
        if i % 2 == 0:
            w_f = jnp.pad(fox_w_in_b[j, :, 3 * d:], ((0, 0), (0, LANES - nh_fox)))
            b_f = row(fox_b_f[j])
            qh, kh, vh, k32, v32, lf, n2 = _fox_in(
                xp, row(norm_mix_g[i]), mod_p[0], mod_p[1], fox_w_in_b, w_f, b_f,
                layer=j, tm=tm_fox, tpb=s // tm_fox, head_major=True, nh=nh_fox, hd=hd_fox)
            lft = lf.reshape(bp, s, nh_fox).transpose(0, 2, 1)
            c = _cumsum_rows(lft.reshape(-1, LANES), s // LANES).reshape(bp * nh_fox, s)
            qmax, kmax = _head_norm_bounds(n2, bp, nh_fox, min(FOX_IN_COLS, d) // hd_fox)
            first = _fox_first_blocks(c, qmax, kmax, tk)
            op = _fox_attn(qh, kh, vh, c, first, tk=tk, nsub=tq // tk)
            xp = _proj_res(op.reshape(mp, d), fox_w_out_b, xp, mod_p[2], layer=j, tm=tm_p, tpb=tpb_p)
            outs["fkp"].append(k32.reshape(bp, s, nh_fox, hd_fox))
            outs["fvp"].append(v32.reshape(bp, s, nh_fox, hd_fox))
            outs["flp"].append(lf.reshape(bp, s, nh_fox))
            qs_, k32s, v32s, lfs = _fox_in(xs, row(norm_mix_g[i]), mod_s[0], mod_s[1], fox_w_in_b, w_f, b_f,
                                           layer=j, tm=ms, tpb=1, head_major=False, nh=nh_fox, hd=hd_fox)
            lf_all = jnp.concatenate([cache_fox_logf[j], lfs.reshape(bs, t, nh_fox)], axis=1)
            lc = LANES * int(2 ** np.ceil(np.log2(-(-(past_len + t) // LANES))))
            lf_all = jnp.pad(lf_all.transpose(0, 2, 1), ((0, 0), (0, 0), (0, lc - past_len - t)))
            c_s = _cumsum_rows(lf_all.reshape(-1, LANES), lc // LANES).reshape(bs, nh_fox, lc)
            os_ = _fox_sample_attn(qs_, cache_fox_k, cache_fox_v, k32s, v32s, c_s, layer=j, t=t)
            xs = _proj_res(os_, fox_w_out_b, xs, mod_s[2], layer=j, tm=ms, tpb=1)
            outs["fks"].append(k32s.reshape(bs, t, nh_fox, hd_fox))
            outs["fvs"].append(v32s.reshape(bs, t, nh_fox, hd_fox))
            outs["fls"].append(lfs.reshape(bs, t, nh_fox))
        else:
            q, kk, vv, k32, v32 = _swa_in(xp, row(norm_mix_g[i]), mod_p[0], mod_p[1], swa_w_in_b, tabs_p,
                                          layer=j, tm=tm_p, tpb=tpb_p, kvd=kvd, hd=hd_swa)
            op = _swa_attn(q, kk, vv, bias_p, _sink_rows(swa_sinks[j], kv, group, 2 * CHUNK),
                           nb=bp, s=s, kv=kv, group=group, hd=hd_swa)
            xp = _proj_res(op, swa_w_out_b, xp, mod_p[2], layer=j, tm=tm_p, tpb=tpb_p)
            outs["skp"].append(k32.reshape(bp, s, kvd)[:, -buf:].reshape(bp, buf, kv, hd_swa))
            outs["svp"].append(v32.reshape(bp, s, kvd)[:, -buf:].reshape(bp, buf, kv, hd_swa))
            q, kk, vv, k32, v32 = _swa_in(xs, row(norm_mix_g[i]), mod_s[0], mod_s[1], swa_w_in_b, tabs_s,
                                          layer=j, tm=ms, tpb=1, kvd=kvd, hd=hd_swa)
            os_ = _swa_sample_attn(q, cache_swa_k[j].reshape(bs, buf, kvd), cache_swa_v[j].reshape(bs, buf, kvd),
                                   kk, vv, bias_s, _sink_rows(swa_sinks[j], kv, group, t),
                                   kv=kv, group=group, hd=hd_swa, t=t)
            xs = _proj_res(os_, swa_w_out_b, xs, mod_s[2], layer=j, tm=ms, tpb=1)
            k_all = jnp.concatenate([cache_swa_k[j], k32.reshape(bs, t, kv, hd_swa)], axis=1)
            v_all = jnp.concatenate([cache_swa_v[j], v32.reshape(bs, t, kv, hd_swa)], axis=1)
            outs["sks"].append(k_all[:, -buf:])
            outs["svs"].append(v_all[:, -buf:])
        fg = row(final_g) if i == depth - 1 else None
        xp = _ffn(xp, row(norm_ffn_g[i]), mod_p[3], mod_p[4], mod_p[5], w_up_b, w_down_b, fg,
                  layer=i, tm=tm_p, tpb=tpb_p)
        xs = _ffn(xs, row(norm_ffn_g[i]), mod_s[3], mod_s[4], mod_s[5], w_up_b, w_down_b, fg,
                  layer=i, tm=ms, tpb=1)

    st = lambda k: jnp.stack(outs[k])
    return (xp.reshape(bp, s, d), xs.reshape(bs, t, d),
            st("fkp"), st("fvp"), st("flp"), st("fks"), st("fvs"), st("fls"),
            st("skp"), st("svp"), st("sks"), st("svs"))
```

```python
import functools

import numpy as np
import jax
import jax.numpy as jnp
from jax import lax
from jax.experimental import pallas as pl
from jax.experimental.pallas import tpu as pltpu

F32 = jnp.float32
BF16 = jnp.bfloat16

RMS_EPS = 1e-6
CHUNK = 64
SWA_WINDOW_CHUNKS = 2
ROPE_THETA = 500000.0
LANES = 128
NEG = -1e30
NEG_BLOCK = -3e38
VMEM_LIMIT_BYTES = 56 * 1024 * 1024
LOG2E = 1.4426950408889634
FOX_SKIP_LOG2 = 136.0
FFN_STEP_ELEMS = 512 * 1024
SWA_TILES_PER_STEP = 4
FOX_IN_COLS = 512
FOX_TK = 512
FOX_NSUB = 16


def _cp(n_axes):
    return pltpu.CompilerParams(dimension_semantics=("arbitrary",) * n_axes,
                                vmem_limit_bytes=VMEM_LIMIT_BYTES)


def _dot(a, b):
    return jnp.dot(a, b, preferred_element_type=F32)


def _dot_nt(a, b):
    return lax.dot_general(a, b, (((1,), (1,)), ((), ())), preferred_element_type=F32)


def _rms(x):
    return x * lax.rsqrt(jnp.mean(x * x, axis=-1, keepdims=True) + RMS_EPS)


NORM_ROWS = 16


def _norm_mod_store(h_ref, x_ref, g_ref, sh_ref, sc_ref):
    per_token = sh_ref.shape[0] > 1
    g = g_ref[...]
    if not per_token:
        gain, shift = g * (1.0 + sc_ref[...]), sh_ref[...]

    def body(i, carry):
        rows = pl.ds(pl.multiple_of(i * NORM_ROWS, NORM_ROWS), NORM_ROWS)
        x = x_ref[rows, :]
        if per_token:
            gain_, shift_ = g * (1.0 + sc_ref[rows, :]), sh_ref[rows, :]
        else:
            gain_, shift_ = gain, shift
        h_ref[rows, :] = (_rms(x) * gain_ + shift_).astype(h_ref.dtype)
        return carry

    lax.fori_loop(0, x_ref.shape[0] // NORM_ROWS, body, 0, unroll=4)


def _log_sigmoid(z):
    return jnp.minimum(z, 0.0) - jnp.log1p(jnp.exp(-jnp.abs(z)))


CAST_BLOCK_BYTES = 8 * 1024 * 1024


def _cast_kernel(*refs):
    n = len(refs) // 2
    for x_ref, o_ref in zip(refs[:n], refs[n:]):
        o_ref[...] = x_ref[...].astype(o_ref.dtype)


def _to_bf16(*ws):
    steps = 1
    while any(w.size * 4 // steps > CAST_BLOCK_BYTES for w in ws):
        steps *= 2
    specs = []
    for w in ws:
        layers, rows, cols = w.shape
        per_layer = steps // layers
        assert per_layer >= 1 and rows % (16 * per_layer) == 0
        specs.append(pl.BlockSpec((None, rows // per_layer, cols),
                                  lambda i, per_layer=per_layer: (i // per_layer, i % per_layer, 0)))
    return pl.pallas_call(
        _cast_kernel, grid=(steps,), in_specs=specs, out_specs=specs,
        out_shape=[jax.ShapeDtypeStruct(w.shape, BF16) for w in ws],
        compiler_params=_cp(1), name="weight_to_bf16",
    )(*ws)


def _ada_kernel(c_ref, w_ref, b_ref, o_ref):
    c = c_ref[...]
    a = (c / (1.0 + jnp.exp(-c))).astype(BF16)
    o_ref[...] = _dot(a, w_ref[...].astype(BF16)) + b_ref[...]


def _ada(c_all, ada_w, ada_b):
    depth, d, n6 = ada_w.shape
    r = c_all.shape[0]
    tn = min(1024, n6)
    return pl.pallas_call(
        _ada_kernel,
        grid=(depth, n6 // tn),
        in_specs=[pl.BlockSpec((r, d), lambda l, n: (0, 0)),
                  pl.BlockSpec((None, d, tn), lambda l, n: (l, 0, n)),
                  pl.BlockSpec((None, 1, tn), lambda l, n: (l, 0, n))],
        out_specs=pl.BlockSpec((None, r, tn), lambda l, n: (l, 0, n)),
        out_shape=jax.ShapeDtypeStruct((depth, r, n6), F32),
        compiler_params=_cp(2), name="ada_params",
    )(c_all, ada_w, ada_b.reshape(depth, 1, n6))


def _fox_in_kernel(x_ref, g_ref, sh_ref, sc_ref, wq_ref, wk_ref, wv_ref, wf_ref, bf_ref, *rest,
                   head_major, hd, nh, q_scale):
    if head_major:
        qb_ref, kb_ref, vb_ref, k_ref, v_ref, lf_ref, n2_ref, h_ref = rest
    else:
        qb_ref, k_ref, v_ref, lf_ref, h_ref = rest

    @pl.when(pl.program_id(1) == 0)
    def _():
        _norm_mod_store(h_ref, x_ref, g_ref, sh_ref, sc_ref)
        z = _dot(h_ref[...], wf_ref[...])[:, :nh] + bf_ref[...]
        lf_ref[...] = _log_sigmoid(z)

    hb = h_ref[...]
    q = _dot(hb, wq_ref[...]) * q_scale
    k = _dot(hb, wk_ref[...])
    v = _dot(hb, wv_ref[...])
    k_ref[...] = k
    v_ref[...] = v
    if head_major:
        for j in range(q.shape[1] // hd):
            sl = slice(j * hd, (j + 1) * hd)
            qb_ref[j] = q[:, sl].astype(BF16)
            kb_ref[j] = k[:, sl].astype(BF16)
            vb_ref[j] = v[:, sl].astype(BF16)
        tm, tn = q.shape
        grp = lax.shift_right_logical(lax.broadcasted_iota(jnp.int32, (2 * tn, LANES), 0), hd.bit_length() - 1)
        sel = jnp.where(grp == lax.broadcasted_iota(jnp.int32, (2 * tn, LANES), 1), 1.0, 0.0).astype(BF16)
        n2 = _dot(jnp.concatenate([q * q, k * k], axis=1).astype(BF16), sel)
        n2_ref[...] = jnp.max(n2.reshape(tm // 8, 8, LANES), axis=0)
    else:
        qb_ref[...] = q.astype(BF16)


def _mod_spec(mod, tpb, cols=None):
    _, rows, d = mod.shape
    if cols is None:
        return pl.BlockSpec((None, rows, d), lambda i, n: (i // tpb, 0, 0))
    return pl.BlockSpec((None, rows, cols), lambda i, n: (i // tpb, 0, n))


def _fox_in(x, g, shift, scale, w_in, w_f, b_f, *, layer, tm, tpb, head_major, nh, hd):
    m, d = x.shape
    tn = min(FOX_IN_COLS, d)
    nt = d // tn
    hpt = tn // hd
    nb, s = m // (tm * tpb), tm * tpb
    in_specs = [pl.BlockSpec((tm, d), lambda i, n: (i, 0)),
                pl.BlockSpec((1, d), lambda i, n: (0, 0)),
                _mod_spec(shift, tpb), _mod_spec(scale, tpb),
                pl.BlockSpec((None, d, tn), lambda i, n: (layer, 0, n)),
                pl.BlockSpec((None, d, tn), lambda i, n: (layer, 0, n + nt)),
                pl.BlockSpec((None, d, tn), lambda i, n: (layer, 0, n + 2 * nt)),
                pl.BlockSpec((d, LANES), lambda i, n: (0, 0)),
                pl.BlockSpec((1, nh), lambda i, n: (0, 0))]
    rm_spec = pl.BlockSpec((tm, tn), lambda i, n: (i, n))
    lf_spec = pl.BlockSpec((tm, nh), lambda i, n: (i, 0))
    if head_major:
        hm_spec = pl.BlockSpec((None, hpt, tm, hd), lambda i, n: (i // tpb, n, i % tpb, 0))
        hm_shape = jax.ShapeDtypeStruct((nb, nh, s, hd), BF16)
        assert hd & (hd - 1) == 0 and 2 * hpt <= LANES
        n2_spec = pl.BlockSpec((8, LANES), lambda i, n: (i, n))
        out_specs = [hm_spec, hm_spec, hm_spec, rm_spec, rm_spec, lf_spec, n2_spec]
        out_shape = [hm_shape, hm_shape, hm_shape]
    else:
        out_specs = [rm_spec, rm_spec, rm_spec, lf_spec]
        out_shape = [jax.ShapeDtypeStruct((m, d), BF16)]
    out_shape += [jax.ShapeDtypeStruct((m, d), F32), jax.ShapeDtypeStruct((m, d), F32),
                  jax.ShapeDtypeStruct((m, nh), F32)]
    if head_major:
        out_shape += [jax.ShapeDtypeStruct((m // tm * 8, nt * LANES), F32)]
    return pl.pallas_call(
        functools.partial(_fox_in_kernel, head_major=head_major, hd=hd, nh=nh, q_scale=hd ** -0.5 * LOG2E),
        grid=(m // tm, nt), in_specs=in_specs, out_specs=out_specs, out_shape=out_shape,
        scratch_shapes=[pltpu.VMEM((tm, d), BF16)],
        compiler_params=_cp(2), name="fox_in_proj",
    )(x, g, shift, scale, w_in, w_in, w_in, w_f, b_f)


def _split3(x):
    x1 = x.astype(BF16)
    r1 = x - x1.astype(F32)
    x2 = r1.astype(BF16)
    x3 = (r1 - x2.astype(F32)).astype(BF16)
    return x1, x2, x3


def _cumsum_kernel(x_ref, u_ref, l_ref, o_ref):
    u = u_ref[...]
    lo = l_ref[...]
    y = sum(_dot(p, u) for p in _split3(x_ref[...]))
    tot = jnp.broadcast_to(y[:, LANES - 1:LANES], y.shape)
    o_ref[...] = (y + sum(_dot(lo, p) for p in _split3(tot))) * LOG2E


def _cumsum_rows(x, nc):
    rows = x.shape[0]
    assert LANES % nc == 0 or nc % LANES == 0
    n = max(nc, LANES)
    assert rows % n == 0
    u = jnp.asarray(np.triu(np.ones((LANES, LANES), np.float32)), BF16)
    idx = np.arange(n)
    same = (idx[:, None] // nc) == (idx[None, :] // nc)
    lo = jnp.asarray((same & (idx[None, :] < idx[:, None])).astype(np.float32), BF16)
    return pl.pallas_call(
        _cumsum_kernel, grid=(rows // n,),
        in_specs=[pl.BlockSpec((n, LANES), lambda i: (i, 0)),
                  pl.BlockSpec((LANES, LANES), lambda i: (0, 0)),
                  pl.BlockSpec((n, n), lambda i: (0, 0))],
        out_specs=pl.BlockSpec((n, LANES), lambda i: (i, 0)),
        out_shape=jax.ShapeDtypeStruct(x.shape, F32),
        compiler_params=_cp(1), name="logf_cumsum",
    )(x, u, lo)


NORM_SLACK = 1.01


def _head_norm_bounds(n2, nb, nh, hpt):
    n2 = jnp.max(n2.reshape(nb, -1, nh // hpt, LANES), axis=1)
    bound = lambda part: (NORM_SLACK * jnp.sqrt(part)).reshape(-1)
    return bound(n2[:, :, :hpt]), bound(n2[:, :, hpt:2 * hpt])


def _fox_first_blocks(c, qmax, kmax, tk):
    bh, s = c.shape
    cb = c.reshape(bh, s // tk, tk)
    c_start, c_end = cb[:, :, 0], cb[:, :, -1]
    budget = (2.0 * qmax * kmax + FOX_SKIP_LOG2)[:, None, None]
    skippable = (c_end[:, None, :] - c_start[:, :, None]) > budget
    earlier = np.tril(np.ones((s // tk, s // tk), bool), -1)
    return jnp.sum(skippable & earlier, axis=-1).astype(jnp.int32).reshape(-1)


def _fox_attn_kernel(first_ref, q_ref, k_ref, v_ref, c_ref, o_ref, m_ref, acc_ref, *, tk, nsub, hd):
    b, h, qi = pl.program_id(0), pl.program_id(1), pl.program_id(2)
    nkb = pl.num_programs(2) * nsub
    base = qi * nsub
    m_ref[...] = jnp.full(m_ref.shape, NEG, F32)
    acc_ref[...] = jnp.zeros(acc_ref.shape, F32)
    ones = jnp.ones((tk, hd), BF16)
    ncol = tk // LANES
    c_first = [c_ref[base + r][:, 0:1] for r in range(nsub)]

    def step(r, j, mask=None, valid=None):
        rows = pl.ds(pl.multiple_of(j * tk, tk), tk)
        c_row = c_ref[j]
        if valid is not None:
            c_row = jnp.where(valid, c_row, -NEG_BLOCK)
        s = _dot_nt(q_ref[r * tk:(r + 1) * tk, :], k_ref[rows, :]) + (c_first[r] - c_row)
        if mask is not None:
            s = jnp.where(mask, s, NEG)
        cols = [s[:, c * LANES:(c + 1) * LANES] for c in range(ncol)]
        m_cur = functools.reduce(jnp.maximum, cols)
        m_prev = m_ref[r]
        m_next = jnp.maximum(m_prev, jnp.max(m_cur, axis=-1, keepdims=True))
        alpha = jnp.exp2(m_prev - m_next)
        p = jnp.concatenate([jnp.exp2(col - m_next) for col in cols], axis=1).astype(BF16)
        pv = _dot(p, jnp.concatenate([v_ref[rows, :], ones], axis=1))
        acc_ref[r] = jnp.concatenate([alpha, alpha], axis=1) * acc_ref[r] + pv
        m_ref[r] = m_next

    off = (b * pl.num_programs(1) + h) * nkb + base
    n = functools.reduce(jnp.maximum, [base + r - first_ref[off + r] for r in range(nsub)])

    def body(t, carry):
        for r in range(nsub):
            j = base + r - n + t
            step(r, jnp.maximum(j, 0), valid=j >= 0)
        return carry

    lax.fori_loop(0, n, body, 0)
    row = lax.broadcasted_iota(jnp.int32, (tk, tk), 0)
    col = lax.broadcasted_iota(jnp.int32, (tk, tk), 1)
    causal = col <= row
    for r in range(nsub):
        step(r, base + r, mask=causal)
    for r in range(nsub):
        acc = acc_ref[r]
        o_ref[r * tk:(r + 1) * tk, :] = (acc[:, :hd] / acc[:, hd:]).astype(o_ref.dtype)


def _fox_attn(qh, kh, vh, c, first, *, tk, nsub):
    nb, nh, s, hd = qh.shape
    tq = tk * nsub
    nq = s // tq
    kv_spec = pl.BlockSpec((None, None, s, hd), lambda b, h, i, f: (b, h, 0, 0))
    return pl.pallas_call(
        functools.partial(_fox_attn_kernel, tk=tk, nsub=nsub, hd=hd),
        grid_spec=pltpu.PrefetchScalarGridSpec(
            num_scalar_prefetch=1, grid=(nb, nh, nq),
            in_specs=[pl.BlockSpec((None, None, tq, hd), lambda b, h, i, f: (b, h, i, 0)),
                      kv_spec, kv_spec,
                      pl.BlockSpec((None, s // tk, 1, tk), lambda b, h, i, f: (b * nh + h, 0, 0, 0))],
            out_specs=pl.BlockSpec((None, tq, hd), lambda b, h, i, f: (b, i, h)),
            scratch_shapes=[pltpu.VMEM((nsub, tk, LANES), F32), pltpu.VMEM((nsub, tk, 2 * hd), F32)]),
        out_shape=jax.ShapeDtypeStruct((nb, s, nh * hd), BF16),
        compiler_params=_cp(3), name="fox_attention",
    )(first, qh, kh, vh, c.reshape(nb * nh, s // tk, 1, tk))


def _fox_sample_kernel(q_ref, kc_ref, vc_ref, kn_ref, vn_ref, c_ref, cf_ref, o_ref, *, hb, hd, p_len, t):
    n = p_len * hb
    kf = kc_ref[...].reshape(n, hd).astype(BF16)
    vf = vc_ref[...].reshape(n, hd).astype(BF16)
    heads = [slice(j * hd, (j + 1) * hd) for j in range(hb)]
    q8 = jnp.concatenate([q_ref[:, sl] for sl in heads], axis=0)
    row_head = lax.shift_right_logical(lax.broadcasted_iota(jnp.int32, (hb * t, LANES), 0), t.bit_length() - 1)
    col_head = lax.broadcasted_iota(jnp.int32, (hb * t, LANES), 1) & (hb - 1)
    own = jnp.where(row_head == col_head, 0.0, NEG)
    s = _dot_nt(q8, kf)
    cf = cf_ref[...]
    cols = [s[:, g * LANES:(g + 1) * LANES] - cf[:, g * LANES:(g + 1) * LANES] + own for g in range(n // LANES)]
    tri = lax.broadcasted_iota(jnp.int32, (t, t), 1) <= lax.broadcasted_iota(jnp.int32, (t, t), 0)
    s_n = jnp.concatenate(
        [jnp.where(tri, _dot_nt(q_ref[:, sl], kn_ref[:, sl].astype(BF16)) - c_ref[j:j + 1, p_len:p_len + t], NEG)
         for j, sl in enumerate(heads)], axis=0)
    m = jnp.maximum(jnp.max(functools.reduce(jnp.maximum, cols), axis=-1, keepdims=True),
                    jnp.max(s_n, axis=-1, keepdims=True))
    ps = [jnp.exp2(col - m) for col in cols]
    p_n = jnp.exp2(s_n - m)
    den = jnp.sum(functools.reduce(jnp.add, ps), axis=-1, keepdims=True) + jnp.sum(p_n, axis=-1, keepdims=True)
    o = _dot(jnp.concatenate(ps, axis=1).astype(BF16), vf)
    o_n = jnp.concatenate([_dot(p_n[j * t:(j + 1) * t].astype(BF16), vn_ref[:, sl].astype(BF16))
                           for j, sl in enumerate(heads)], axis=0)
    o = (o + o_n) / den
    for j, sl in enumerate(heads):
        o_ref[:, sl] = o[j * t:(j + 1) * t].astype(o_ref.dtype)


def _fox_sample_attn(q, k_cache, v_cache, k_new, v_new, c_all, *, layer, t):
    _, nb, p_len, nh, hd = k_cache.shape
    hb = min(8, nh)
    assert t & (t - 1) == 0 and hb & (hb - 1) == 0 and (p_len * hb) % LANES == 0 and LANES % hb == 0
    w = hb * hd
    lc = c_all.shape[-1]
    c_flat = c_all[:, :, :p_len].reshape(nb, nh // hb, hb, p_len).transpose(0, 1, 3, 2)
    c_flat = c_flat.reshape(nb, nh // hb, 1, p_len * hb)
    cache_spec = pl.BlockSpec((None, None, p_len, hb, hd), lambda b, h: (layer, b, 0, h, 0))
    row_spec = pl.BlockSpec((t, w), lambda b, h: (b, h))
    return pl.pallas_call(
        functools.partial(_fox_sample_kernel, hb=hb, hd=hd, p_len=p_len, t=t),
        grid=(nb, nh // hb),
        in_specs=[row_spec, cache_spec, cache_spec, row_spec, row_spec,
                  pl.BlockSpec((None, hb, lc), lambda b, h: (b, h, 0)),
                  pl.BlockSpec((None, None, 1, p_len * hb), lambda b, h: (b, h, 0, 0))],
        out_specs=row_spec,
        out_shape=jax.ShapeDtypeStruct((nb * t, nh * hd), BF16),
        compiler_params=_cp(2), name="fox_sample_attention",
    )(q, k_cache, v_cache, k_new, v_new, c_all, c_flat)


def _proj_res_kernel(a_ref, w_ref, x_ref, gate_ref, o_ref):
    o_ref[...] = x_ref[...] + gate_ref[...] * _dot(a_ref[...], w_ref[...])


def _proj_res(a, w, x, gate, *, layer, tm, tpb):
    m, k = a.shape
    n = w.shape[2]
    return pl.pallas_call(
        _proj_res_kernel, grid=(m // tm, 1),
        in_specs=[pl.BlockSpec((tm, k), lambda i, j: (i, 0)),
                  pl.BlockSpec((None, k, n), lambda i, j: (layer, 0, 0)),
                  pl.BlockSpec((tm, n), lambda i, j: (i, 0)),
                  _mod_spec(gate, tpb)],
        out_specs=pl.BlockSpec((tm, n), lambda i, j: (i, 0)),
        out_shape=jax.ShapeDtypeStruct((m, n), F32),
        compiler_params=_cp(2), name="attn_out_proj",
    )(a, w, x, gate)


def _ffn_kernel(x_ref, g_ref, sh_ref, sc_ref, gate_ref, wu_ref, wd_ref, *rest, final):
    if final:
        fg_ref, o_ref, h_ref, acc_ref = rest
    else:
        o_ref, h_ref, acc_ref = rest
    f = pl.program_id(1)

    @pl.when(f == 0)
    def _():
        _norm_mod_store(h_ref, x_ref, g_ref, sh_ref, sc_ref)
        acc_ref[...] = jnp.zeros(acc_ref.shape, F32)

    a = jnp.maximum(_dot(h_ref[...], wu_ref[...]), 0.0)
    acc_ref[...] += _dot((a * a).astype(BF16), wd_ref[...])

    @pl.when(f == pl.num_programs(1) - 1)
    def _():
        y = x_ref[...] + gate_ref[...] * acc_ref[...]
        if final:
            y = _rms(y) * fg_ref[...]
        o_ref[...] = y


def _ffn(x, g, shift, scale, gate, w_up, w_down, final_g, *, layer, tm, tpb):
    m, d = x.shape
    ff = w_up.shape[2]
    tf = min(ff, FFN_STEP_ELEMS // tm)
    vec = pl.BlockSpec((1, d), lambda i, f: (0, 0))
    in_specs = [pl.BlockSpec((tm, d), lambda i, f: (i, 0)), vec,
                _mod_spec(shift, tpb), _mod_spec(scale, tpb), _mod_spec(gate, tpb),
                pl.BlockSpec((None, d, tf), lambda i, f: (layer, 0, f)),
                pl.BlockSpec((None, tf, d), lambda i, f: (layer, f, 0))]
    args = [x, g, shift, scale, gate, w_up, w_down]
    if final_g is not None:
        in_specs.append(vec)
        args.append(final_g)
    return pl.pallas_call(
        functools.partial(_ffn_kernel, final=final_g is not None),
        grid=(m // tm, ff // tf), in_specs=in_specs,
        out_specs=pl.BlockSpec((tm, d), lambda i, f: (i, 0)),
        out_shape=jax.ShapeDtypeStruct((m, d), F32),
        scratch_shapes=[pltpu.VMEM((tm, d), BF16), pltpu.VMEM((tm, d), F32)],
        compiler_params=_cp(2), name="ffn_final" if final_g is not None else "ffn",
    )(*args)


def _rope(r, a, b, c):
    half_shift = (LANES // 8) // 2
    cols = []
    for j in range(r.shape[1] // LANES):
        x = r[:, j * LANES:(j + 1) * LANES]
        cols.append(x * a + pltpu.roll(x, LANES - half_shift, 1) * b + pltpu.roll(x, half_shift, 1) * c)
    return jnp.concatenate(cols, axis=1)


def _dup_halves(x, hd):
    lo = lax.broadcasted_iota(jnp.int32, (x.shape[0], LANES), 1) < hd
    zero = jnp.zeros((x.shape[0], LANES), x.dtype)
    out = []
    for j in range(x.shape[1] // LANES):
        p = x[:, j * LANES:(j + 1) * LANES]
        r = pltpu.roll(p, hd, 1)
        out += [jnp.where(lo, p, zero), jnp.where(lo, zero, r), jnp.where(lo, r, zero), jnp.where(lo, zero, p)]
    return jnp.concatenate(out, axis=1)


def _swa_in_kernel(x_ref, g_ref, sh_ref, sc_ref, w_ref, ra_ref, rb_ref, rc_ref,
                   q_ref, kk_ref, vv_ref, k_ref, v_ref, h_ref, *, kvd, hd, q_scale):
    _norm_mod_store(h_ref, x_ref, g_ref, sh_ref, sc_ref)
    h = h_ref[...]
    tabs = (ra_ref[...], rb_ref[...], rc_ref[...])
    d = q_ref.shape[1]
    tn = 2 * kvd
    for n in range(d // tn):
        r = _dot(h, w_ref[:, n * tn:(n + 1) * tn])
        q_ref[:, n * tn:(n + 1) * tn] = (_rope(r, *tabs) * q_scale).astype(BF16)
    r = _dot(h, w_ref[:, d:])
    k = _rope(r[:, :kvd], *tabs)
    v = r[:, kvd:]
    k_ref[...] = k
    v_ref[...] = v
    kk_ref[...] = _dup_halves(k, hd).astype(BF16)
    vv_ref[...] = _dup_halves(v, hd).astype(BF16)


def _swa_in(x, g, shift, scale, w, tabs, *, layer, tm, tpb, kvd, hd):
    m, d = x.shape
    ntab = tabs[0].shape[0] // tm
    kkw = (kvd // hd) * 2 * LANES
    tab_spec = pl.BlockSpec((tm, LANES), lambda i, n: (i % ntab, 0))
    const = lambda w_: pl.BlockSpec((tm, w_), lambda i, n: (i, 0))
    return pl.pallas_call(
        functools.partial(_swa_in_kernel, kvd=kvd, hd=hd, q_scale=hd ** -0.5 * LOG2E),
        grid=(m // tm, 1),
        in_specs=[pl.BlockSpec((tm, d), lambda i, n: (i, 0)),
                  pl.BlockSpec((1, d), lambda i, n: (0, 0)),
                  _mod_spec(shift, tpb), _mod_spec(scale, tpb),
                  pl.BlockSpec((None, d, d + 2 * kvd), lambda i, n: (layer, 0, 0)),
                  tab_spec, tab_spec, tab_spec],
        out_specs=[const(d), const(kkw), const(kkw), const(kvd), const(kvd)],
        out_shape=[jax.ShapeDtypeStruct((m, d), BF16),
                   jax.ShapeDtypeStruct((m, kkw), BF16), jax.ShapeDtypeStruct((m, kkw), BF16),
                   jax.ShapeDtypeStruct((m, kvd), F32), jax.ShapeDtypeStruct((m, kvd), F32)],
        scratch_shapes=[pltpu.VMEM((tm, d), BF16)],
        compiler_params=_cp(2), name="swa_in_proj",
    )(x, g, shift, scale, w, *tabs)


def _rope_tables(pos, hd, rope_dims):
    half = rope_dims // 2
    inv_freq = ROPE_THETA ** (-jnp.arange(half, dtype=F32) * 2.0 / rope_dims)
    ang = pos.astype(F32)[:, None] * inv_freq[None, :]
    cos, sin = jnp.cos(ang), jnp.sin(ang)
    lane = np.arange(LANES) % hd
    idx = lane % half
    a = jnp.where(lane < rope_dims, cos[:, idx], 1.0)
    b = jnp.where(lane < half, -sin[:, idx], 0.0)
    c = jnp.where((lane >= half) & (lane < rope_dims), sin[:, idx], 0.0)
    return a, b, c


def _swa_core(q, k_top, k_bot, v_top, v_bot, bias, sinks, *, group):
    t = q.shape[0]
    npair = group // 2
    win = k_top.shape[0]
    half_lanes = lax.broadcasted_iota(jnp.int32, (win, LANES), 1) < LANES // 2
    count_even = jnp.where(half_lanes, 1.0, 0.0).astype(BF16)
    count_odd = jnp.where(half_lanes, 0.0, 1.0).astype(BF16)
    v2 = jnp.concatenate([jnp.concatenate([v_top, count_even], axis=1),
                          jnp.concatenate([v_bot, count_odd], axis=1)], axis=0)
    qs = jnp.concatenate([q[:, p * LANES:(p + 1) * LANES] for p in range(npair)], axis=0)
    s = _dot_nt(qs, jnp.concatenate([k_top, k_bot], axis=0)) + bias
    ps, es = [], []
    for half in range(2):
        cols = [s[:, half * win + c * LANES: half * win + (c + 1) * LANES] for c in range(win // LANES)]
        sink = sinks[:, half * LANES:(half + 1) * LANES]
        m = jnp.maximum(jnp.max(functools.reduce(jnp.maximum, cols), axis=-1, keepdims=True), sink)
        ps += [jnp.exp2(col - m) for col in cols]
        es.append(jnp.exp2(sink - m))
    o = _dot(jnp.concatenate(ps, axis=1).astype(BF16), v2)
    lane = lax.broadcasted_iota(jnp.int32, (o.shape[0], LANES), 1)
    o = o[:, :LANES] / (o[:, LANES:] + jnp.where(lane < LANES // 2, es[0], es[1]))
    return jnp.concatenate([o[p * t:(p + 1) * t] for p in range(npair)], axis=1)


def _swa_attn_kernel(q_ref, kh_ref, km_ref, vh_ref, vm_ref, bias_ref, sink_ref, o_ref, *, kv, group, hd, ns):
    t = 2 * CHUNK
    gw = group * hd
    first_bias = bias_ref[jnp.minimum(pl.program_id(1), 1)]
    for u in range(ns):
        rows = slice(u * t, (u + 1) * t)
        for g in range(kv):
            def win(halo, main, off):
                sl = slice(g * 2 * LANES + off, g * 2 * LANES + off + LANES)
                before = halo[:, sl] if u == 0 else main[(u - 1) * t:u * t, sl]
                return jnp.concatenate([before, main[rows, sl]], axis=0)
            o = _swa_core(q_ref[rows, g * gw:(g + 1) * gw],
                          win(kh_ref, km_ref, 0), win(kh_ref, km_ref, LANES),
                          win(vh_ref, vm_ref, 0), win(vh_ref, vm_ref, LANES),
                          first_bias if u == 0 else bias_ref[1], sink_ref[g], group=group)
            o_ref[rows, g * gw:(g + 1) * gw] = o.astype(o_ref.dtype)


def _swa_attn(q, kk, vv, bias2, sink_rows, *, nb, s, kv, group, hd):
    t = 2 * CHUNK
    ns = SWA_TILES_PER_STEP if (s // t) % SWA_TILES_PER_STEP == 0 else 1
    nstep = s // (ns * t)
    d = q.shape[1]
    kkw = kk.shape[1]
    main = pl.BlockSpec((ns * t, kkw), lambda b, i: (b * nstep + i, 0))
    halo = pl.BlockSpec((t, kkw), lambda b, i: (b * nstep * ns + jnp.maximum(i * ns - 1, 0), 0))
    full = lambda a: pl.BlockSpec(a.shape, lambda b, i: (0,) * a.ndim)
    return pl.pallas_call(
        functools.partial(_swa_attn_kernel, kv=kv, group=group, hd=hd, ns=ns),
        grid=(nb, nstep),
        in_specs=[pl.BlockSpec((ns * t, d), lambda b, i: (b * nstep + i, 0)),
                  halo, main, halo, main, full(bias2), full(sink_rows)],
        out_specs=pl.BlockSpec((ns * t, d), lambda b, i: (b * nstep + i, 0)),
        out_shape=jax.ShapeDtypeStruct(q.shape, BF16),
        compiler_params=_cp(2), name="swa_attention",
    )(q, kk, kk, vv, vv, bias2, sink_rows)


def _swa_sample_kernel(q_ref, kc_ref, vc_ref, kkn_ref, vvn_ref, bias_ref, sink_ref, o_ref, *, kv, group, hd, t):
    gw = group * hd
    kkc = _dup_halves(kc_ref[...], hd).astype(BF16)
    vvc = _dup_halves(vc_ref[...], hd).astype(BF16)
    pad = jnp.zeros((kc_ref.shape[0] - t, LANES), BF16)
    for g in range(kv):
        def win(cache, new, off):
            sl = slice(g * 2 * LANES + off, g * 2 * LANES + off + LANES)
            return jnp.concatenate([cache[:, sl], new[:, sl], pad], axis=0)
        o = _swa_core(q_ref[:, g * gw:(g + 1) * gw],
                      win(kkc, kkn_ref, 0), win(kkc, kkn_ref, LANES),
                      win(vvc, vvn_ref, 0), win(vvc, vvn_ref, LANES),
                      bias_ref[...], sink_ref[g], group=group)
        o_ref[:, g * gw:(g + 1) * gw] = o.astype(o_ref.dtype)


def _swa_sample_attn(q, k_cache, v_cache, kk_new, vv_new, bias, sink_rows, *, kv, group, hd, t):
    nb, buf, kvd = k_cache.shape
    d = q.shape[1]
    full = lambda a: pl.BlockSpec(a.shape, lambda b: (0,) * a.ndim)
    cache_spec = pl.BlockSpec((None, buf, kvd), lambda b: (b, 0, 0))
    new_spec = pl.BlockSpec((t, kk_new.shape[1]), lambda b: (b, 0))
    return pl.pallas_call(
        functools.partial(_swa_sample_kernel, kv=kv, group=group, hd=hd, t=t),
        grid=(nb,),
        in_specs=[pl.BlockSpec((t, d), lambda b: (b, 0)), cache_spec, cache_spec, new_spec, new_spec,
                  full(bias), full(sink_rows)],
        out_specs=pl.BlockSpec((t, d), lambda b: (b, 0)),
        out_shape=jax.ShapeDtypeStruct(q.shape, BF16),
        compiler_params=_cp(1), name="swa_sample_attention",
    )(q, k_cache, v_cache, kk_new, vv_new, bias, sink_rows)


def _window_bias(valid, npair):
    b = np.where(valid, 0.0, NEG).astype(np.float32)
    return np.tile(b, (npair, 2))


def _prompt_bias(npair):
    t = 2 * CHUNK
    qc = np.arange(t)[:, None] // CHUNK
    kc = np.arange(t + SWA_WINDOW_CHUNKS * CHUNK)[None, :] // CHUNK
    valid = (kc >= qc) & (kc <= qc + SWA_WINDOW_CHUNKS)
    first = valid & (kc >= SWA_WINDOW_CHUNKS)
    return jnp.asarray(np.stack([_window_bias(first, npair), _window_bias(valid, npair)]))


def _sample_bias(past_len, buf, t, npair):
    q_pos = past_len + np.arange(t)
    k_pos = np.concatenate([past_len - buf + np.arange(buf), q_pos])
    qch, kch = q_pos // CHUNK, k_pos // CHUNK
    valid = np.zeros((t, 2 * buf), bool)
    valid[:, :buf + t] = (kch[None, :] <= qch[:, None]) & (kch[None, :] >= qch[:, None] - SWA_WINDOW_CHUNKS)
    return jnp.asarray(_window_bias(valid, npair))


def _sink_rows(sinks, kv, group, t):
    s = (sinks * LOG2E).reshape(kv, group // 2, 1, 2, 1)
    return jnp.broadcast_to(s, (kv, group // 2, t, 2, LANES)).reshape(kv, (group // 2) * t, 2 * LANES)


def kernel(x_prompt, x_sample, c_prompt, c_sample, cache_fox_k, cache_fox_v, cache_fox_logf, cache_swa_k,
           cache_swa_v, ada_w, ada_b, norm_mix_g, norm_ffn_g, fox_w_in, fox_b_f, fox_w_out, swa_w_in,
           swa_sinks, swa_w_out, ffn_w_up, ffn_w_down, final_g):
    bp, s, d = x_prompt.shape
    bs, t, _ = x_sample.shape
    depth = ada_w.shape[0]
    past_len = cache_fox_k.shape[2]
    nh_fox, hd_fox = cache_fox_k.shape[3], cache_fox_k.shape[4]
    buf, kv, hd_swa = cache_swa_k.shape[2], cache_swa_k.shape[3], cache_swa_k.shape[4]
    nh_swa = swa_sinks.shape[1]
    group = nh_swa // kv
    kvd = kv * hd_swa
    rope_dims = hd_swa // 4
    assert hd_fox == LANES and hd_swa == LANES // 2 and rope_dims == 16 and group % 2 == 0
    assert buf == SWA_WINDOW_CHUNKS * CHUNK and t <= buf and s % (2 * CHUNK) == 0

    mp, ms = bp * s, bs * t
    tm_p = min(512, s)
    tpb_p = s // tm_p
    tm_fox = min(1024, s)
    tk = min(FOX_TK, s)
    tq = min(FOX_NSUB * tk, s)

    mods = _ada(jnp.concatenate([c_prompt, c_sample], axis=0), ada_w, ada_b)

    def split_mods(i):
        six = jnp.split(mods[i], 6, axis=-1)
        prompt = [m[:bp, None, :] for m in six]
        sample = [jnp.repeat(m[bp:], t, axis=0)[None] for m in six]
        return prompt, sample

    row = lambda v: v.reshape(1, -1)
    xp = x_prompt.reshape(mp, d)
    xs = x_sample.reshape(ms, d)
    pos_p = jnp.arange(s)
    pos_s = past_len + jnp.arange(t)
    tabs_p = _rope_tables(pos_p, hd_swa, rope_dims)
    tabs_s = tuple(jnp.tile(a, (bs, 1)) for a in _rope_tables(pos_s, hd_swa, rope_dims))
    bias_p = _prompt_bias(group // 2)
    bias_s = _sample_bias(past_len, buf, t, group // 2)

    fox_w_in_b, fox_w_out_b, swa_w_in_b, swa_w_out_b = _to_bf16(fox_w_in, fox_w_out, swa_w_in, swa_w_out)
    (w_up_b,), (w_down_b,) = _to_bf16(ffn_w_up), _to_bf16(ffn_w_down)

    outs = {k: [] for k in ("fkp", "fvp", "flp", "fks", "fvs", "fls", "skp", "svp", "sks", "svs")}
    for i in range(depth):
        mod_p, mod_s = split_mods(i)
        j = i // 2
        if i % 2 == 0:
            w_f = jnp.pad(fox_w_in_b[j, :, 3 * d:], ((0, 0), (0, LANES - nh_fox)))
            b_f = row(fox_b_f[j])
            qh, kh, vh, k32, v32, lf, n2 = _fox_in(
                xp, row(norm_mix_g[i]), mod_p[0], mod_p[1], fox_w_in_b, w_f, b_f,
                layer=j, tm=tm_fox, tpb=s // tm_fox, head_major=True, nh=nh_fox, hd=hd_fox)
            lft = lf.reshape(bp, s, nh_fox).transpose(0, 2, 1)
            c = _cumsum_rows(lft.reshape(-1, LANES), s // LANES).reshape(bp * nh_fox, s)
            qmax, kmax = _head_norm_bounds(n2, bp, nh_fox, min(FOX_IN_COLS, d) // hd_fox)
            first = _fox_first_blocks(c, qmax, kmax, tk)
            op = _fox_attn(qh, kh, vh, c, first, tk=tk, nsub=tq // tk)
            xp = _proj_res(op.reshape(mp, d), fox_w_out_b, xp, mod_p[2], layer=j, tm=tm_p, tpb=tpb_p)
            outs["fkp"].append(k32.reshape(bp, s, nh_fox, hd_fox))
            outs["fvp"].append(v32.reshape(bp, s, nh_fox, hd_fox))
            outs["flp"].append(lf.reshape(bp, s, nh_fox))
            qs_, k32s, v32s, lfs = _fox_in(xs, row(norm_mix_g[i]), mod_s[0], mod_s[1], fox_w_in_b, w_f, b_f,
                                           layer=j, tm=ms, tpb=1, head_major=False, nh=nh_fox, hd=hd_fox)
            lf_all = jnp.concatenate([cache_fox_logf[j], lfs.reshape(bs, t, nh_fox)], axis=1)
            lc = LANES * int(2 ** np.ceil(np.log2(-(-(past_len + t) // LANES))))
            lf_all = jnp.pad(lf_all.transpose(0, 2, 1), ((0, 0), (0, 0), (0, lc - past_len - t)))
            c_s = _cumsum_rows(lf_all.reshape(-1, LANES), lc // LANES).reshape(bs, nh_fox, lc)
            os_ = _fox_sample_attn(qs_, cache_fox_k, cache_fox_v, k32s, v32s, c_s, layer=j, t=t)
            xs = _proj_res(os_, fox_w_out_b, xs, mod_s[2], layer=j, tm=ms, tpb=1)
            outs["fks"].append(k32s.reshape(bs, t, nh_fox, hd_fox))
            outs["fvs"].append(v32s.reshape(bs, t, nh_fox, hd_fox))
            outs["fls"].append(lfs.reshape(bs, t, nh_fox))
        else:
            q, kk, vv, k32, v32 = _swa_in(xp, row(norm_mix_g[i]), mod_p[0], mod_p[1], swa_w_in_b, tabs_p,
                                          layer=j, tm=tm_p, tpb=tpb_p, kvd=kvd, hd=hd_swa)
            op = _swa_attn(q, kk, vv, bias_p, _sink_rows(swa_sinks[j], kv, group, 2 * CHUNK),
                           nb=bp, s=s, kv=kv, group=group, hd=hd_swa)
            xp = _proj_res(op, swa_w_out_b, xp, mod_p[2], layer=j, tm=tm_p, tpb=tpb_p)
            outs["skp"].append(k32.reshape(bp, s, kvd)[:, -buf:].reshape(bp, buf, kv, hd_swa))
            outs["svp"].append(v32.reshape(bp, s, kvd)[:, -buf:].reshape(bp, buf, kv, hd_swa))
            q, kk, vv, k32, v32 = _swa_in(xs, row(norm_mix_g[i]), mod_s[0], mod_s[1], swa_w_in_b, tabs_s,
                                          layer=j, tm=ms, tpb=1, kvd=kvd, hd=hd_swa)
            os_ = _swa_sample_attn(q, cache_swa_k[j].reshape(bs, buf, kvd), cache_swa_v[j].reshape(bs, buf, kvd),
                                   kk, vv, bias_s, _sink_rows(swa_sinks[j], kv, group, t),
                                   kv=kv, group=group, hd=hd_swa, t=t)
            xs = _proj_res(os_, swa_w_out_b, xs, mod_s[2], layer=j, tm=ms, tpb=1)
            k_all = jnp.concatenate([cache_swa_k[j], k32.reshape(bs, t, kv, hd_swa)], axis=1)
            v_all = jnp.concatenate([cache_swa_v[j], v32.reshape(bs, t, kv, hd_swa)], axis=1)
            outs["sks"].append(k_all[:, -buf:])
            outs["svs"].append(v_all[:, -buf:])
        fg = row(final_g) if i == depth - 1 else None
        xp = _ffn(xp, row(norm_ffn_g[i]), mod_p[3], mod_p[4], mod_p[5], w_up_b, w_down_b, fg,
                  layer=i, tm=tm_p, tpb=tpb_p)
        xs = _ffn(xs, row(norm_ffn_g[i]), mod_s[3], mod_s[4], mod_s[5], w_up_b, w_down_b, fg,
                  layer=i, tm=ms, tpb=1)

    st = lambda k: jnp.stack(outs[k])
    return (xp.reshape(bp, s, d), xs.reshape(bs, t, d),
            st("fkp"), st("fvp"), st("flp"), st("fks"), st("fvs"), st("fls"),
            st("skp"), st("svp"), st("sks"), st("svs"))
```

```python
import functools

import numpy as np
import jax
import jax.numpy as jnp
from jax import lax
from jax.experimental import pallas as pl
from jax.experimental.pallas import tpu as pltpu

F32 = jnp.float32
BF16 = jnp.bfloat16

RMS_EPS = 1e-6
CHUNK = 64
SWA_WINDOW_CHUNKS = 2
ROPE_THETA = 500000.0
LANES = 128
SUBLANES = 8
NEG = -1e30
NEG_BLOCK = -3e38
VMEM_LIMIT_BYTES = 56 * 1024 * 1024
LOG2E = 1.4426950408889634
FOX_SKIP_LOG2 = 136.0
FFN_STEP_ELEMS = 512 * 1024
SWA_TILES_PER_STEP = 8
FOX_IN_COLS = 512
FOX_TK = 512
FOX_NSUB = 16


def _cp(n_axes):
    return pltpu.CompilerParams(dimension_semantics=("arbitrary",) * n_axes,
                                vmem_limit_bytes=VMEM_LIMIT_BYTES)


def _dot(a, b):
    return jnp.dot(a, b, preferred_element_type=F32)


def _dot_nt(a, b):
    return lax.dot_general(a, b, (((1,), (1,)), ((), ())), preferred_element_type=F32)


def _rms(x):
    return x * lax.rsqrt(jnp.mean(x * x, axis=-1, keepdims=True) + RMS_EPS)


NORM_ROWS = 32


def _norm_mod_store(h_ref, x_ref, g_ref, sh_ref, sc_ref):
    per_token = sh_ref.shape[0] > 1
    g = g_ref[...]
    if not per_token:
        gain, shift = g * (1.0 + sc_ref[...]), sh_ref[...]

    def body(i, carry):
        rows = pl.ds(pl.multiple_of(i * NORM_ROWS, NORM_ROWS), NORM_ROWS)
        x = x_ref[rows, :]
        if per_token:
            gain_, shift_ = g * (1.0 + sc_ref[rows, :]), sh_ref[rows, :]
        else:
            gain_, shift_ = gain, shift
        h_ref[rows, :] = (_rms(x) * gain_ + shift_).astype(h_ref.dtype)
        return carry

    lax.fori_loop(0, x_ref.shape[0] // NORM_ROWS, body, 0, unroll=4)


def _log_sigmoid(z):
    return jnp.minimum(z, 0.0) - jnp.log1p(jnp.exp(-jnp.abs(z)))


CAST_BLOCK_BYTES = 4 * 1024 * 1024


def _cast_kernel(*refs):
    n = len(refs) // 2
    for x_ref, o_ref in zip(refs[:n], refs[n:]):
        o_ref[...] = x_ref[...].astype(o_ref.dtype)


def _to_bf16(*ws):
    steps = 1
    while any(w.size * 4 // steps > CAST_BLOCK_BYTES for w in ws):
        steps *= 2
    specs = []
    for w in ws:
        layers, rows, cols = w.shape
        per_layer = steps // layers
        assert per_layer >= 1 and rows % (16 * per_layer) == 0
        specs.append(pl.BlockSpec((None, rows // per_layer, cols),
                                  lambda i, per_layer=per_layer: (i // per_layer, i % per_layer, 0)))
    return pl.pallas_call(
        _cast_kernel, grid=(steps,), in_specs=specs, out_specs=specs,
        out_shape=[jax.ShapeDtypeStruct(w.shape, BF16) for w in ws],
        compiler_params=_cp(1), name="weight_to_bf16",
    )(*ws)


def _ada_kernel(c_ref, w_ref, b_ref, o_ref):
    c = c_ref[...]
    a = (c / (1.0 + jnp.exp(-c))).astype(BF16)
    o_ref[...] = _dot(a, w_ref[...].astype(BF16)) + b_ref[...]


def _ada(c_all, ada_w, ada_b):
    depth, d, n6 = ada_w.shape
    r = c_all.shape[0]
    tn = min(1024, n6)
    return pl.pallas_call(
        _ada_kernel,
        grid=(depth, n6 // tn),
        in_specs=[pl.BlockSpec((r, d), lambda l, n: (0, 0)),
                  pl.BlockSpec((None, d, tn), lambda l, n: (l, 0, n)),
                  pl.BlockSpec((None, 1, tn), lambda l, n: (l, 0, n))],
        out_specs=pl.BlockSpec((None, r, tn), lambda l, n: (l, 0, n)),
        out_shape=jax.ShapeDtypeStruct((depth, r, n6), F32),
        compiler_params=_cp(2), name="ada_params",
    )(c_all, ada_w, ada_b.reshape(depth, 1, n6))


def _fox_in_kernel(x_ref, g_ref, sh_ref, sc_ref, wq_ref, wk_ref, wv_ref, wf_ref, bf_ref, *rest,
                   head_major, hd, nh, q_scale):
    if head_major:
        qb_ref, kb_ref, vb_ref, k_ref, v_ref, lf_ref, n2_ref, h_ref = rest
    else:
        qb_ref, k_ref, v_ref, lf_ref, h_ref = rest

    @pl.when(pl.program_id(1) == 0)
    def _():
        _norm_mod_store(h_ref, x_ref, g_ref, sh_ref, sc_ref)
        z = _dot(h_ref[...], wf_ref[...])[:, :nh] + bf_ref[...]
        lf_ref[...] = _log_sigmoid(z)

    hb = h_ref[...]
    q = _dot(hb, wq_ref[...]) * q_scale
    k = _dot(hb, wk_ref[...])
    v = _dot(hb, wv_ref[...])
    k_ref[...] = k
    v_ref[...] = v
    if head_major:
        for j in range(q.shape[1] // hd):
            sl = slice(j * hd, (j + 1) * hd)
            qb_ref[j] = q[:, sl].astype(BF16)
            kb_ref[j] = k[:, sl].astype(BF16)
            vb_ref[j] = v[:, sl].astype(BF16)
        tm, tn = q.shape
        grp = lax.shift_right_logical(lax.broadcasted_iota(jnp.int32, (2 * tn, LANES), 0), hd.bit_length() - 1)
        sel = jnp.where(grp == lax.broadcasted_iota(jnp.int32, (2 * tn, LANES), 1), 1.0, 0.0).astype(BF16)
        n2 = _dot(jnp.concatenate([q * q, k * k], axis=1).astype(BF16), sel)
        n2_ref[...] = jnp.max(n2.reshape(tm // SUBLANES, SUBLANES, LANES), axis=0)
    else:
        qb_ref[...] = q.astype(BF16)


def _mod_spec(mod, tpb, cols=None):
    _, rows, d = mod.shape
    if cols is None:
        return pl.BlockSpec((None, rows, d), lambda i, n: (i // tpb, 0, 0))
    return pl.BlockSpec((None, rows, cols), lambda i, n: (i // tpb, 0, n))


def _fox_in(x, g, shift, scale, w_in, w_f, b_f, *, layer, tm, tpb, head_major, nh, hd):
    m, d = x.shape
    tn = min(FOX_IN_COLS, d)
    nt = d // tn
    hpt = tn // hd
    nb, s = m // (tm * tpb), tm * tpb
    in_specs = [pl.BlockSpec((tm, d), lambda i, n: (i, 0)),
                pl.BlockSpec((1, d), lambda i, n: (0, 0)),
                _mod_spec(shift, tpb), _mod_spec(scale, tpb),
                pl.BlockSpec((None, d, tn), lambda i, n: (layer, 0, n)),
                pl.BlockSpec((None, d, tn), lambda i, n: (layer, 0, n + nt)),
                pl.BlockSpec((None, d, tn), lambda i, n: (layer, 0, n + 2 * nt)),
                pl.BlockSpec((d, LANES), lambda i, n: (0, 0)),
                pl.BlockSpec((1, nh), lambda i, n: (0, 0))]
    rm_spec = pl.BlockSpec((tm, tn), lambda i, n: (i, n))
    lf_spec = pl.BlockSpec((tm, nh), lambda i, n: (i, 0))
    if head_major:
        hm_spec = pl.BlockSpec((None, hpt, tm, hd), lambda i, n: (i // tpb, n, i % tpb, 0))
        hm_shape = jax.ShapeDtypeStruct((nb, nh, s, hd), BF16)
        assert hd & (hd - 1) == 0 and 2 * hpt <= LANES
        n2_spec = pl.BlockSpec((SUBLANES, LANES), lambda i, n: (i, n))
        out_specs = [hm_spec, hm_spec, hm_spec, rm_spec, rm_spec, lf_spec, n2_spec]
        out_shape = [hm_shape, hm_shape, hm_shape]
    else:
        out_specs = [rm_spec, rm_spec, rm_spec, lf_spec]
        out_shape = [jax.ShapeDtypeStruct((m, d), BF16)]
    out_shape += [jax.ShapeDtypeStruct((m, d), F32), jax.ShapeDtypeStruct((m, d), F32),
                  jax.ShapeDtypeStruct((m, nh), F32)]
    if head_major:
        out_shape += [jax.ShapeDtypeStruct((m // tm * SUBLANES, nt * LANES), F32)]
    return pl.pallas_call(
        functools.partial(_fox_in_kernel, head_major=head_major, hd=hd, nh=nh, q_scale=hd ** -0.5 * LOG2E),
        grid=(m // tm, nt), in_specs=in_specs, out_specs=out_specs, out_shape=out_shape,
        scratch_shapes=[pltpu.VMEM((tm, d), BF16)],
        compiler_params=_cp(2), name="fox_in_proj",
    )(x, g, shift, scale, w_in, w_in, w_in, w_f, b_f)


def _split3(x):
    x1 = x.astype(BF16)
    r1 = x - x1.astype(F32)
    x2 = r1.astype(BF16)
    x3 = (r1 - x2.astype(F32)).astype(BF16)
    return x1, x2, x3


CUMSUM_BLOCK_ROWS = 1024


def _cumsum_kernel(x_ref, u_ref, l_ref, o_ref):
    u = u_ref[...]
    lo = l_ref[...]
    y = sum(_dot(p, u) for p in _split3(x_ref[...]))
    tot = jnp.broadcast_to(y[:, LANES - 1:LANES], y.shape)
    o_ref[...] = (y + sum(_dot(lo, p) for p in _split3(tot))) * LOG2E


def _cumsum_rows(x, nc):
    rows = x.shape[0]
    assert LANES % nc == 0 or nc % LANES == 0
    n = max(nc, LANES)
    assert rows % n == 0
    while 2 * n <= CUMSUM_BLOCK_ROWS and rows % (2 * n) == 0:
        n *= 2
    u = jnp.asarray(np.triu(np.ones((LANES, LANES), np.float32)), BF16)
    idx = np.arange(n)
    same = (idx[:, None] // nc) == (idx[None, :] // nc)
    lo = jnp.asarray((same & (idx[None, :] < idx[:, None])).astype(np.float32), BF16)
    return pl.pallas_call(
        _cumsum_kernel, grid=(rows // n,),
        in_specs=[pl.BlockSpec((n, LANES), lambda i: (i, 0)),
                  pl.BlockSpec((LANES, LANES), lambda i: (0, 0)),
                  pl.BlockSpec((n, n), lambda i: (0, 0))],
        out_specs=pl.BlockSpec((n, LANES), lambda i: (i, 0)),
        out_shape=jax.ShapeDtypeStruct(x.shape, F32),
        compiler_params=_cp(1), name="logf_cumsum",
    )(x, u, lo)


NORM_SLACK = 1.01


def _head_norm_bounds(n2, nb, nh, hpt):
    n2 = jnp.max(n2.reshape(nb, -1, nh // hpt, LANES), axis=1)
    bound = lambda part: (NORM_SLACK * jnp.sqrt(part)).reshape(-1)
    return bound(n2[:, :, :hpt]), bound(n2[:, :, hpt:2 * hpt])


def _fox_first_blocks(c, qmax, kmax, tk):
    bh, s = c.shape
    cb = c.reshape(bh, s // tk, tk)
    c_start, c_end = cb[:, :, 0], cb[:, :, -1]
    budget = (2.0 * qmax * kmax + FOX_SKIP_LOG2)[:, None, None]
    skippable = (c_end[:, None, :] - c_start[:, :, None]) > budget
    earlier = np.tril(np.ones((s // tk, s // tk), bool), -1)
    return jnp.sum(skippable & earlier, axis=-1).astype(jnp.int32).reshape(-1)


def _fox_attn_kernel(first_ref, q_ref, k_ref, v_ref, c_ref, o_ref, m_ref, acc_ref, *, tk, nsub, hd):
    b, h, qi = pl.program_id(0), pl.program_id(1), pl.program_id(2)
    nkb = pl.num_programs(2) * nsub
    base = qi * nsub
    m_ref[...] = jnp.full(m_ref.shape, NEG, F32)
    acc_ref[...] = jnp.zeros(acc_ref.shape, F32)
    ones = jnp.ones((tk, hd), BF16)
    ncol = tk // LANES
    c_first = [c_ref[base + r][:, 0:1] for r in range(nsub)]

    def step(r, j, mask=None, valid=None):
        rows = pl.ds(pl.multiple_of(j * tk, tk), tk)
        c_row = c_ref[j]
        if valid is not None:
            c_row = jnp.where(valid, c_row, -NEG_BLOCK)
        s = _dot_nt(q_ref[r * tk:(r + 1) * tk, :], k_ref[rows, :]) + (c_first[r] - c_row)
        if mask is not None:
            s = jnp.where(mask, s, NEG)
        cols = [s[:, c * LANES:(c + 1) * LANES] for c in range(ncol)]
        m_cur = functools.reduce(jnp.maximum, cols)
        m_prev = m_ref[r]
        m_next = jnp.maximum(m_prev, jnp.max(m_cur, axis=-1, keepdims=True))
        alpha = jnp.exp2(m_prev - m_next)
        p = jnp.concatenate([jnp.exp2(col - m_next) for col in cols], axis=1).astype(BF16)
        pv = _dot(p, jnp.concatenate([v_ref[rows, :], ones], axis=1))
        acc_ref[r] = jnp.concatenate([alpha, alpha], axis=1) * acc_ref[r] + pv
        m_ref[r] = m_next

    off = (b * pl.num_programs(1) + h) * nkb + base
    n = functools.reduce(jnp.maximum, [base + r - first_ref[off + r] for r in range(nsub)])

    def body(t, carry):
        for r in range(nsub):
            j = base + r - n + t
            step(r, jnp.maximum(j, 0), valid=j >= 0)
        return carry

    lax.fori_loop(0, n, body, 0)
    row = lax.broadcasted_iota(jnp.int32, (tk, tk), 0)
    col = lax.broadcasted_iota(jnp.int32, (tk, tk), 1)
    causal = col <= row
    for r in range(nsub):
        step(r, base + r, mask=causal)
    for r in range(nsub):
        acc = acc_ref[r]
        o_ref[r * tk:(r + 1) * tk, :] = (acc[:, :hd] / acc[:, hd:]).astype(o_ref.dtype)


def _fox_attn(qh, kh, vh, c, first, *, tk, nsub):
    nb, nh, s, hd = qh.shape
    tq = tk * nsub
    nq = s // tq
    kv_spec = pl.BlockSpec((None, None, s, hd), lambda b, h, i, f: (b, h, 0, 0))
    return pl.pallas_call(
        functools.partial(_fox_attn_kernel, tk=tk, nsub=nsub, hd=hd),
        grid_spec=pltpu.PrefetchScalarGridSpec(
            num_scalar_prefetch=1, grid=(nb, nh, nq),
            in_specs=[pl.BlockSpec((None, None, tq, hd), lambda b, h, i, f: (b, h, i, 0)),
                      kv_spec, kv_spec,
                      pl.BlockSpec((None, s // tk, 1, tk), lambda b, h, i, f: (b * nh + h, 0, 0, 0))],
            out_specs=pl.BlockSpec((None, tq, hd), lambda b, h, i, f: (b, i, h)),
            scratch_shapes=[pltpu.VMEM((nsub, tk, LANES), F32), pltpu.VMEM((nsub, tk, 2 * hd), F32)]),
        out_shape=jax.ShapeDtypeStruct((nb, s, nh * hd), BF16),
        compiler_params=_cp(3), name="fox_attention",
    )(first, qh, kh, vh, c.reshape(nb * nh, s // tk, 1, tk))


def _fox_sample_kernel(q_ref, kc_ref, vc_ref, kn_ref, vn_ref, c_ref, cf_ref, o_ref, *, hb, hd, p_len, t):
    n = p_len * hb
    kf = kc_ref[...].reshape(n, hd).astype(BF16)
    vf = vc_ref[...].reshape(n, hd).astype(BF16)
    heads = [slice(j * hd, (j + 1) * hd) for j in range(hb)]
    q8 = jnp.concatenate([q_ref[:, sl] for sl in heads], axis=0)
    row_head = lax.shift_right_logical(lax.broadcasted_iota(jnp.int32, (hb * t, LANES), 0), t.bit_length() - 1)
    col_head = lax.broadcasted_iota(jnp.int32, (hb * t, LANES), 1) & (hb - 1)
    own = jnp.where(row_head == col_head, 0.0, NEG)
    s = _dot_nt(q8, kf)
    cf = cf_ref[...]
    cols = [s[:, g * LANES:(g + 1) * LANES] - cf[:, g * LANES:(g + 1) * LANES] + own for g in range(n // LANES)]
    tri = lax.broadcasted_iota(jnp.int32, (t, t), 1) <= lax.broadcasted_iota(jnp.int32, (t, t), 0)
    s_n = jnp.concatenate(
        [jnp.where(tri, _dot_nt(q_ref[:, sl], kn_ref[:, sl].astype(BF16)) - c_ref[j:j + 1, p_len:p_len + t], NEG)
         for j, sl in enumerate(heads)], axis=0)
    m = jnp.maximum(jnp.max(functools.reduce(jnp.maximum, cols), axis=-1, keepdims=True),
                    jnp.max(s_n, axis=-1, keepdims=True))
    ps = [jnp.exp2(col - m) for col in cols]
    p_n = jnp.exp2(s_n - m)
    den = jnp.sum(functools.reduce(jnp.add, ps), axis=-1, keepdims=True) + jnp.sum(p_n, axis=-1, keepdims=True)
    o = _dot(jnp.concatenate(ps, axis=1).astype(BF16), vf)
    o_n = jnp.concatenate([_dot(p_n[j * t:(j + 1) * t].astype(BF16), vn_ref[:, sl].astype(BF16))
                           for j, sl in enumerate(heads)], axis=0)
    o = (o + o_n) / den
    for j, sl in enumerate(heads):
        o_ref[:, sl] = o[j * t:(j + 1) * t].astype(o_ref.dtype)


def _fox_sample_attn(q, k_cache, v_cache, k_new, v_new, c_all, *, layer, t):
    _, nb, p_len, nh, hd = k_cache.shape
    hb = min(SUBLANES, nh)
    assert t & (t - 1) == 0 and hb & (hb - 1) == 0 and (p_len * hb) % LANES == 0 and LANES % hb == 0
    w = hb * hd
    lc = c_all.shape[-1]
    c_flat = c_all[:, :, :p_len].reshape(nb, nh // hb, hb, p_len).transpose(0, 1, 3, 2)
    c_flat = c_flat.reshape(nb, nh // hb, 1, p_len * hb)
    cache_spec = pl.BlockSpec((None, None, p_len, hb, hd), lambda b, h: (layer, b, 0, h, 0))
    row_spec = pl.BlockSpec((t, w), lambda b, h: (b, h))
    return pl.pallas_call(
        functools.partial(_fox_sample_kernel, hb=hb, hd=hd, p_len=p_len, t=t),
        grid=(nb, nh // hb),
        in_specs=[row_spec, cache_spec, cache_spec, row_spec, row_spec,
                  pl.BlockSpec((None, hb, lc), lambda b, h: (b, h, 0)),
                  pl.BlockSpec((None, None, 1, p_len * hb), lambda b, h: (b, h, 0, 0))],
        out_specs=row_spec,
        out_shape=jax.ShapeDtypeStruct((nb * t, nh * hd), BF16),
        compiler_params=_cp(2), name="fox_sample_attention",
    )(q, k_cache, v_cache, k_new, v_new, c_all, c_flat)


def _proj_res_kernel(a_ref, w_ref, x_ref, gate_ref, o_ref):
    o_ref[...] = x_ref[...] + gate_ref[...] * _dot(a_ref[...], w_ref[...])


def _proj_res(a, w, x, gate, *, layer, tm, tpb):
    m, k = a.shape
    n = w.shape[2]
    return pl.pallas_call(
        _proj_res_kernel, grid=(m // tm, 1),
        in_specs=[pl.BlockSpec((tm, k), lambda i, j: (i, 0)),
                  pl.BlockSpec((None, k, n), lambda i, j: (layer, 0, 0)),
                  pl.BlockSpec((tm, n), lambda i, j: (i, 0)),
                  _mod_spec(gate, tpb)],
        out_specs=pl.BlockSpec((tm, n), lambda i, j: (i, 0)),
        out_shape=jax.ShapeDtypeStruct((m, n), F32),
        compiler_params=_cp(2), name="attn_out_proj",
    )(a, w, x, gate)


def _ffn_kernel(x_ref, g_ref, sh_ref, sc_ref, gate_ref, wu_ref, wd_ref, *rest, final):
    if final:
        fg_ref, o_ref, h_ref, acc_ref = rest
    else:
        o_ref, h_ref, acc_ref = rest
    f = pl.program_id(1)

    @pl.when(f == 0)
    def _():
        _norm_mod_store(h_ref, x_ref, g_ref, sh_ref, sc_ref)
        acc_ref[...] = jnp.zeros(acc_ref.shape, F32)

    a = jnp.maximum(_dot(h_ref[...], wu_ref[...]), 0.0)
    acc_ref[...] += _dot((a * a).astype(BF16), wd_ref[...])

    @pl.when(f == pl.num_programs(1) - 1)
    def _():
        y = x_ref[...] + gate_ref[...] * acc_ref[...]
        if final:
            y = _rms(y) * fg_ref[...]
        o_ref[...] = y


def _ffn(x, g, shift, scale, gate, w_up, w_down, final_g, *, layer, tm, tpb):
    m, d = x.shape
    ff = w_up.shape[2]
    tf = min(ff, FFN_STEP_ELEMS // tm)
    vec = pl.BlockSpec((1, d), lambda i, f: (0, 0))
    in_specs = [pl.BlockSpec((tm, d), lambda i, f: (i, 0)), vec,
                _mod_spec(shift, tpb), _mod_spec(scale, tpb), _mod_spec(gate, tpb),
                pl.BlockSpec((None, d, tf), lambda i, f: (layer, 0, f)),
                pl.BlockSpec((None, tf, d), lambda i, f: (layer, f, 0))]
    args = [x, g, shift, scale, gate, w_up, w_down]
    if final_g is not None:
        in_specs.append(vec)
        args.append(final_g)
    return pl.pallas_call(
        functools.partial(_ffn_kernel, final=final_g is not None),
        grid=(m // tm, ff // tf), in_specs=in_specs,
        out_specs=pl.BlockSpec((tm, d), lambda i, f: (i, 0)),
        out_shape=jax.ShapeDtypeStruct((m, d), F32),
        scratch_shapes=[pltpu.VMEM((tm, d), BF16), pltpu.VMEM((tm, d), F32)],
        compiler_params=_cp(2), name="ffn_final" if final_g is not None else "ffn",
    )(*args)


def _rope(r, a, b, c, half):
    cols = []
    for j in range(r.shape[1] // LANES):
        x = r[:, j * LANES:(j + 1) * LANES]
        cols.append(x * a + pltpu.roll(x, LANES - half, 1) * b + pltpu.roll(x, half, 1) * c)
    return jnp.concatenate(cols, axis=1)


def _dup_halves(x, hd):
    lo = lax.broadcasted_iota(jnp.int32, (x.shape[0], LANES), 1) < hd
    zero = jnp.zeros((x.shape[0], LANES), x.dtype)
    out = []
    for j in range(x.shape[1] // LANES):
        p = x[:, j * LANES:(j + 1) * LANES]
        r = pltpu.roll(p, hd, 1)
        out += [jnp.where(lo, p, zero), jnp.where(lo, zero, r), jnp.where(lo, r, zero), jnp.where(lo, zero, p)]
    return jnp.concatenate(out, axis=1)


def _swa_in_kernel(x_ref, g_ref, sh_ref, sc_ref, w_ref, ra_ref, rb_ref, rc_ref,
                   q_ref, kk_ref, vv_ref, k_ref, v_ref, h_ref, *, kvd, hd, rope_half, q_scale):
    _norm_mod_store(h_ref, x_ref, g_ref, sh_ref, sc_ref)
    h = h_ref[...]
    tabs = (ra_ref[...], rb_ref[...], rc_ref[...], rope_half)
    d = q_ref.shape[1]
    tn = 2 * kvd
    for n in range(d // tn):
        r = _dot(h, w_ref[:, n * tn:(n + 1) * tn])
        q_ref[:, n * tn:(n + 1) * tn] = (_rope(r, *tabs) * q_scale).astype(BF16)
    r = _dot(h, w_ref[:, d:])
    k = _rope(r[:, :kvd], *tabs)
    v = r[:, kvd:]
    k_ref[...] = k
    v_ref[...] = v
    kk_ref[...] = _dup_halves(k, hd).astype(BF16)
    vv_ref[...] = _dup_halves(v, hd).astype(BF16)


def _swa_in(x, g, shift, scale, w, tabs, *, layer, tm, tpb, kvd, hd, rope_dims):
    m, d = x.shape
    ntab = tabs[0].shape[0] // tm
    kkw = (kvd // hd) * 2 * LANES
    tab_spec = pl.BlockSpec((tm, LANES), lambda i, n: (i % ntab, 0))
    const = lambda w_: pl.BlockSpec((tm, w_), lambda i, n: (i, 0))
    return pl.pallas_call(
        functools.partial(_swa_in_kernel, kvd=kvd, hd=hd, rope_half=rope_dims // 2, q_scale=hd ** -0.5 * LOG2E),
        grid=(m // tm, 1),
        in_specs=[pl.BlockSpec((tm, d), lambda i, n: (i, 0)),
                  pl.BlockSpec((1, d), lambda i, n: (0, 0)),
                  _mod_spec(shift, tpb), _mod_spec(scale, tpb),
                  pl.BlockSpec((None, d, d + 2 * kvd), lambda i, n: (layer, 0, 0)),
                  tab_spec, tab_spec, tab_spec],
        out_specs=[const(d), const(kkw), const(kkw), const(kvd), const(kvd)],
        out_shape=[jax.ShapeDtypeStruct((m, d), BF16),
                   jax.ShapeDtypeStruct((m, kkw), BF16), jax.ShapeDtypeStruct((m, kkw), BF16),
                   jax.ShapeDtypeStruct((m, kvd), F32), jax.ShapeDtypeStruct((m, kvd), F32)],
        scratch_shapes=[pltpu.VMEM((tm, d), BF16)],
        compiler_params=_cp(2), name="swa_in_proj",
    )(x, g, shift, scale, w, *tabs)


def _rope_tables(pos, hd, rope_dims):
    half = rope_dims // 2
    inv_freq = ROPE_THETA ** (-jnp.arange(half, dtype=F32) * 2.0 / rope_dims)
    ang = pos.astype(F32)[:, None] * inv_freq[None, :]
    cos, sin = jnp.cos(ang), jnp.sin(ang)
    lane = np.arange(LANES) % hd
    idx = lane % half
    a = jnp.where(lane < rope_dims, cos[:, idx], 1.0)
    b = jnp.where(lane < half, -sin[:, idx], 0.0)
    c = jnp.where((lane >= half) & (lane < rope_dims), sin[:, idx], 0.0)
    return a, b, c


def _swa_core(q, k_top, k_bot, v_top, v_bot, bias, sinks, *, group):
    t = q.shape[0]
    npair = group // 2
    win = k_top.shape[0]
    half_lanes = lax.broadcasted_iota(jnp.int32, (win, LANES), 1) < LANES // 2
    count_even = jnp.where(half_lanes, 1.0, 0.0).astype(BF16)
    count_odd = jnp.where(half_lanes, 0.0, 1.0).astype(BF16)
    v2 = jnp.concatenate([jnp.concatenate([v_top, count_even], axis=1),
                          jnp.concatenate([v_bot, count_odd], axis=1)], axis=0)
    qs = jnp.concatenate([q[:, p * LANES:(p + 1) * LANES] for p in range(npair)], axis=0)
    s = _dot_nt(qs, jnp.concatenate([k_top, k_bot], axis=0)) + bias
    ps, es = [], []
    for half in range(2):
        cols = [s[:, half * win + c * LANES: half * win + (c + 1) * LANES] for c in range(win // LANES)]
        sink = sinks[:, half * LANES:(half + 1) * LANES]
        m = jnp.maximum(jnp.max(functools.reduce(jnp.maximum, cols), axis=-1, keepdims=True), sink)
        ps += [jnp.exp2(col - m) for col in cols]
        es.append(jnp.exp2(sink - m))
    o = _dot(jnp.concatenate(ps, axis=1).astype(BF16), v2)
    lane = lax.broadcasted_iota(jnp.int32, (o.shape[0], LANES), 1)
    o = o[:, :LANES] / (o[:, LANES:] + jnp.where(lane < LANES // 2, es[0], es[1]))
    return jnp.concatenate([o[p * t:(p + 1) * t] for p in range(npair)], axis=1)


def _swa_attn_kernel(q_ref, kh_ref, km_ref, vh_ref, vm_ref, bias_ref, sink_ref, o_ref, *, kv, group, hd, ns):
    t = 2 * CHUNK
    gw = group * hd
    first_bias = bias_ref[jnp.minimum(pl.program_id(1), 1)]
    for u in range(ns):
        rows = slice(u * t, (u + 1) * t)
        for g in range(kv):
            def win(halo, main, off):
                sl = slice(g * 2 * LANES + off, g * 2 * LANES + off + LANES)
                before = halo[:, sl] if u == 0 else main[(u - 1) * t:u * t, sl]
                return jnp.concatenate([before, main[rows, sl]], axis=0)
            o = _swa_core(q_ref[rows, g * gw:(g + 1) * gw],
                          win(kh_ref, km_ref, 0), win(kh_ref, km_ref, LANES),
                          win(vh_ref, vm_ref, 0), win(vh_ref, vm_ref, LANES),
                          first_bias if u == 0 else bias_ref[1], sink_ref[g], group=group)
            o_ref[rows, g * gw:(g + 1) * gw] = o.astype(o_ref.dtype)


def _swa_attn(q, kk, vv, bias2, sink_rows, *, nb, s, kv, group, hd):
    t = 2 * CHUNK
    ns = SWA_TILES_PER_STEP if (s // t) % SWA_TILES_PER_STEP == 0 else 1
    nstep = s // (ns * t)
    d = q.shape[1]
    kkw = kk.shape[1]
    main = pl.BlockSpec((ns * t, kkw), lambda b, i: (b * nstep + i, 0))
    halo = pl.BlockSpec((t, kkw), lambda b, i: (b * nstep * ns + jnp.maximum(i * ns - 1, 0), 0))
    full = lambda a: pl.BlockSpec(a.shape, lambda b, i: (0,) * a.ndim)
    return pl.pallas_call(
        functools.partial(_swa_attn_kernel, kv=kv, group=group, hd=hd, ns=ns),
        grid=(nb, nstep),
        in_specs=[pl.BlockSpec((ns * t, d), lambda b, i: (b * nstep + i, 0)),
                  halo, main, halo, main, full(bias2), full(sink_rows)],
        out_specs=pl.BlockSpec((ns * t, d), lambda b, i: (b * nstep + i, 0)),
        out_shape=jax.ShapeDtypeStruct(q.shape, BF16),
        compiler_params=_cp(2), name="swa_attention",
    )(q, kk, kk, vv, vv, bias2, sink_rows)


def _swa_sample_kernel(q_ref, kc_ref, vc_ref, kkn_ref, vvn_ref, bias_ref, sink_ref, o_ref, *, kv, group, hd, t):
    gw = group * hd
    kkc = _dup_halves(kc_ref[...], hd).astype(BF16)
    vvc = _dup_halves(vc_ref[...], hd).astype(BF16)
    pad = jnp.zeros((kc_ref.shape[0] - t, LANES), BF16)
    for g in range(kv):
        def win(cache, new, off):
            sl = slice(g * 2 * LANES + off, g * 2 * LANES + off + LANES)
            return jnp.concatenate([cache[:, sl], new[:, sl], pad], axis=0)
        o = _swa_core(q_ref[:, g * gw:(g + 1) * gw],
                      win(kkc, kkn_ref, 0), win(kkc, kkn_ref, LANES),
                      win(vvc, vvn_ref, 0), win(vvc, vvn_ref, LANES),
                      bias_ref[...], sink_ref[g], group=group)
        o_ref[:, g * gw:(g + 1) * gw] = o.astype(o_ref.dtype)


def _swa_sample_attn(q, k_cache, v_cache, kk_new, vv_new, bias, sink_rows, *, kv, group, hd, t):
    nb, buf, kvd = k_cache.shape
    d = q.shape[1]
    full = lambda a: pl.BlockSpec(a.shape, lambda b: (0,) * a.ndim)
    cache_spec = pl.BlockSpec((None, buf, kvd), lambda b: (b, 0, 0))
    new_spec = pl.BlockSpec((t, kk_new.shape[1]), lambda b: (b, 0))
    return pl.pallas_call(
        functools.partial(_swa_sample_kernel, kv=kv, group=group, hd=hd, t=t),
        grid=(nb,),
        in_specs=[pl.BlockSpec((t, d), lambda b: (b, 0)), cache_spec, cache_spec, new_spec, new_spec,
                  full(bias), full(sink_rows)],
        out_specs=pl.BlockSpec((t, d), lambda b: (b, 0)),
        out_shape=jax.ShapeDtypeStruct(q.shape, BF16),
        compiler_params=_cp(1), name="swa_sample_attention",
    )(q, k_cache, v_cache, kk_new, vv_new, bias, sink_rows)


def _window_bias(valid, npair):
    b = np.where(valid, 0.0, NEG).astype(np.float32)
    return np.tile(b, (npair, 2))


def _prompt_bias(npair):
    t = 2 * CHUNK
    qc = np.arange(t)[:, None] // CHUNK
    kc = np.arange(t + SWA_WINDOW_CHUNKS * CHUNK)[None, :] // CHUNK
    valid = (kc >= qc) & (kc <= qc + SWA_WINDOW_CHUNKS)
    first = valid & (kc >= SWA_WINDOW_CHUNKS)
    return jnp.asarray(np.stack([_window_bias(first, npair), _window_bias(valid, npair)]))


def _sample_bias(past_len, buf, t, npair):
    q_pos = past_len + np.arange(t)
    k_pos = np.concatenate([past_len - buf + np.arange(buf), q_pos])
    qch, kch = q_pos // CHUNK, k_pos // CHUNK
    valid = np.zeros((t, 2 * buf), bool)
    valid[:, :buf + t] = (kch[None, :] <= qch[:, None]) & (kch[None, :] >= qch[:, None] - SWA_WINDOW_CHUNKS)
    return jnp.asarray(_window_bias(valid, npair))


def _sink_rows(sinks, kv, group, t):
    s = (sinks * LOG2E).reshape(kv, group // 2, 1, 2, 1)
    return jnp.broadcast_to(s, (kv, group // 2, t, 2, LANES)).reshape(kv, (group // 2) * t, 2 * LANES)


def kernel(x_prompt, x_sample, c_prompt, c_sample, cache_fox_k, cache_fox_v, cache_fox_logf, cache_swa_k,
           cache_swa_v, ada_w, ada_b, norm_mix_g, norm_ffn_g, fox_w_in, fox_b_f, fox_w_out, swa_w_in,
           swa_sinks, swa_w_out, ffn_w_up, ffn_w_down, final_g):
    bp, s, d = x_prompt.shape
    bs, t, _ = x_sample.shape
    depth = ada_w.shape[0]
    past_len = cache_fox_k.shape[2]
    nh_fox, hd_fox = cache_fox_k.shape[3], cache_fox_k.shape[4]
    buf, kv, hd_swa = cache_swa_k.shape[2], cache_swa_k.shape[3], cache_swa_k.shape[4]
    nh_swa = swa_sinks.shape[1]
    group = nh_swa // kv
    kvd = kv * hd_swa
    rope_dims = hd_swa // 4
    assert hd_fox == LANES and hd_swa == LANES // 2 and group % 2 == 0
    assert buf == SWA_WINDOW_CHUNKS * CHUNK and t <= buf and s % (2 * CHUNK) == 0

    mp, ms = bp * s, bs * t
    tm_p = min(512, s)
    tpb_p = s // tm_p
    tm_fox = min(1024, s)
    tk = min(FOX_TK, s)
    tq = min(FOX_NSUB * tk, s)

    mods = _ada(jnp.concatenate([c_prompt, c_sample], axis=0), ada_w, ada_b)

    def split_mods(i):
        six = jnp.split(mods[i], 6, axis=-1)
        prompt = [m[:bp, None, :] for m in six]
        sample = [jnp.repeat(m[bp:], t, axis=0)[None] for m in six]
        return prompt, sample

    row = lambda v: v.reshape(1, -1)
    xp = x_prompt.reshape(mp, d)
    xs = x_sample.reshape(ms, d)
    pos_p = jnp.arange(s)
    pos_s = past_len + jnp.arange(t)
    tabs_p = _rope_tables(pos_p, hd_swa, rope_dims)
    tabs_s = tuple(jnp.tile(a, (bs, 1)) for a in _rope_tables(pos_s, hd_swa, rope_dims))
    bias_p = _prompt_bias(group // 2)
    bias_s = _sample_bias(past_len, buf, t, group // 2)

    fox_w_in_b, fox_w_out_b, swa_w_in_b, swa_w_out_b, w_up_b, w_down_b = _to_bf16(
        fox_w_in, fox_w_out, swa_w_in, swa_w_out, ffn_w_up, ffn_w_down)

    outs = {k: [] for k in ("fkp", "fvp", "flp", "fks", "fvs", "fls", "skp", "svp", "sks", "svs")}
    for i in range(depth):
        mod_p, mod_s = split_mods(i)
        j = i // 2
        if i % 2 == 0:
            w_f = jnp.pad(fox_w_in_b[j, :, 3 * d:], ((0, 0), (0, LANES - nh_fox)))
            b_f = row(fox_b_f[j])
            qh, kh, vh, k32, v32, lf, n2 = _fox_in(
                xp, row(norm_mix_g[i]), mod_p[0], mod_p[1], fox_w_in_b, w_f, b_f,
                layer=j, tm=tm_fox, tpb=s // tm_fox, head_major=True, nh=nh_fox, hd=hd_fox)
            lft = lf.reshape(bp, s, nh_fox).transpose(0, 2, 1)
            c = _cumsum_rows(lft.reshape(-1, LANES), s // LANES).reshape(bp * nh_fox, s)
            qmax, kmax = _head_norm_bounds(n2, bp, nh_fox, min(FOX_IN_COLS, d) // hd_fox)
            first = _fox_first_blocks(c, qmax, kmax, tk)
            op = _fox_attn(qh, kh, vh, c, first, tk=tk, nsub=tq // tk)
            xp = _proj_res(op.reshape(mp, d), fox_w_out_b, xp, mod_p[2], layer=j, tm=tm_p, tpb=tpb_p)
            outs["fkp"].append(k32.reshape(bp, s, nh_fox, hd_fox))
            outs["fvp"].append(v32.reshape(bp, s, nh_fox, hd_fox))
            outs["flp"].append(lf.reshape(bp, s, nh_fox))
            qs_, k32s, v32s, lfs = _fox_in(xs, row(norm_mix_g[i]), mod_s[0], mod_s[1], fox_w_in_b, w_f, b_f,
                                           layer=j, tm=ms, tpb=1, head_major=False, nh=nh_fox, hd=hd_fox)
            lf_all = jnp.concatenate([cache_fox_logf[j], lfs.reshape(bs, t, nh_fox)], axis=1)
            lc = LANES * int(2 ** np.ceil(np.log2(-(-(past_len + t) // LANES))))
            lf_all = jnp.pad(lf_all.transpose(0, 2, 1), ((0, 0), (0, 0), (0, lc - past_len - t)))
            c_s = _cumsum_rows(lf_all.reshape(-1, LANES), lc // LANES).reshape(bs, nh_fox, lc)
            os_ = _fox_sample_attn(qs_, cache_fox_k, cache_fox_v, k32s, v32s, c_s, layer=j, t=t)
            xs = _proj_res(os_, fox_w_out_b, xs, mod_s[2], layer=j, tm=ms, tpb=1)
            outs["fks"].append(k32s.reshape(bs, t, nh_fox, hd_fox))
            outs["fvs"].append(v32s.reshape(bs, t, nh_fox, hd_fox))
            outs["fls"].append(lfs.reshape(bs, t, nh_fox))
        else:
            q, kk, vv, k32, v32 = _swa_in(xp, row(norm_mix_g[i]), mod_p[0], mod_p[1], swa_w_in_b, tabs_p,
                                          layer=j, tm=tm_p, tpb=tpb_p, kvd=kvd, hd=hd_swa, rope_dims=rope_dims)
            op = _swa_attn(q, kk, vv, bias_p, _sink_rows(swa_sinks[j], kv, group, 2 * CHUNK),
                           nb=bp, s=s, kv=kv, group=group, hd=hd_swa)
            xp = _proj_res(op, swa_w_out_b, xp, mod_p[2], layer=j, tm=tm_p, tpb=tpb_p)
            outs["skp"].append(k32.reshape(bp, s, kvd)[:, -buf:].reshape(bp, buf, kv, hd_swa))
            outs["svp"].append(v32.reshape(bp, s, kvd)[:, -buf:].reshape(bp, buf, kv, hd_swa))
            q, kk, vv, k32, v32 = _swa_in(xs, row(norm_mix_g[i]), mod_s[0], mod_s[1], swa_w_in_b, tabs_s,
                                          layer=j, tm=ms, tpb=1, kvd=kvd, hd=hd_swa, rope_dims=rope_dims)
            os_ = _swa_sample_attn(q, cache_swa_k[j].reshape(bs, buf, kvd), cache_swa_v[j].reshape(bs, buf, kvd),
                                   kk, vv, bias_s, _sink_rows(swa_sinks[j], kv, group, t),
                                   kv=kv, group=group, hd=hd_swa, t=t)
            xs = _proj_res(os_, swa_w_out_b, xs, mod_s[2], layer=j, tm=ms, tpb=1)
            k_all = jnp.concatenate([cache_swa_k[j], k32.reshape(bs, t, kv, hd_swa)], axis=1)
            v_all = jnp.concatenate([cache_swa_v[j], v32.reshape(bs, t, kv, hd_swa)], axis=1)
            outs["sks"].append(k_all[:, -buf:])
            outs["svs"].append(v_all[:, -buf:])
        fg = row(final_g) if i == depth - 1 else None
        xp = _ffn(xp, row(norm_ffn_g[i]), mod_p[3], mod_p[4], mod_p[5], w_up_b, w_down_b, fg,
                  layer=i, tm=tm_p, tpb=tpb_p)
        xs = _ffn(xs, row(norm_ffn_g[i]), mod_s[3], mod_s[4], mod_s[5], w_up_b, w_down_b, fg,
                  layer=i, tm=ms, tpb=1)

    st = lambda k: jnp.stack(outs[k])
    return (xp.reshape(bp, s, d), xs.reshape(bs, t, d),
            st("fkp"), st("fvp"), st("flp"), st("fks"), st("fvs"), st("fls"),
            st("skp"), st("svp"), st("sks"), st("svs"))
```

```python
import functools

import numpy as np
import jax
import jax.numpy as jnp
from jax import lax
from jax.experimental import pallas as pl
from jax.experimental.pallas import tpu as pltpu

F32 = jnp.float32
BF16 = jnp.bfloat16

RMS_EPS = 1e-6
CHUNK = 64
SWA_WINDOW_CHUNKS = 2
ROPE_THETA = 500000.0
LANES = 128
SUBLANES = 8
NEG = -1e30
NEG_BLOCK = -3e38
VMEM_LIMIT_BYTES = 56 * 1024 * 1024
LOG2E = 1.4426950408889634
FOX_SKIP_LOG2 = 136.0
FFN_STEP_ELEMS = 512 * 1024
SWA_TILES_PER_STEP = 8
FOX_IN_COLS = 512
FOX_TK = 512
FOX_NSUB = 16


def _cp(n_axes):
    return pltpu.CompilerParams(dimension_semantics=("arbitrary",) * n_axes,
                                vmem_limit_bytes=VMEM_LIMIT_BYTES)


def _dot(a, b):
    return jnp.dot(a, b, preferred_element_type=F32)


def _dot_nt(a, b):
    return lax.dot_general(a, b, (((1,), (1,)), ((), ())), preferred_element_type=F32)


def _rms(x):
    return x * lax.rsqrt(jnp.mean(x * x, axis=-1, keepdims=True) + RMS_EPS)


NORM_ROWS = 32


def _norm_mod_store(h_ref, x_ref, g_ref, sh_ref, sc_ref):
    per_token = sh_ref.shape[0] > 1
    g = g_ref[...]
    if not per_token:
        gain, shift = g * (1.0 + sc_ref[...]), sh_ref[...]

    def body(i, carry):
        rows = pl.ds(pl.multiple_of(i * NORM_ROWS, NORM_ROWS), NORM_ROWS)
        x = x_ref[rows, :]
        if per_token:
            gain_, shift_ = g * (1.0 + sc_ref[rows, :]), sh_ref[rows, :]
        else:
            gain_, shift_ = gain, shift
        h_ref[rows, :] = (_rms(x) * gain_ + shift_).astype(h_ref.dtype)
        return carry

    lax.fori_loop(0, x_ref.shape[0] // NORM_ROWS, body, 0, unroll=4)


def _log_sigmoid(z):
    return jnp.minimum(z, 0.0) - jnp.log1p(jnp.exp(-jnp.abs(z)))


CAST_BLOCK_BYTES = 4 * 1024 * 1024


def _cast_kernel(*refs):
    n = len(refs) // 2
    for x_ref, o_ref in zip(refs[:n], refs[n:]):
        o_ref[...] = x_ref[...].astype(o_ref.dtype)


def _to_bf16(*ws):
    steps = 1
    while any(w.size * 4 // steps > CAST_BLOCK_BYTES for w in ws):
        steps *= 2
    specs = []
    for w in ws:
        layers, rows, cols = w.shape
        per_layer = steps // layers
        assert per_layer >= 1 and rows % (16 * per_layer) == 0
        specs.append(pl.BlockSpec((None, rows // per_layer, cols),
                                  lambda i, per_layer=per_layer: (i // per_layer, i % per_layer, 0)))
    return pl.pallas_call(
        _cast_kernel, grid=(steps,), in_specs=specs, out_specs=specs,
        out_shape=[jax.ShapeDtypeStruct(w.shape, BF16) for w in ws],
        compiler_params=_cp(1), name="weight_to_bf16",
    )(*ws)


def _ada_kernel(c_ref, w_ref, b_ref, o_ref):
    c = c_ref[...]
    a = (c / (1.0 + jnp.exp(-c))).astype(BF16)
    o_ref[...] = _dot(a, w_ref[...].astype(BF16)) + b_ref[...]


def _ada(c_all, ada_w, ada_b):
    depth, d, n6 = ada_w.shape
    r = c_all.shape[0]
    tn = min(1024, n6)
    return pl.pallas_call(
        _ada_kernel,
        grid=(depth, n6 // tn),
        in_specs=[pl.BlockSpec((r, d), lambda l, n: (0, 0)),
                  pl.BlockSpec((None, d, tn), lambda l, n: (l, 0, n)),
                  pl.BlockSpec((None, 1, tn), lambda l, n: (l, 0, n))],
        out_specs=pl.BlockSpec((None, r, tn), lambda l, n: (l, 0, n)),
        out_shape=jax.ShapeDtypeStruct((depth, r, n6), F32),
        compiler_params=_cp(2), name="ada_params",
    )(c_all, ada_w, ada_b.reshape(depth, 1, n6))


def _fox_in_kernel(x_ref, g_ref, sh_ref, sc_ref, wq_ref, wk_ref, wv_ref, wf_ref, bf_ref, *rest,
                   head_major, hd, nh, q_scale):
    if head_major:
        qb_ref, kb_ref, vb_ref, k_ref, v_ref, lf_ref, n2_ref, h_ref = rest
    else:
        qb_ref, k_ref, v_ref, lf_ref, h_ref = rest

    @pl.when(pl.program_id(1) == 0)
    def _():
        _norm_mod_store(h_ref, x_ref, g_ref, sh_ref, sc_ref)
        z = _dot(h_ref[...], wf_ref[...])[:, :nh] + bf_ref[...]
        lf_ref[...] = _log_sigmoid(z)

    hb = h_ref[...]
    q = _dot(hb, wq_ref[...]) * q_scale
    k = _dot(hb, wk_ref[...])
    v = _dot(hb, wv_ref[...])
    k_ref[...] = k
    v_ref[...] = v
    if head_major:
        for j in range(q.shape[1] // hd):
            sl = slice(j * hd, (j + 1) * hd)
            qb_ref[j] = q[:, sl].astype(BF16)
            kb_ref[j] = k[:, sl].astype(BF16)
            vb_ref[j] = v[:, sl].astype(BF16)
        tm, tn = q.shape
        grp = lax.shift_right_logical(lax.broadcasted_iota(jnp.int32, (2 * tn, LANES), 0), hd.bit_length() - 1)
        sel = jnp.where(grp == lax.broadcasted_iota(jnp.int32, (2 * tn, LANES), 1), 1.0, 0.0).astype(BF16)
        n2 = _dot(jnp.concatenate([q * q, k * k], axis=1).astype(BF16), sel)
        n2_ref[...] = jnp.max(n2.reshape(tm // SUBLANES, SUBLANES, LANES), axis=0)
    else:
        qb_ref[...] = q.astype(BF16)


def _mod_spec(mod, tpb, cols=None):
    _, rows, d = mod.shape
    if cols is None:
        return pl.BlockSpec((None, rows, d), lambda i, n: (i // tpb, 0, 0))
    return pl.BlockSpec((None, rows, cols), lambda i, n: (i // tpb, 0, n))


def _fox_in(x, g, shift, scale, w_in, w_f, b_f, *, layer, tm, tpb, head_major, nh, hd):
    m, d = x.shape
    tn = min(FOX_IN_COLS, d)
    nt = d // tn
    hpt = tn // hd
    nb, s = m // (tm * tpb), tm * tpb
    in_specs = [pl.BlockSpec((tm, d), lambda i, n: (i, 0)),
                pl.BlockSpec((1, d), lambda i, n: (0, 0)),
                _mod_spec(shift, tpb), _mod_spec(scale, tpb),
                pl.BlockSpec((None, d, tn), lambda i, n: (layer, 0, n)),
                pl.BlockSpec((None, d, tn), lambda i, n: (layer, 0, n + nt)),
                pl.BlockSpec((None, d, tn), lambda i, n: (layer, 0, n + 2 * nt)),
                pl.BlockSpec((d, LANES), lambda i, n: (0, 0)),
                pl.BlockSpec((1, nh), lambda i, n: (0, 0))]
    rm_spec = pl.BlockSpec((tm, tn), lambda i, n: (i, n))
    lf_spec = pl.BlockSpec((tm, nh), lambda i, n: (i, 0))
    if head_major:
        hm_spec = pl.BlockSpec((None, hpt, tm, hd), lambda i, n: (i // tpb, n, i % tpb, 0))
        hm_shape = jax.ShapeDtypeStruct((nb, nh, s, hd), BF16)
        assert hd & (hd - 1) == 0 and 2 * hpt <= LANES
        n2_spec = pl.BlockSpec((SUBLANES, LANES), lambda i, n: (i, n))
        out_specs = [hm_spec, hm_spec, hm_spec, rm_spec, rm_spec, lf_spec, n2_spec]
        out_shape = [hm_shape, hm_shape, hm_shape]
    else:
        out_specs = [rm_spec, rm_spec, rm_spec, lf_spec]
        out_shape = [jax.ShapeDtypeStruct((m, d), BF16)]
    out_shape += [jax.ShapeDtypeStruct((m, d), F32), jax.ShapeDtypeStruct((m, d), F32),
                  jax.ShapeDtypeStruct((m, nh), F32)]
    if head_major:
        out_shape += [jax.ShapeDtypeStruct((m // tm * SUBLANES, nt * LANES), F32)]
    return pl.pallas_call(
        functools.partial(_fox_in_kernel, head_major=head_major, hd=hd, nh=nh, q_scale=hd ** -0.5 * LOG2E),
        grid=(m // tm, nt), in_specs=in_specs, out_specs=out_specs, out_shape=out_shape,
        scratch_shapes=[pltpu.VMEM((tm, d), BF16)],
        compiler_params=_cp(2), name="fox_in_proj",
    )(x, g, shift, scale, w_in, w_in, w_in, w_f, b_f)


def _split3(x):
    x1 = x.astype(BF16)
    r1 = x - x1.astype(F32)
    x2 = r1.astype(BF16)
    x3 = (r1 - x2.astype(F32)).astype(BF16)
    return x1, x2, x3


CUMSUM_BLOCK_ROWS = 1024


def _cumsum_kernel(x_ref, u_ref, l_ref, o_ref):
    u = u_ref[...]
    lo = l_ref[...]
    y = sum(_dot(p, u) for p in _split3(x_ref[...]))
    tot = jnp.broadcast_to(y[:, LANES - 1:LANES], y.shape)
    o_ref[...] = (y + sum(_dot(lo, p) for p in _split3(tot))) * LOG2E


def _cumsum_rows(x, nc):
    rows = x.shape[0]
    assert LANES % nc == 0 or nc % LANES == 0
    n = max(nc, LANES)
    assert rows % n == 0
    while 2 * n <= CUMSUM_BLOCK_ROWS and rows % (2 * n) == 0:
        n *= 2
    u = jnp.asarray(np.triu(np.ones((LANES, LANES), np.float32)), BF16)
    idx = np.arange(n)
    same = (idx[:, None] // nc) == (idx[None, :] // nc)
    lo = jnp.asarray((same & (idx[None, :] < idx[:, None])).astype(np.float32), BF16)
    return pl.pallas_call(
        _cumsum_kernel, grid=(rows // n,),
        in_specs=[pl.BlockSpec((n, LANES), lambda i: (i, 0)),
                  pl.BlockSpec((LANES, LANES), lambda i: (0, 0)),
                  pl.BlockSpec((n, n), lambda i: (0, 0))],
        out_specs=pl.BlockSpec((n, LANES), lambda i: (i, 0)),
        out_shape=jax.ShapeDtypeStruct(x.shape, F32),
        compiler_params=_cp(1), name="logf_cumsum",
    )(x, u, lo)


NORM_SLACK = 1.01


def _head_norm_bounds(n2, nb, nh, hpt):
    n2 = jnp.max(n2.reshape(nb, -1, nh // hpt, LANES), axis=1)
    bound = lambda part: (NORM_SLACK * jnp.sqrt(part)).reshape(-1)
    return bound(n2[:, :, :hpt]), bound(n2[:, :, hpt:2 * hpt])


def _fox_first_blocks(c, qmax, kmax, tk):
    bh, s = c.shape
    cb = c.reshape(bh, s // tk, tk)
    c_start, c_end = cb[:, :, 0], cb[:, :, -1]
    budget = (2.0 * qmax * kmax + FOX_SKIP_LOG2)[:, None, None]
    skippable = (c_end[:, None, :] - c_start[:, :, None]) > budget
    earlier = np.tril(np.ones((s // tk, s // tk), bool), -1)
    return jnp.sum(skippable & earlier, axis=-1).astype(jnp.int32).reshape(-1)


def _fox_attn_kernel(first_ref, q_ref, k_ref, v_ref, c_ref, o_ref, m_ref, acc_ref, *, tk, nsub, hd):
    b, h, qi = pl.program_id(0), pl.program_id(1), pl.program_id(2)
    nkb = pl.num_programs(2) * nsub
    base = qi * nsub
    m_ref[...] = jnp.full(m_ref.shape, NEG, F32)
    acc_ref[...] = jnp.zeros(acc_ref.shape, F32)
    ones = jnp.ones((tk, hd), BF16)
    ncol = tk // LANES
    c_first = [c_ref[base + r][:, 0:1] for r in range(nsub)]

    def step(r, j, mask=None, valid=None):
        rows = pl.ds(pl.multiple_of(j * tk, tk), tk)
        c_row = c_ref[j]
        if valid is not None:
            c_row = jnp.where(valid, c_row, -NEG_BLOCK)
        s = _dot_nt(q_ref[r * tk:(r + 1) * tk, :], k_ref[rows, :]) + (c_first[r] - c_row)
        if mask is not None:
            s = jnp.where(mask, s, NEG)
        cols = [s[:, c * LANES:(c + 1) * LANES] for c in range(ncol)]
        m_cur = functools.reduce(jnp.maximum, cols)
        m_prev = m_ref[r]
        m_next = jnp.maximum(m_prev, jnp.max(m_cur, axis=-1, keepdims=True))
        alpha = jnp.exp2(m_prev - m_next)
        p = jnp.concatenate([jnp.exp2(col - m_next) for col in cols], axis=1).astype(BF16)
        pv = _dot(p, jnp.concatenate([v_ref[rows, :], ones], axis=1))
        acc_ref[r] = jnp.concatenate([alpha, alpha], axis=1) * acc_ref[r] + pv
        m_ref[r] = m_next

    off = (b * pl.num_programs(1) + h) * nkb + base
    n = functools.reduce(jnp.maximum, [base + r - first_ref[off + r] for r in range(nsub)])

    def body(t, carry):
        for r in range(nsub):
            j = base + r - n + t
            step(r, jnp.maximum(j, 0), valid=j >= 0)
        return carry

    lax.fori_loop(0, n, body, 0)
    row = lax.broadcasted_iota(jnp.int32, (tk, tk), 0)
    col = lax.broadcasted_iota(jnp.int32, (tk, tk), 1)
    causal = col <= row
    for r in range(nsub):
        step(r, base + r, mask=causal)
    for r in range(nsub):
        acc = acc_ref[r]
        o_ref[r * tk:(r + 1) * tk, :] = (acc[:, :hd] / acc[:, hd:]).astype(o_ref.dtype)


def _fox_attn(qh, kh, vh, c, first, *, tk, nsub):
    nb, nh, s, hd = qh.shape
    tq = tk * nsub
    nq = s // tq
    kv_spec = pl.BlockSpec((None, None, s, hd), lambda b, h, i, f: (b, h, 0, 0))
    return pl.pallas_call(
        functools.partial(_fox_attn_kernel, tk=tk, nsub=nsub, hd=hd),
        grid_spec=pltpu.PrefetchScalarGridSpec(
            num_scalar_prefetch=1, grid=(nb, nh, nq),
            in_specs=[pl.BlockSpec((None, None, tq, hd), lambda b, h, i, f: (b, h, i, 0)),
                      kv_spec, kv_spec,
                      pl.BlockSpec((None, s // tk, 1, tk), lambda b, h, i, f: (b * nh + h, 0, 0, 0))],
            out_specs=pl.BlockSpec((None, tq, hd), lambda b, h, i, f: (b, i, h)),
            scratch_shapes=[pltpu.VMEM((nsub, tk, LANES), F32), pltpu.VMEM((nsub, tk, 2 * hd), F32)]),
        out_shape=jax.ShapeDtypeStruct((nb, s, nh * hd), BF16),
        compiler_params=_cp(3), name="fox_attention",
    )(first, qh, kh, vh, c.reshape(nb * nh, s // tk, 1, tk))


def _fox_sample_kernel(q_ref, kc_ref, vc_ref, kn_ref, vn_ref, c_ref, cf_ref, after_ref, o_ref, *, hb, hd, p_len, t):
    del after_ref
    n = p_len * hb
    kf = kc_ref[...].reshape(n, hd).astype(BF16)
    vf = vc_ref[...].reshape(n, hd).astype(BF16)
    heads = [slice(j * hd, (j + 1) * hd) for j in range(hb)]
    q8 = jnp.concatenate([q_ref[:, sl] for sl in heads], axis=0)
    row_head = lax.shift_right_logical(lax.broadcasted_iota(jnp.int32, (hb * t, LANES), 0), t.bit_length() - 1)
    col_head = lax.broadcasted_iota(jnp.int32, (hb * t, LANES), 1) & (hb - 1)
    own = jnp.where(row_head == col_head, 0.0, NEG)
    s = _dot_nt(q8, kf)
    cf = cf_ref[...]
    cols = [s[:, g * LANES:(g + 1) * LANES] - cf[:, g * LANES:(g + 1) * LANES] + own for g in range(n // LANES)]
    tri = lax.broadcasted_iota(jnp.int32, (t, t), 1) <= lax.broadcasted_iota(jnp.int32, (t, t), 0)
    s_n = jnp.concatenate(
        [jnp.where(tri, _dot_nt(q_ref[:, sl], kn_ref[:, sl].astype(BF16)) - c_ref[j:j + 1, p_len:p_len + t], NEG)
         for j, sl in enumerate(heads)], axis=0)
    m = jnp.maximum(jnp.max(functools.reduce(jnp.maximum, cols), axis=-1, keepdims=True),
                    jnp.max(s_n, axis=-1, keepdims=True))
    ps = [jnp.exp2(col - m) for col in cols]
    p_n = jnp.exp2(s_n - m)
    den = jnp.sum(functools.reduce(jnp.add, ps), axis=-1, keepdims=True) + jnp.sum(p_n, axis=-1, keepdims=True)
    o = _dot(jnp.concatenate(ps, axis=1).astype(BF16), vf)
    o_n = jnp.concatenate([_dot(p_n[j * t:(j + 1) * t].astype(BF16), vn_ref[:, sl].astype(BF16))
                           for j, sl in enumerate(heads)], axis=0)
    o = (o + o_n) / den
    for j, sl in enumerate(heads):
        o_ref[:, sl] = o[j * t:(j + 1) * t].astype(o_ref.dtype)


def _fox_sample_attn(q, k_cache, v_cache, k_new, v_new, c_all, after, *, layer, t):
    _, nb, p_len, nh, hd = k_cache.shape
    hb = min(SUBLANES, nh)
    assert t & (t - 1) == 0 and hb & (hb - 1) == 0 and (p_len * hb) % LANES == 0 and LANES % hb == 0
    w = hb * hd
    lc = c_all.shape[-1]
    c_flat = c_all[:, :, :p_len].reshape(nb, nh // hb, hb, p_len).transpose(0, 1, 3, 2)
    c_flat = c_flat.reshape(nb, nh // hb, 1, p_len * hb)
    cache_spec = pl.BlockSpec((None, None, p_len, hb, hd), lambda b, h: (layer, b, 0, h, 0))
    row_spec = pl.BlockSpec((t, w), lambda b, h: (b, h))
    return pl.pallas_call(
        functools.partial(_fox_sample_kernel, hb=hb, hd=hd, p_len=p_len, t=t),
        grid=(nb, nh // hb),
        in_specs=[row_spec, cache_spec, cache_spec, row_spec, row_spec,
                  pl.BlockSpec((None, hb, lc), lambda b, h: (b, h, 0)),
                  pl.BlockSpec((None, None, 1, p_len * hb), lambda b, h: (b, h, 0, 0)),
                  pl.BlockSpec(memory_space=pl.ANY)],
        out_specs=row_spec,
        out_shape=jax.ShapeDtypeStruct((nb * t, nh * hd), BF16),
        compiler_params=_cp(2), name="fox_sample_attention",
    )(q, k_cache, v_cache, k_new, v_new, c_all, c_flat, after)


def _proj_res_kernel(a_ref, w_ref, x_ref, gate_ref, o_ref):
    o_ref[...] = x_ref[...] + gate_ref[...] * _dot(a_ref[...], w_ref[...])


def _proj_res(a, w, x, gate, *, layer, tm, tpb):
    m, k = a.shape
    n = w.shape[2]
    return pl.pallas_call(
        _proj_res_kernel, grid=(m // tm, 1),
        in_specs=[pl.BlockSpec((tm, k), lambda i, j: (i, 0)),
                  pl.BlockSpec((None, k, n), lambda i, j: (layer, 0, 0)),
                  pl.BlockSpec((tm, n), lambda i, j: (i, 0)),
                  _mod_spec(gate, tpb)],
        out_specs=pl.BlockSpec((tm, n), lambda i, j: (i, 0)),
        out_shape=jax.ShapeDtypeStruct((m, n), F32),
        compiler_params=_cp(2), name="attn_out_proj",
    )(a, w, x, gate)


def _ffn_kernel(x_ref, g_ref, sh_ref, sc_ref, gate_ref, wu_ref, wd_ref, *rest, final):
    if final:
        fg_ref, o_ref, h_ref, acc_ref = rest
    else:
        o_ref, h_ref, acc_ref = rest
    f = pl.program_id(1)

    @pl.when(f == 0)
    def _():
        _norm_mod_store(h_ref, x_ref, g_ref, sh_ref, sc_ref)
        acc_ref[...] = jnp.zeros(acc_ref.shape, F32)

    a = jnp.maximum(_dot(h_ref[...], wu_ref[...]), 0.0)
    acc_ref[...] += _dot((a * a).astype(BF16), wd_ref[...])

    @pl.when(f == pl.num_programs(1) - 1)
    def _():
        y = x_ref[...] + gate_ref[...] * acc_ref[...]
        if final:
            y = _rms(y) * fg_ref[...]
        o_ref[...] = y


def _ffn(x, g, shift, scale, gate, w_up, w_down, final_g, *, layer, tm, tpb):
    m, d = x.shape
    ff = w_up.shape[2]
    tf = min(ff, FFN_STEP_ELEMS // tm)
    vec = pl.BlockSpec((1, d), lambda i, f: (0, 0))
    in_specs = [pl.BlockSpec((tm, d), lambda i, f: (i, 0)), vec,
                _mod_spec(shift, tpb), _mod_spec(scale, tpb), _mod_spec(gate, tpb),
                pl.BlockSpec((None, d, tf), lambda i, f: (layer, 0, f)),
                pl.BlockSpec((None, tf, d), lambda i, f: (layer, f, 0))]
    args = [x, g, shift, scale, gate, w_up, w_down]
    if final_g is not None:
        in_specs.append(vec)
        args.append(final_g)
    return pl.pallas_call(
        functools.partial(_ffn_kernel, final=final_g is not None),
        grid=(m // tm, ff // tf), in_specs=in_specs,
        out_specs=pl.BlockSpec((tm, d), lambda i, f: (i, 0)),
        out_shape=jax.ShapeDtypeStruct((m, d), F32),
        scratch_shapes=[pltpu.VMEM((tm, d), BF16), pltpu.VMEM((tm, d), F32)],
        compiler_params=_cp(2), name="ffn_final" if final_g is not None else "ffn",
    )(*args)


def _rope(r, a, b, c, half):
    cols = []
    for j in range(r.shape[1] // LANES):
        x = r[:, j * LANES:(j + 1) * LANES]
        cols.append(x * a + pltpu.roll(x, LANES - half, 1) * b + pltpu.roll(x, half, 1) * c)
    return jnp.concatenate(cols, axis=1)


def _dup_halves(x, hd):
    lo = lax.broadcasted_iota(jnp.int32, (x.shape[0], LANES), 1) < hd
    zero = jnp.zeros((x.shape[0], LANES), x.dtype)
    out = []
    for j in range(x.shape[1] // LANES):
        p = x[:, j * LANES:(j + 1) * LANES]
        r = pltpu.roll(p, hd, 1)
        out += [jnp.where(lo, p, zero), jnp.where(lo, zero, r), jnp.where(lo, r, zero), jnp.where(lo, zero, p)]
    return jnp.concatenate(out, axis=1)


def _swa_in_kernel(x_ref, g_ref, sh_ref, sc_ref, w_ref, ra_ref, rb_ref, rc_ref,
                   q_ref, kk_ref, vv_ref, k_ref, v_ref, h_ref, *, kvd, hd, rope_half, q_scale):
    _norm_mod_store(h_ref, x_ref, g_ref, sh_ref, sc_ref)
    h = h_ref[...]
    tabs = (ra_ref[...], rb_ref[...], rc_ref[...], rope_half)
    d = q_ref.shape[1]
    tn = 2 * kvd
    for n in range(d // tn):
        r = _dot(h, w_ref[:, n * tn:(n + 1) * tn])
        q_ref[:, n * tn:(n + 1) * tn] = (_rope(r, *tabs) * q_scale).astype(BF16)
    r = _dot(h, w_ref[:, d:])
    k = _rope(r[:, :kvd], *tabs)
    v = r[:, kvd:]
    k_ref[...] = k
    v_ref[...] = v
    kk_ref[...] = _dup_halves(k, hd).astype(BF16)
    vv_ref[...] = _dup_halves(v, hd).astype(BF16)


def _swa_in(x, g, shift, scale, w, tabs, *, layer, tm, tpb, kvd, hd, rope_dims):
    m, d = x.shape
    ntab = tabs[0].shape[0] // tm
    kkw = (kvd // hd) * 2 * LANES
    tab_spec = pl.BlockSpec((tm, LANES), lambda i, n: (i % ntab, 0))
    const = lambda w_: pl.BlockSpec((tm, w_), lambda i, n: (i, 0))
    return pl.pallas_call(
        functools.partial(_swa_in_kernel, kvd=kvd, hd=hd, rope_half=rope_dims // 2, q_scale=hd ** -0.5 * LOG2E),
        grid=(m // tm, 1),
        in_specs=[pl.BlockSpec((tm, d), lambda i, n: (i, 0)),
                  pl.BlockSpec((1, d), lambda i, n: (0, 0)),
                  _mod_spec(shift, tpb), _mod_spec(scale, tpb),
                  pl.BlockSpec((None, d, d + 2 * kvd), lambda i, n: (layer, 0, 0)),
                  tab_spec, tab_spec, tab_spec],
        out_specs=[const(d), const(kkw), const(kkw), const(kvd), const(kvd)],
        out_shape=[jax.ShapeDtypeStruct((m, d), BF16),
                   jax.ShapeDtypeStruct((m, kkw), BF16), jax.ShapeDtypeStruct((m, kkw), BF16),
                   jax.ShapeDtypeStruct((m, kvd), F32), jax.ShapeDtypeStruct((m, kvd), F32)],
        scratch_shapes=[pltpu.VMEM((tm, d), BF16)],
        compiler_params=_cp(2), name="swa_in_proj",
    )(x, g, shift, scale, w, *tabs)


def _rope_tables(pos, hd, rope_dims):
    half = rope_dims // 2
    inv_freq = ROPE_THETA ** (-jnp.arange(half, dtype=F32) * 2.0 / rope_dims)
    ang = pos.astype(F32)[:, None] * inv_freq[None, :]
    cos, sin = jnp.cos(ang), jnp.sin(ang)
    lane = np.arange(LANES) % hd
    idx = lane % half
    a = jnp.where(lane < rope_dims, cos[:, idx], 1.0)
    b = jnp.where(lane < half, -sin[:, idx], 0.0)
    c = jnp.where((lane >= half) & (lane < rope_dims), sin[:, idx], 0.0)
    return a, b, c


def _swa_core(q, k_top, k_bot, v_top, v_bot, bias, sinks, *, group):
    t = q.shape[0]
    npair = group // 2
    win = k_top.shape[0]
    half_lanes = lax.broadcasted_iota(jnp.int32, (win, LANES), 1) < LANES // 2
    count_even = jnp.where(half_lanes, 1.0, 0.0).astype(BF16)
    count_odd = jnp.where(half_lanes, 0.0, 1.0).astype(BF16)
    v2 = jnp.concatenate([jnp.concatenate([v_top, count_even], axis=1),
                          jnp.concatenate([v_bot, count_odd], axis=1)], axis=0)
    qs = jnp.concatenate([q[:, p * LANES:(p + 1) * LANES] for p in range(npair)], axis=0)
    s = _dot_nt(qs, jnp.concatenate([k_top, k_bot], axis=0)) + bias
    ps, es = [], []
    for half in range(2):
        cols = [s[:, half * win + c * LANES: half * win + (c + 1) * LANES] for c in range(win // LANES)]
        sink = sinks[:, half * LANES:(half + 1) * LANES]
        m = jnp.maximum(jnp.max(functools.reduce(jnp.maximum, cols), axis=-1, keepdims=True), sink)
        ps += [jnp.exp2(col - m) for col in cols]
        es.append(jnp.exp2(sink - m))
    o = _dot(jnp.concatenate(ps, axis=1).astype(BF16), v2)
    lane = lax.broadcasted_iota(jnp.int32, (o.shape[0], LANES), 1)
    o = o[:, :LANES] / (o[:, LANES:] + jnp.where(lane < LANES // 2, es[0], es[1]))
    return jnp.concatenate([o[p * t:(p + 1) * t] for p in range(npair)], axis=1)


def _swa_attn_kernel(q_ref, kh_ref, km_ref, vh_ref, vm_ref, bias_ref, sink_ref, o_ref, *, kv, group, hd, ns):
    t = 2 * CHUNK
    gw = group * hd
    first_bias = bias_ref[jnp.minimum(pl.program_id(1), 1)]
    for u in range(ns):
        rows = slice(u * t, (u + 1) * t)
        for g in range(kv):
            def win(halo, main, off):
                sl = slice(g * 2 * LANES + off, g * 2 * LANES + off + LANES)
                before = halo[:, sl] if u == 0 else main[(u - 1) * t:u * t, sl]
                return jnp.concatenate([before, main[rows, sl]], axis=0)
            o = _swa_core(q_ref[rows, g * gw:(g + 1) * gw],
                          win(kh_ref, km_ref, 0), win(kh_ref, km_ref, LANES),
                          win(vh_ref, vm_ref, 0), win(vh_ref, vm_ref, LANES),
                          first_bias if u == 0 else bias_ref[1], sink_ref[g], group=group)
            o_ref[rows, g * gw:(g + 1) * gw] = o.astype(o_ref.dtype)


def _swa_attn(q, kk, vv, bias2, sink_rows, *, nb, s, kv, group, hd):
    t = 2 * CHUNK
    ns = SWA_TILES_PER_STEP if (s // t) % SWA_TILES_PER_STEP == 0 else 1
    nstep = s // (ns * t)
    d = q.shape[1]
    kkw = kk.shape[1]
    main = pl.BlockSpec((ns * t, kkw), lambda b, i: (b * nstep + i, 0))
    halo = pl.BlockSpec((t, kkw), lambda b, i: (b * nstep * ns + jnp.maximum(i * ns - 1, 0), 0))
    full = lambda a: pl.BlockSpec(a.shape, lambda b, i: (0,) * a.ndim)
    return pl.pallas_call(
        functools.partial(_swa_attn_kernel, kv=kv, group=group, hd=hd, ns=ns),
        grid=(nb, nstep),
        in_specs=[pl.BlockSpec((ns * t, d), lambda b, i: (b * nstep + i, 0)),
                  halo, main, halo, main, full(bias2), full(sink_rows)],
        out_specs=pl.BlockSpec((ns * t, d), lambda b, i: (b * nstep + i, 0)),
        out_shape=jax.ShapeDtypeStruct(q.shape, BF16),
        compiler_params=_cp(2), name="swa_attention",
    )(q, kk, kk, vv, vv, bias2, sink_rows)


def _swa_sample_kernel(q_ref, kc_ref, vc_ref, kkn_ref, vvn_ref, bias_ref, sink_ref, o_ref, *, kv, group, hd, t):
    gw = group * hd
    kkc = _dup_halves(kc_ref[...], hd).astype(BF16)
    vvc = _dup_halves(vc_ref[...], hd).astype(BF16)
    pad = jnp.zeros((kc_ref.shape[0] - t, LANES), BF16)
    for g in range(kv):
        def win(cache, new, off):
            sl = slice(g * 2 * LANES + off, g * 2 * LANES + off + LANES)
            return jnp.concatenate([cache[:, sl], new[:, sl], pad], axis=0)
        o = _swa_core(q_ref[:, g * gw:(g + 1) * gw],
                      win(kkc, kkn_ref, 0), win(kkc, kkn_ref, LANES),
                      win(vvc, vvn_ref, 0), win(vvc, vvn_ref, LANES),
                      bias_ref[...], sink_ref[g], group=group)
        o_ref[:, g * gw:(g + 1) * gw] = o.astype(o_ref.dtype)


def _swa_sample_attn(q, k_cache, v_cache, kk_new, vv_new, bias, sink_rows, *, kv, group, hd, t):
    nb, buf, kvd = k_cache.shape
    d = q.shape[1]
    full = lambda a: pl.BlockSpec(a.shape, lambda b: (0,) * a.ndim)
    cache_spec = pl.BlockSpec((None, buf, kvd), lambda b: (b, 0, 0))
    new_spec = pl.BlockSpec((t, kk_new.shape[1]), lambda b: (b, 0))
    return pl.pallas_call(
        functools.partial(_swa_sample_kernel, kv=kv, group=group, hd=hd, t=t),
        grid=(nb,),
        in_specs=[pl.BlockSpec((t, d), lambda b: (b, 0)), cache_spec, cache_spec, new_spec, new_spec,
                  full(bias), full(sink_rows)],
        out_specs=pl.BlockSpec((t, d), lambda b: (b, 0)),
        out_shape=jax.ShapeDtypeStruct(q.shape, BF16),
        compiler_params=_cp(1), name="swa_sample_attention",
    )(q, k_cache, v_cache, kk_new, vv_new, bias, sink_rows)


def _window_bias(valid, npair):
    b = np.where(valid, 0.0, NEG).astype(np.float32)
    return np.tile(b, (npair, 2))


def _prompt_bias(npair):
    t = 2 * CHUNK
    qc = np.arange(t)[:, None] // CHUNK
    kc = np.arange(t + SWA_WINDOW_CHUNKS * CHUNK)[None, :] // CHUNK
    valid = (kc >= qc) & (kc <= qc + SWA_WINDOW_CHUNKS)
    first = valid & (kc >= SWA_WINDOW_CHUNKS)
    return jnp.asarray(np.stack([_window_bias(first, npair), _window_bias(valid, npair)]))


def _sample_bias(past_len, buf, t, npair):
    q_pos = past_len + np.arange(t)
    k_pos = np.concatenate([past_len - buf + np.arange(buf), q_pos])
    qch, kch = q_pos // CHUNK, k_pos // CHUNK
    valid = np.zeros((t, 2 * buf), bool)
    valid[:, :buf + t] = (kch[None, :] <= qch[:, None]) & (kch[None, :] >= qch[:, None] - SWA_WINDOW_CHUNKS)
    return jnp.asarray(_window_bias(valid, npair))


def _sink_rows(sinks, kv, group, t):
    s = (sinks * LOG2E).reshape(kv, group // 2, 1, 2, 1)
    return jnp.broadcast_to(s, (kv, group // 2, t, 2, LANES)).reshape(kv, (group // 2) * t, 2 * LANES)


def kernel(x_prompt, x_sample, c_prompt, c_sample, cache_fox_k, cache_fox_v, cache_fox_logf, cache_swa_k,
           cache_swa_v, ada_w, ada_b, norm_mix_g, norm_ffn_g, fox_w_in, fox_b_f, fox_w_out, swa_w_in,
           swa_sinks, swa_w_out, ffn_w_up, ffn_w_down, final_g):
    bp, s, d = x_prompt.shape
    bs, t, _ = x_sample.shape
    depth = ada_w.shape[0]
    past_len = cache_fox_k.shape[2]
    nh_fox, hd_fox = cache_fox_k.shape[3], cache_fox_k.shape[4]
    buf, kv, hd_swa = cache_swa_k.shape[2], cache_swa_k.shape[3], cache_swa_k.shape[4]
    nh_swa = swa_sinks.shape[1]
    group = nh_swa // kv
    kvd = kv * hd_swa
    rope_dims = hd_swa // 4
    assert hd_fox == LANES and hd_swa == LANES // 2 and group % 2 == 0
    assert buf == SWA_WINDOW_CHUNKS * CHUNK and t <= buf and s % (2 * CHUNK) == 0

    mp, ms = bp * s, bs * t
    tm_p = min(512, s)
    tpb_p = s // tm_p
    tm_fox = min(1024, s)
    tk = min(FOX_TK, s)
    tq = min(FOX_NSUB * tk, s)

    mods = _ada(jnp.concatenate([c_prompt, c_sample], axis=0), ada_w, ada_b)

    def split_mods(i):
        six = jnp.split(mods[i], 6, axis=-1)
        prompt = [m[:bp, None, :] for m in six]
        sample = [jnp.repeat(m[bp:], t, axis=0)[None] for m in six]
        return prompt, sample

    row = lambda v: v.reshape(1, -1)
    xp = x_prompt.reshape(mp, d)
    xs = x_sample.reshape(ms, d)
    pos_p = jnp.arange(s)
    pos_s = past_len + jnp.arange(t)
    tabs_p = _rope_tables(pos_p, hd_swa, rope_dims)
    tabs_s = tuple(jnp.tile(a, (bs, 1)) for a in _rope_tables(pos_s, hd_swa, rope_dims))
    bias_p = _prompt_bias(group // 2)
    bias_s = _sample_bias(past_len, buf, t, group // 2)

    fox_w_in_b, fox_w_out_b, swa_w_in_b, swa_w_out_b, w_up_b, w_down_b = _to_bf16(
        fox_w_in, fox_w_out, swa_w_in, swa_w_out, ffn_w_up, ffn_w_down)

    outs = {k: [] for k in ("fkp", "fvp", "flp", "fks", "fvs", "fls", "skp", "svp", "sks", "svs")}
    for i in range(depth):
        mod_p, mod_s = split_mods(i)
        j = i // 2
        if i % 2 == 0:
            w_f = jnp.pad(fox_w_in_b[j, :, 3 * d:], ((0, 0), (0, LANES - nh_fox)))
            b_f = row(fox_b_f[j])
            qh, kh, vh, k32, v32, lf, n2 = _fox_in(
                xp, row(norm_mix_g[i]), mod_p[0], mod_p[1], fox_w_in_b, w_f, b_f,
                layer=j, tm=tm_fox, tpb=s // tm_fox, head_major=True, nh=nh_fox, hd=hd_fox)
            lft = lf.reshape(bp, s, nh_fox).transpose(0, 2, 1)
            c = _cumsum_rows(lft.reshape(-1, LANES), s // LANES).reshape(bp * nh_fox, s)
            qmax, kmax = _head_norm_bounds(n2, bp, nh_fox, min(FOX_IN_COLS, d) // hd_fox)
            first = _fox_first_blocks(c, qmax, kmax, tk)
            op = _fox_attn(qh, kh, vh, c, first, tk=tk, nsub=tq // tk)
            xp = _proj_res(op.reshape(mp, d), fox_w_out_b, xp, mod_p[2], layer=j, tm=tm_p, tpb=tpb_p)
            outs["fkp"].append(k32.reshape(bp, s, nh_fox, hd_fox))
            outs["fvp"].append(v32.reshape(bp, s, nh_fox, hd_fox))
            outs["flp"].append(lf.reshape(bp, s, nh_fox))
            qs_, k32s, v32s, lfs = _fox_in(xs, row(norm_mix_g[i]), mod_s[0], mod_s[1], fox_w_in_b, w_f, b_f,
                                           layer=j, tm=ms, tpb=1, head_major=False, nh=nh_fox, hd=hd_fox)
            lf_all = jnp.concatenate([cache_fox_logf[j], lfs.reshape(bs, t, nh_fox)], axis=1)
            lc = LANES * int(2 ** np.ceil(np.log2(-(-(past_len + t) // LANES))))
            lf_all = jnp.pad(lf_all.transpose(0, 2, 1), ((0, 0), (0, 0), (0, lc - past_len - t)))
            c_s = _cumsum_rows(lf_all.reshape(-1, LANES), lc // LANES).reshape(bs, nh_fox, lc)
            os_ = _fox_sample_attn(qs_, cache_fox_k, cache_fox_v, k32s, v32s, c_s, op, layer=j, t=t)
            xs = _proj_res(os_, fox_w_out_b, xs, mod_s[2], layer=j, tm=ms, tpb=1)
            outs["fks"].append(k32s.reshape(bs, t, nh_fox, hd_fox))
            outs["fvs"].append(v32s.reshape(bs, t, nh_fox, hd_fox))
            outs["fls"].append(lfs.reshape(bs, t, nh_fox))
        else:
            q, kk, vv, k32, v32 = _swa_in(xp, row(norm_mix_g[i]), mod_p[0], mod_p[1], swa_w_in_b, tabs_p,
                                          layer=j, tm=tm_p, tpb=tpb_p, kvd=kvd, hd=hd_swa, rope_dims=rope_dims)
            op = _swa_attn(q, kk, vv, bias_p, _sink_rows(swa_sinks[j], kv, group, 2 * CHUNK),
                           nb=bp, s=s, kv=kv, group=group, hd=hd_swa)
            xp = _proj_res(op, swa_w_out_b, xp, mod_p[2], layer=j, tm=tm_p, tpb=tpb_p)
            outs["skp"].append(k32.reshape(bp, s, kvd)[:, -buf:].reshape(bp, buf, kv, hd_swa))
            outs["svp"].append(v32.reshape(bp, s, kvd)[:, -buf:].reshape(bp, buf, kv, hd_swa))
            q, kk, vv, k32, v32 = _swa_in(xs, row(norm_mix_g[i]), mod_s[0], mod_s[1], swa_w_in_b, tabs_s,
                                          layer=j, tm=ms, tpb=1, kvd=kvd, hd=hd_swa, rope_dims=rope_dims)
            os_ = _swa_sample_attn(q, cache_swa_k[j].reshape(bs, buf, kvd), cache_swa_v[j].reshape(bs, buf, kvd),
                                   kk, vv, bias_s, _sink_rows(swa_sinks[j], kv, group, t),
                                   kv=kv, group=group, hd=hd_swa, t=t)
            xs = _proj_res(os_, swa_w_out_b, xs, mod_s[2], layer=j, tm=ms, tpb=1)
            k_all = jnp.concatenate([cache_swa_k[j], k32.reshape(bs, t, kv, hd_swa)], axis=1)
            v_all = jnp.concatenate([cache_swa_v[j], v32.reshape(bs, t, kv, hd_swa)], axis=1)
            outs["sks"].append(k_all[:, -buf:])
            outs["svs"].append(v_all[:, -buf:])
        fg = row(final_g) if i == depth - 1 else None
        xp = _ffn(xp, row(norm_ffn_g[i]), mod_p[3], mod_p[4], mod_p[5], w_up_b, w_down_b, fg,
                  layer=i, tm=tm_p, tpb=tpb_p)
        xs = _ffn(xs, row(norm_ffn_g[i]), mod_s[3], mod_s[4], mod_s[5], w_up_b, w_down_b, fg,
                  layer=i, tm=ms, tpb=1)

    st = lambda k: jnp.stack(outs[k])
    return (xp.reshape(bp, s, d), xs.reshape(bs, t, d),
            st("fkp"), st("fvp"), st("flp"), st("fks"), st("fvs"), st("fls"),
            st("skp"), st("svp"), st("sks"), st("svs"))
```

```python
import functools

import numpy as np
import jax
import jax.numpy as jnp
from jax import lax
from jax.experimental import pallas as pl
from jax.experimental.pallas import tpu as pltpu

F32 = jnp.float32
BF16 = jnp.bfloat16

RMS_EPS = 1e-6
CHUNK = 64
SWA_WINDOW_CHUNKS = 2
ROPE_THETA = 500000.0
LANES = 128
SUBLANES = 8
NEG = -1e30
NEG_BLOCK = -3e38
VMEM_LIMIT_BYTES = 56 * 1024 * 1024
LOG2E = 1.4426950408889634
FOX_SKIP_LOG2 = 136.0
FFN_STEP_ELEMS = 512 * 1024
SWA_TILES_PER_STEP = 8
FOX_IN_COLS = 512
FOX_TK = 512
FOX_NSUB = 16


def _cp(n_axes):
    return pltpu.CompilerParams(dimension_semantics=("arbitrary",) * n_axes,
                                vmem_limit_bytes=VMEM_LIMIT_BYTES)


def _dot(a, b):
    return jnp.dot(a, b, preferred_element_type=F32)


def _dot_nt(a, b):
    return lax.dot_general(a, b, (((1,), (1,)), ((), ())), preferred_element_type=F32)


def _rms(x):
    return x * lax.rsqrt(jnp.mean(x * x, axis=-1, keepdims=True) + RMS_EPS)


NORM_ROWS = 32


def _norm_mod_store(h_ref, x_ref, g_ref, sh_ref, sc_ref):
    per_token = sh_ref.shape[0] > 1
    g = g_ref[...]
    if not per_token:
        gain, shift = g * (1.0 + sc_ref[...]), sh_ref[...]

    def body(i, carry):
        rows = pl.ds(pl.multiple_of(i * NORM_ROWS, NORM_ROWS), NORM_ROWS)
        x = x_ref[rows, :]
        if per_token:
            gain_, shift_ = g * (1.0 + sc_ref[rows, :]), sh_ref[rows, :]
        else:
            gain_, shift_ = gain, shift
        h_ref[rows, :] = (_rms(x) * gain_ + shift_).astype(h_ref.dtype)
        return carry

    lax.fori_loop(0, x_ref.shape[0] // NORM_ROWS, body, 0, unroll=4)


def _log_sigmoid(z):
    return jnp.minimum(z, 0.0) - jnp.log1p(jnp.exp(-jnp.abs(z)))


CAST_BLOCK_BYTES = 4 * 1024 * 1024


def _cast_kernel(*refs):
    n = len(refs) // 2
    for x_ref, o_ref in zip(refs[:n], refs[n:]):
        o_ref[...] = x_ref[...].astype(o_ref.dtype)


def _to_bf16(*ws):
    steps = 1
    while any(w.size * 4 // steps > CAST_BLOCK_BYTES for w in ws):
        steps *= 2
    specs = []
    for w in ws:
        layers, rows, cols = w.shape
        per_layer = steps // layers
        assert per_layer >= 1 and rows % (16 * per_layer) == 0
        specs.append(pl.BlockSpec((None, rows // per_layer, cols),
                                  lambda i, per_layer=per_layer: (i // per_layer, i % per_layer, 0)))
    return pl.pallas_call(
        _cast_kernel, grid=(steps,), in_specs=specs, out_specs=specs,
        out_shape=[jax.ShapeDtypeStruct(w.shape, BF16) for w in ws],
        compiler_params=_cp(1), name="weight_to_bf16",
    )(*ws)


def _ada_kernel(c_ref, w_ref, b_ref, o_ref):
    c = c_ref[...]
    a = (c / (1.0 + jnp.exp(-c))).astype(BF16)
    o_ref[...] = _dot(a, w_ref[...].astype(BF16)) + b_ref[...]


def _ada(c_all, ada_w, ada_b):
    depth, d, n6 = ada_w.shape
    r = c_all.shape[0]
    tn = min(1024, n6)
    return pl.pallas_call(
        _ada_kernel,
        grid=(depth, n6 // tn),
        in_specs=[pl.BlockSpec((r, d), lambda l, n: (0, 0)),
                  pl.BlockSpec((None, d, tn), lambda l, n: (l, 0, n)),
                  pl.BlockSpec((None, 1, tn), lambda l, n: (l, 0, n))],
        out_specs=pl.BlockSpec((None, r, tn), lambda l, n: (l, 0, n)),
        out_shape=jax.ShapeDtypeStruct((depth, r, n6), F32),
        compiler_params=_cp(2), name="ada_params",
    )(c_all, ada_w, ada_b.reshape(depth, 1, n6))


def _fox_in_kernel(x_ref, g_ref, sh_ref, sc_ref, wq_ref, wk_ref, wv_ref, wf_ref, bf_ref, *rest,
                   head_major, hd, nh, q_scale):
    if head_major:
        qb_ref, kb_ref, vb_ref, k_ref, v_ref, lf_ref, n2_ref, h_ref = rest
    else:
        qb_ref, k_ref, v_ref, lf_ref, h_ref = rest

    @pl.when(pl.program_id(1) == 0)
    def _():
        _norm_mod_store(h_ref, x_ref, g_ref, sh_ref, sc_ref)
        z = _dot(h_ref[...], wf_ref[...])[:, :nh] + bf_ref[...]
        lf_ref[...] = _log_sigmoid(z)

    hb = h_ref[...]
    q = _dot(hb, wq_ref[...]) * q_scale
    k = _dot(hb, wk_ref[...])
    v = _dot(hb, wv_ref[...])
    k_ref[...] = k
    v_ref[...] = v
    if head_major:
        for j in range(q.shape[1] // hd):
            sl = slice(j * hd, (j + 1) * hd)
            qb_ref[j] = q[:, sl].astype(BF16)
            kb_ref[j] = k[:, sl].astype(BF16)
            vb_ref[j] = v[:, sl].astype(BF16)
        tm, tn = q.shape
        grp = lax.shift_right_logical(lax.broadcasted_iota(jnp.int32, (2 * tn, LANES), 0), hd.bit_length() - 1)
        sel = jnp.where(grp == lax.broadcasted_iota(jnp.int32, (2 * tn, LANES), 1), 1.0, 0.0).astype(BF16)
        n2 = _dot(jnp.concatenate([q * q, k * k], axis=1).astype(BF16), sel)
        n2_ref[...] = jnp.max(n2.reshape(tm // SUBLANES, SUBLANES, LANES), axis=0)
    else:
        qb_ref[...] = q.astype(BF16)


def _mod_spec(mod, tpb, cols=None):
    _, rows, d = mod.shape
    if cols is None:
        return pl.BlockSpec((None, rows, d), lambda i, n: (i // tpb, 0, 0))
    return pl.BlockSpec((None, rows, cols), lambda i, n: (i // tpb, 0, n))


def _fox_in(x, g, shift, scale, w_in, w_f, b_f, *, layer, tm, tpb, head_major, nh, hd):
    m, d = x.shape
    tn = min(FOX_IN_COLS, d)
    nt = d // tn
    hpt = tn // hd
    nb, s = m // (tm * tpb), tm * tpb
    in_specs = [pl.BlockSpec((tm, d), lambda i, n: (i, 0)),
                pl.BlockSpec((1, d), lambda i, n: (0, 0)),
                _mod_spec(shift, tpb), _mod_spec(scale, tpb),
                pl.BlockSpec((None, d, tn), lambda i, n: (layer, 0, n)),
                pl.BlockSpec((None, d, tn), lambda i, n: (layer, 0, n + nt)),
                pl.BlockSpec((None, d, tn), lambda i, n: (layer, 0, n + 2 * nt)),
                pl.BlockSpec((d, LANES), lambda i, n: (0, 0)),
                pl.BlockSpec((1, nh), lambda i, n: (0, 0))]
    rm_spec = pl.BlockSpec((tm, tn), lambda i, n: (i, n))
    lf_spec = pl.BlockSpec((tm, nh), lambda i, n: (i, 0))
    if head_major:
        hm_spec = pl.BlockSpec((None, hpt, tm, hd), lambda i, n: (i // tpb, n, i % tpb, 0))
        hm_shape = jax.ShapeDtypeStruct((nb, nh, s, hd), BF16)
        assert hd & (hd - 1) == 0 and 2 * hpt <= LANES
        n2_spec = pl.BlockSpec((SUBLANES, LANES), lambda i, n: (i, n))
        out_specs = [hm_spec, hm_spec, hm_spec, rm_spec, rm_spec, lf_spec, n2_spec]
        out_shape = [hm_shape, hm_shape, hm_shape]
    else:
        out_specs = [rm_spec, rm_spec, rm_spec, lf_spec]
        out_shape = [jax.ShapeDtypeStruct((m, d), BF16)]
    out_shape += [jax.ShapeDtypeStruct((m, d), F32), jax.ShapeDtypeStruct((m, d), F32),
                  jax.ShapeDtypeStruct((m, nh), F32)]
    if head_major:
        out_shape += [jax.ShapeDtypeStruct((m // tm * SUBLANES, nt * LANES), F32)]
    return pl.pallas_call(
        functools.partial(_fox_in_kernel, head_major=head_major, hd=hd, nh=nh, q_scale=hd ** -0.5 * LOG2E),
        grid=(m // tm, nt), in_specs=in_specs, out_specs=out_specs, out_shape=out_shape,
        scratch_shapes=[pltpu.VMEM((tm, d), BF16)],
        compiler_params=_cp(2), name="fox_in_proj",
    )(x, g, shift, scale, w_in, w_in, w_in, w_f, b_f)


def _split3(x):
    x1 = x.astype(BF16)
    r1 = x - x1.astype(F32)
    x2 = r1.astype(BF16)
    x3 = (r1 - x2.astype(F32)).astype(BF16)
    return x1, x2, x3


CUMSUM_BLOCK_ROWS = 1024


def _cumsum_kernel(x_ref, u_ref, l_ref, o_ref):
    u = u_ref[...]
    lo = l_ref[...]
    y = sum(_dot(p, u) for p in _split3(x_ref[...]))
    tot = jnp.broadcast_to(y[:, LANES - 1:LANES], y.shape)
    o_ref[...] = (y + sum(_dot(lo, p) for p in _split3(tot))) * LOG2E


def _cumsum_rows(x, nc):
    rows = x.shape[0]
    assert LANES % nc == 0 or nc % LANES == 0
    n = max(nc, LANES)
    assert rows % n == 0
    while 2 * n <= CUMSUM_BLOCK_ROWS and rows % (2 * n) == 0:
        n *= 2
    u = jnp.asarray(np.triu(np.ones((LANES, LANES), np.float32)), BF16)
    idx = np.arange(n)
    same = (idx[:, None] // nc) == (idx[None, :] // nc)
    lo = jnp.asarray((same & (idx[None, :] < idx[:, None])).astype(np.float32), BF16)
    return pl.pallas_call(
        _cumsum_kernel, grid=(rows // n,),
        in_specs=[pl.BlockSpec((n, LANES), lambda i: (i, 0)),
                  pl.BlockSpec((LANES, LANES), lambda i: (0, 0)),
                  pl.BlockSpec((n, n), lambda i: (0, 0))],
        out_specs=pl.BlockSpec((n, LANES), lambda i: (i, 0)),
        out_shape=jax.ShapeDtypeStruct(x.shape, F32),
        compiler_params=_cp(1), name="logf_cumsum",
    )(x, u, lo)


NORM_SLACK = 1.01


def _head_norm_bounds(n2, nb, nh, hpt):
    n2 = jnp.max(n2.reshape(nb, -1, nh // hpt, LANES), axis=1)
    bound = lambda part: (NORM_SLACK * jnp.sqrt(part)).reshape(-1)
    return bound(n2[:, :, :hpt]), bound(n2[:, :, hpt:2 * hpt])


def _fox_first_blocks(c, qmax, kmax, tk):
    bh, s = c.shape
    cb = c.reshape(bh, s // tk, tk)
    c_start, c_end = cb[:, :, 0], cb[:, :, -1]
    budget = (2.0 * qmax * kmax + FOX_SKIP_LOG2)[:, None, None]
    skippable = (c_end[:, None, :] - c_start[:, :, None]) > budget
    earlier = np.tril(np.ones((s // tk, s // tk), bool), -1)
    return jnp.sum(skippable & earlier, axis=-1).astype(jnp.int32).reshape(-1)


def _fox_attn_kernel(first_ref, q_ref, k_ref, v_ref, c_ref, o_ref, m_ref, acc_ref, *, tk, nsub, hd):
    b, h, qi = pl.program_id(0), pl.program_id(1), pl.program_id(2)
    nkb = pl.num_programs(2) * nsub
    base = qi * nsub
    @pl.when(jnp.logical_and(jnp.logical_and(b == 0, h == 0), qi == 0))
    def _():
        m_ref[...] = jnp.full(m_ref.shape, NEG, F32)
        acc_ref[...] = jnp.zeros(acc_ref.shape, F32)

    ones = jnp.ones((tk, hd), BF16)
    ncol = tk // LANES
    c_first = [c_ref[base + r][:, 0:1] for r in range(nsub)]

    def step(r, j, mask=None, valid=None):
        rows = pl.ds(pl.multiple_of(j * tk, tk), tk)
        c_row = c_ref[j]
        if valid is not None:
            c_row = jnp.where(valid, c_row, -NEG_BLOCK)
        s = _dot_nt(q_ref[r * tk:(r + 1) * tk, :], k_ref[rows, :]) + (c_first[r] - c_row)
        if mask is not None:
            s = jnp.where(mask, s, NEG)
        cols = [s[:, c * LANES:(c + 1) * LANES] for c in range(ncol)]
        m_cur = functools.reduce(jnp.maximum, cols)
        m_prev = m_ref[r]
        m_next = jnp.maximum(m_prev, jnp.max(m_cur, axis=-1, keepdims=True))
        alpha = jnp.exp2(m_prev - m_next)
        p = jnp.concatenate([jnp.exp2(col - m_next) for col in cols], axis=1).astype(BF16)
        pv = _dot(p, jnp.concatenate([v_ref[rows, :], ones], axis=1))
        acc_ref[r] = jnp.concatenate([alpha, alpha], axis=1) * acc_ref[r] + pv
        m_ref[r] = m_next

    off = (b * pl.num_programs(1) + h) * nkb + base
    n = functools.reduce(jnp.maximum, [base + r - first_ref[off + r] for r in range(nsub)])

    def body(t, carry):
        for r in range(nsub):
            j = base + r - n + t
            step(r, jnp.maximum(j, 0), valid=j >= 0)
        return carry

    lax.fori_loop(0, n, body, 0)
    row = lax.broadcasted_iota(jnp.int32, (tk, tk), 0)
    col = lax.broadcasted_iota(jnp.int32, (tk, tk), 1)
    causal = col <= row
    for r in range(nsub):
        step(r, base + r, mask=causal)
    for r in range(nsub):
        acc = acc_ref[r]
        o_ref[r * tk:(r + 1) * tk, :] = (acc[:, :hd] / acc[:, hd:]).astype(o_ref.dtype)
        acc_ref[r] = jnp.zeros(acc.shape, F32)
        m_ref[r] = jnp.full(m_ref.shape[1:], NEG, F32)


def _fox_attn(qh, kh, vh, c, first, *, tk, nsub):
    nb, nh, s, hd = qh.shape
    tq = tk * nsub
    nq = s // tq
    kv_spec = pl.BlockSpec((None, None, s, hd), lambda b, h, i, f: (b, h, 0, 0))
    return pl.pallas_call(
        functools.partial(_fox_attn_kernel, tk=tk, nsub=nsub, hd=hd),
        grid_spec=pltpu.PrefetchScalarGridSpec(
            num_scalar_prefetch=1, grid=(nb, nh, nq),
            in_specs=[pl.BlockSpec((None, None, tq, hd), lambda b, h, i, f: (b, h, i, 0)),
                      kv_spec, kv_spec,
                      pl.BlockSpec((None, s // tk, 1, tk), lambda b, h, i, f: (b * nh + h, 0, 0, 0))],
            out_specs=pl.BlockSpec((None, tq, hd), lambda b, h, i, f: (b, i, h)),
            scratch_shapes=[pltpu.VMEM((nsub, tk, LANES), F32), pltpu.VMEM((nsub, tk, 2 * hd), F32)]),
        out_shape=jax.ShapeDtypeStruct((nb, s, nh * hd), BF16),
        compiler_params=_cp(3), name="fox_attention",
    )(first, qh, kh, vh, c.reshape(nb * nh, s // tk, 1, tk))


def _fox_sample_kernel(q_ref, kc_ref, vc_ref, kn_ref, vn_ref, c_ref, cf_ref, after_ref, o_ref, *, hb, hd, p_len, t):
    del after_ref
    n = p_len * hb
    kf = kc_ref[...].reshape(n, hd).astype(BF16)
    vf = vc_ref[...].reshape(n, hd).astype(BF16)
    heads = [slice(j * hd, (j + 1) * hd) for j in range(hb)]
    q8 = jnp.concatenate([q_ref[:, sl] for sl in heads], axis=0)
    row_head = lax.shift_right_logical(lax.broadcasted_iota(jnp.int32, (hb * t, LANES), 0), t.bit_length() - 1)
    col_head = lax.broadcasted_iota(jnp.int32, (hb * t, LANES), 1) & (hb - 1)
    own = jnp.where(row_head == col_head, 0.0, NEG)
    s = _dot_nt(q8, kf)
    cf = cf_ref[...]
    cols = [s[:, g * LANES:(g + 1) * LANES] - cf[:, g * LANES:(g + 1) * LANES] + own for g in range(n // LANES)]
    tri = lax.broadcasted_iota(jnp.int32, (t, t), 1) <= lax.broadcasted_iota(jnp.int32, (t, t), 0)
    s_n = jnp.concatenate(
        [jnp.where(tri, _dot_nt(q_ref[:, sl], kn_ref[:, sl].astype(BF16)) - c_ref[j:j + 1, p_len:p_len + t], NEG)
         for j, sl in enumerate(heads)], axis=0)
    m = jnp.maximum(jnp.max(functools.reduce(jnp.maximum, cols), axis=-1, keepdims=True),
                    jnp.max(s_n, axis=-1, keepdims=True))
    ps = [jnp.exp2(col - m) for col in cols]
    p_n = jnp.exp2(s_n - m)
    den = jnp.sum(functools.reduce(jnp.add, ps), axis=-1, keepdims=True) + jnp.sum(p_n, axis=-1, keepdims=True)
    o = _dot(jnp.concatenate(ps, axis=1).astype(BF16), vf)
    o_n = jnp.concatenate([_dot(p_n[j * t:(j + 1) * t].astype(BF16), vn_ref[:, sl].astype(BF16))
                           for j, sl in enumerate(heads)], axis=0)
    o = (o + o_n) / den
    for j, sl in enumerate(heads):
        o_ref[:, sl] = o[j * t:(j + 1) * t].astype(o_ref.dtype)


def _fox_sample_attn(q, k_cache, v_cache, k_new, v_new, c_all, after, *, layer, t):
    _, nb, p_len, nh, hd = k_cache.shape
    hb = min(SUBLANES, nh)
    assert t & (t - 1) == 0 and hb & (hb - 1) == 0 and (p_len * hb) % LANES == 0 and LANES % hb == 0
    w = hb * hd
    lc = c_all.shape[-1]
    c_flat = c_all[:, :, :p_len].reshape(nb, nh // hb, hb, p_len).transpose(0, 1, 3, 2)
    c_flat = c_flat.reshape(nb, nh // hb, 1, p_len * hb)
    cache_spec = pl.BlockSpec((None, None, p_len, hb, hd), lambda b, h: (layer, b, 0, h, 0))
    row_spec = pl.BlockSpec((t, w), lambda b, h: (b, h))
    return pl.pallas_call(
        functools.partial(_fox_sample_kernel, hb=hb, hd=hd, p_len=p_len, t=t),
        grid=(nb, nh // hb),
        in_specs=[row_spec, cache_spec, cache_spec, row_spec, row_spec,
                  pl.BlockSpec((None, hb, lc), lambda b, h: (b, h, 0)),
                  pl.BlockSpec((None, None, 1, p_len * hb), lambda b, h: (b, h, 0, 0)),
                  pl.BlockSpec(memory_space=pl.ANY)],
        out_specs=row_spec,
        out_shape=jax.ShapeDtypeStruct((nb * t, nh * hd), BF16),
        compiler_params=_cp(2), name="fox_sample_attention",
    )(q, k_cache, v_cache, k_new, v_new, c_all, c_flat, after)


def _proj_res_kernel(a_ref, w_ref, x_ref, gate_ref, o_ref):
    o_ref[...] = x_ref[...] + gate_ref[...] * _dot(a_ref[...], w_ref[...])


def _proj_res(a, w, x, gate, *, layer, tm, tpb):
    m, k = a.shape
    n = w.shape[2]
    return pl.pallas_call(
        _proj_res_kernel, grid=(m // tm, 1),
        in_specs=[pl.BlockSpec((tm, k), lambda i, j: (i, 0)),
                  pl.BlockSpec((None, k, n), lambda i, j: (layer, 0, 0)),
                  pl.BlockSpec((tm, n), lambda i, j: (i, 0)),
                  _mod_spec(gate, tpb)],
        out_specs=pl.BlockSpec((tm, n), lambda i, j: (i, 0)),
        out_shape=jax.ShapeDtypeStruct((m, n), F32),
        compiler_params=_cp(2), name="attn_out_proj",
    )(a, w, x, gate)


def _ffn_kernel(x_ref, g_ref, sh_ref, sc_ref, gate_ref, wu_ref, wd_ref, *rest, final):
    if final:
        fg_ref, o_ref, h_ref, acc_ref = rest
    else:
        o_ref, h_ref, acc_ref = rest
    f = pl.program_id(1)

    @pl.when(f == 0)
    def _():
        _norm_mod_store(h_ref, x_ref, g_ref, sh_ref, sc_ref)
        acc_ref[...] = jnp.zeros(acc_ref.shape, F32)

    a = jnp.maximum(_dot(h_ref[...], wu_ref[...]), 0.0)
    acc_ref[...] += _dot((a * a).astype(BF16), wd_ref[...])

    @pl.when(f == pl.num_programs(1) - 1)
    def _():
        y = x_ref[...] + gate_ref[...] * acc_ref[...]
        if final:
            y = _rms(y) * fg_ref[...]
        o_ref[...] = y


def _ffn(x, g, shift, scale, gate, w_up, w_down, final_g, *, layer, tm, tpb):
    m, d = x.shape
    ff = w_up.shape[2]
    tf = min(ff, FFN_STEP_ELEMS // tm)
    vec = pl.BlockSpec((1, d), lambda i, f: (0, 0))
    in_specs = [pl.BlockSpec((tm, d), lambda i, f: (i, 0)), vec,
                _mod_spec(shift, tpb), _mod_spec(scale, tpb), _mod_spec(gate, tpb),
                pl.BlockSpec((None, d, tf), lambda i, f: (layer, 0, f)),
                pl.BlockSpec((None, tf, d), lambda i, f: (layer, f, 0))]
    args = [x, g, shift, scale, gate, w_up, w_down]
    if final_g is not None:
        in_specs.append(vec)
        args.append(final_g)
    return pl.pallas_call(
        functools.partial(_ffn_kernel, final=final_g is not None),
        grid=(m // tm, ff // tf), in_specs=in_specs,
        out_specs=pl.BlockSpec((tm, d), lambda i, f: (i, 0)),
        out_shape=jax.ShapeDtypeStruct((m, d), F32),
        scratch_shapes=[pltpu.VMEM((tm, d), BF16), pltpu.VMEM((tm, d), F32)],
        compiler_params=_cp(2), name="ffn_final" if final_g is not None else "ffn",
    )(*args)


def _rope(r, a, b, c, half):
    cols = []
    for j in range(r.shape[1] // LANES):
        x = r[:, j * LANES:(j + 1) * LANES]
        cols.append(x * a + pltpu.roll(x, LANES - half, 1) * b + pltpu.roll(x, half, 1) * c)
    return jnp.concatenate(cols, axis=1)


def _dup_halves(x, hd):
    lo = lax.broadcasted_iota(jnp.int32, (x.shape[0], LANES), 1) < hd
    zero = jnp.zeros((x.shape[0], LANES), x.dtype)
    out = []
    for j in range(x.shape[1] // LANES):
        p = x[:, j * LANES:(j + 1) * LANES]
        r = pltpu.roll(p, hd, 1)
        out += [jnp.where(lo, p, zero), jnp.where(lo, zero, r), jnp.where(lo, r, zero), jnp.where(lo, zero, p)]
    return jnp.concatenate(out, axis=1)


def _swa_in_kernel(x_ref, g_ref, sh_ref, sc_ref, w_ref, ra_ref, rb_ref, rc_ref,
                   q_ref, kk_ref, vv_ref, k_ref, v_ref, h_ref, *, kvd, hd, rope_half, q_scale):
    _norm_mod_store(h_ref, x_ref, g_ref, sh_ref, sc_ref)
    h = h_ref[...]
    tabs = (ra_ref[...], rb_ref[...], rc_ref[...], rope_half)
    d = q_ref.shape[1]
    tn = 2 * kvd
    for n in range(d // tn):
        r = _dot(h, w_ref[:, n * tn:(n + 1) * tn])
        q_ref[:, n * tn:(n + 1) * tn] = (_rope(r, *tabs) * q_scale).astype(BF16)
    r = _dot(h, w_ref[:, d:])
    k = _rope(r[:, :kvd], *tabs)
    v = r[:, kvd:]
    k_ref[...] = k
    v_ref[...] = v
    kk_ref[...] = _dup_halves(k, hd).astype(BF16)
    vv_ref[...] = _dup_halves(v, hd).astype(BF16)


def _swa_in(x, g, shift, scale, w, tabs, *, layer, tm, tpb, kvd, hd, rope_dims):
    m, d = x.shape
    ntab = tabs[0].shape[0] // tm
    kkw = (kvd // hd) * 2 * LANES
    tab_spec = pl.BlockSpec((tm, LANES), lambda i, n: (i % ntab, 0))
    const = lambda w_: pl.BlockSpec((tm, w_), lambda i, n: (i, 0))
    return pl.pallas_call(
        functools.partial(_swa_in_kernel, kvd=kvd, hd=hd, rope_half=rope_dims // 2, q_scale=hd ** -0.5 * LOG2E),
        grid=(m // tm, 1),
        in_specs=[pl.BlockSpec((tm, d), lambda i, n: (i, 0)),
                  pl.BlockSpec((1, d), lambda i, n: (0, 0)),
                  _mod_spec(shift, tpb), _mod_spec(scale, tpb),
                  pl.BlockSpec((None, d, d + 2 * kvd), lambda i, n: (layer, 0, 0)),
                  tab_spec, tab_spec, tab_spec],
        out_specs=[const(d), const(kkw), const(kkw), const(kvd), const(kvd)],
        out_shape=[jax.ShapeDtypeStruct((m, d), BF16),
                   jax.ShapeDtypeStruct((m, kkw), BF16), jax.ShapeDtypeStruct((m, kkw), BF16),
                   jax.ShapeDtypeStruct((m, kvd), F32), jax.ShapeDtypeStruct((m, kvd), F32)],
        scratch_shapes=[pltpu.VMEM((tm, d), BF16)],
        compiler_params=_cp(2), name="swa_in_proj",
    )(x, g, shift, scale, w, *tabs)


def _rope_tables(pos, hd, rope_dims):
    half = rope_dims // 2
    inv_freq = ROPE_THETA ** (-jnp.arange(half, dtype=F32) * 2.0 / rope_dims)
    ang = pos.astype(F32)[:, None] * inv_freq[None, :]
    cos, sin = jnp.cos(ang), jnp.sin(ang)
    lane = np.arange(LANES) % hd
    idx = lane % half
    a = jnp.where(lane < rope_dims, cos[:, idx], 1.0)
    b = jnp.where(lane < half, -sin[:, idx], 0.0)
    c = jnp.where((lane >= half) & (lane < rope_dims), sin[:, idx], 0.0)
    return a, b, c


def _swa_core(q, k_top, k_bot, v_top, v_bot, bias, sinks, *, group):
    t = q.shape[0]
    npair = group // 2
    win = k_top.shape[0]
    half_lanes = lax.broadcasted_iota(jnp.int32, (win, LANES), 1) < LANES // 2
    count_even = jnp.where(half_lanes, 1.0, 0.0).astype(BF16)
    count_odd = jnp.where(half_lanes, 0.0, 1.0).astype(BF16)
    v2 = jnp.concatenate([jnp.concatenate([v_top, count_even], axis=1),
                          jnp.concatenate([v_bot, count_odd], axis=1)], axis=0)
    qs = jnp.concatenate([q[:, p * LANES:(p + 1) * LANES] for p in range(npair)], axis=0)
    s = _dot_nt(qs, jnp.concatenate([k_top, k_bot], axis=0)) + bias
    ps, es = [], []
    for half in range(2):
        cols = [s[:, half * win + c * LANES: half * win + (c + 1) * LANES] for c in range(win // LANES)]
        sink = sinks[:, half * LANES:(half + 1) * LANES]
        m = jnp.maximum(jnp.max(functools.reduce(jnp.maximum, cols), axis=-1, keepdims=True), sink)
        ps += [jnp.exp2(col - m) for col in cols]
        es.append(jnp.exp2(sink - m))
    o = _dot(jnp.concatenate(ps, axis=1).astype(BF16), v2)
    lane = lax.broadcasted_iota(jnp.int32, (o.shape[0], LANES), 1)
    o = o[:, :LANES] / (o[:, LANES:] + jnp.where(lane < LANES // 2, es[0], es[1]))
    return jnp.concatenate([o[p * t:(p + 1) * t] for p in range(npair)], axis=1)


def _swa_attn_kernel(q_ref, kh_ref, km_ref, vh_ref, vm_ref, bias_ref, sink_ref, o_ref, *, kv, group, hd, ns):
    t = 2 * CHUNK
    gw = group * hd
    first_bias = bias_ref[jnp.minimum(pl.program_id(1), 1)]
    for u in range(ns):
        rows = slice(u * t, (u + 1) * t)
        for g in range(kv):
            def win(halo, main, off):
                sl = slice(g * 2 * LANES + off, g * 2 * LANES + off + LANES)
                before = halo[:, sl] if u == 0 else main[(u - 1) * t:u * t, sl]
                return jnp.concatenate([before, main[rows, sl]], axis=0)
            o = _swa_core(q_ref[rows, g * gw:(g + 1) * gw],
                          win(kh_ref, km_ref, 0), win(kh_ref, km_ref, LANES),
                          win(vh_ref, vm_ref, 0), win(vh_ref, vm_ref, LANES),
                          first_bias if u == 0 else bias_ref[1], sink_ref[g], group=group)
            o_ref[rows, g * gw:(g + 1) * gw] = o.astype(o_ref.dtype)


def _swa_attn(q, kk, vv, bias2, sink_rows, *, nb, s, kv, group, hd):
    t = 2 * CHUNK
    ns = SWA_TILES_PER_STEP if (s // t) % SWA_TILES_PER_STEP == 0 else 1
    nstep = s // (ns * t)
    d = q.shape[1]
    kkw = kk.shape[1]
    main = pl.BlockSpec((ns * t, kkw), lambda b, i: (b * nstep + i, 0))
    halo = pl.BlockSpec((t, kkw), lambda b, i: (b * nstep * ns + jnp.maximum(i * ns - 1, 0), 0))
    full = lambda a: pl.BlockSpec(a.shape, lambda b, i: (0,) * a.ndim)
    return pl.pallas_call(
        functools.partial(_swa_attn_kernel, kv=kv, group=group, hd=hd, ns=ns),
        grid=(nb, nstep),
        in_specs=[pl.BlockSpec((ns * t, d), lambda b, i: (b * nstep + i, 0)),
                  halo, main, halo, main, full(bias2), full(sink_rows)],
        out_specs=pl.BlockSpec((ns * t, d), lambda b, i: (b * nstep + i, 0)),
        out_shape=jax.ShapeDtypeStruct(q.shape, BF16),
        compiler_params=_cp(2), name="swa_attention",
    )(q, kk, kk, vv, vv, bias2, sink_rows)


def _swa_sample_kernel(q_ref, kc_ref, vc_ref, kkn_ref, vvn_ref, bias_ref, sink_ref, o_ref, *, kv, group, hd, t):
    gw = group * hd
    kkc = _dup_halves(kc_ref[...], hd).astype(BF16)
    vvc = _dup_halves(vc_ref[...], hd).astype(BF16)
    pad = jnp.zeros((kc_ref.shape[0] - t, LANES), BF16)
    for g in range(kv):
        def win(cache, new, off):
            sl = slice(g * 2 * LANES + off, g * 2 * LANES + off + LANES)
            return jnp.concatenate([cache[:, sl], new[:, sl], pad], axis=0)
        o = _swa_core(q_ref[:, g * gw:(g + 1) * gw],
                      win(kkc, kkn_ref, 0), win(kkc, kkn_ref, LANES),
                      win(vvc, vvn_ref, 0), win(vvc, vvn_ref, LANES),
                      bias_ref[...], sink_ref[g], group=group)
        o_ref[:, g * gw:(g + 1) * gw] = o.astype(o_ref.dtype)


def _swa_sample_attn(q, k_cache, v_cache, kk_new, vv_new, bias, sink_rows, *, kv, group, hd, t):
    nb, buf, kvd = k_cache.shape
    d = q.shape[1]
    full = lambda a: pl.BlockSpec(a.shape, lambda b: (0,) * a.ndim)
    cache_spec = pl.BlockSpec((None, buf, kvd), lambda b: (b, 0, 0))
    new_spec = pl.BlockSpec((t, kk_new.shape[1]), lambda b: (b, 0))
    return pl.pallas_call(
        functools.partial(_swa_sample_kernel, kv=kv, group=group, hd=hd, t=t),
        grid=(nb,),
        in_specs=[pl.BlockSpec((t, d), lambda b: (b, 0)), cache_spec, cache_spec, new_spec, new_spec,
                  full(bias), full(sink_rows)],
        out_specs=pl.BlockSpec((t, d), lambda b: (b, 0)),
        out_shape=jax.ShapeDtypeStruct(q.shape, BF16),
        compiler_params=_cp(1), name="swa_sample_attention",
    )(q, k_cache, v_cache, kk_new, vv_new, bias, sink_rows)


def _window_bias(valid, npair):
    b = np.where(valid, 0.0, NEG).astype(np.float32)
    return np.tile(b, (npair, 2))


def _prompt_bias(npair):
    t = 2 * CHUNK
    qc = np.arange(t)[:, None] // CHUNK
    kc = np.arange(t + SWA_WINDOW_CHUNKS * CHUNK)[None, :] // CHUNK
    valid = (kc >= qc) & (kc <= qc + SWA_WINDOW_CHUNKS)
    first = valid & (kc >= SWA_WINDOW_CHUNKS)
    return jnp.asarray(np.stack([_window_bias(first, npair), _window_bias(valid, npair)]))


def _sample_bias(past_len, buf, t, npair):
    q_pos = past_len + np.arange(t)
    k_pos = np.concatenate([past_len - buf + np.arange(buf), q_pos])
    qch, kch = q_pos // CHUNK, k_pos // CHUNK
    valid = np.zeros((t, 2 * buf), bool)
    valid[:, :buf + t] = (kch[None, :] <= qch[:, None]) & (kch[None, :] >= qch[:, None] - SWA_WINDOW_CHUNKS)
    return jnp.asarray(_window_bias(valid, npair))


def _sink_rows(sinks, kv, group, t):
    s = (sinks * LOG2E).reshape(kv, group // 2, 1, 2, 1)
    return jnp.broadcast_to(s, (kv, group // 2, t, 2, LANES)).reshape(kv, (group // 2) * t, 2 * LANES)


def kernel(x_prompt, x_sample, c_prompt, c_sample, cache_fox_k, cache_fox_v, cache_fox_logf, cache_swa_k,
           cache_swa_v, ada_w, ada_b, norm_mix_g, norm_ffn_g, fox_w_in, fox_b_f, fox_w_out, swa_w_in,
           swa_sinks, swa_w_out, ffn_w_up, ffn_w_down, final_g):
    bp, s, d = x_prompt.shape
    bs, t, _ = x_sample.shape
    depth = ada_w.shape[0]
    past_len = cache_fox_k.shape[2]
    nh_fox, hd_fox = cache_fox_k.shape[3], cache_fox_k.shape[4]
    buf, kv, hd_swa = cache_swa_k.shape[2], cache_swa_k.shape[3], cache_swa_k.shape[4]
    nh_swa = swa_sinks.shape[1]
    group = nh_swa // kv
    kvd = kv * hd_swa
    rope_dims = hd_swa // 4
    assert hd_fox == LANES and hd_swa == LANES // 2 and group % 2 == 0
    assert buf == SWA_WINDOW_CHUNKS * CHUNK and t <= buf and s % (2 * CHUNK) == 0

    mp, ms = bp * s, bs * t
    tm_p = min(512, s)
    tpb_p = s // tm_p
    tm_fox = min(1024, s)
    tk = min(FOX_TK, s)
    tq = min(FOX_NSUB * tk, s)

    mods = _ada(jnp.concatenate([c_prompt, c_sample], axis=0), ada_w, ada_b)

    def split_mods(i):
        six = jnp.split(mods[i], 6, axis=-1)
        prompt = [m[:bp, None, :] for m in six]
        sample = [jnp.repeat(m[bp:], t, axis=0)[None] for m in six]
        return prompt, sample

    row = lambda v: v.reshape(1, -1)
    xp = x_prompt.reshape(mp, d)
    xs = x_sample.reshape(ms, d)
    pos_p = jnp.arange(s)
    pos_s = past_len + jnp.arange(t)
    tabs_p = _rope_tables(pos_p, hd_swa, rope_dims)
    tabs_s = tuple(jnp.tile(a, (bs, 1)) for a in _rope_tables(pos_s, hd_swa, rope_dims))
    bias_p = _prompt_bias(group // 2)
    bias_s = _sample_bias(past_len, buf, t, group // 2)

    fox_w_in_b, fox_w_out_b, swa_w_in_b, swa_w_out_b, w_up_b, w_down_b = _to_bf16(
        fox_w_in, fox_w_out, swa_w_in, swa_w_out, ffn_w_up, ffn_w_down)

    outs = {k: [] for k in ("fkp", "fvp", "flp", "fks", "fvs", "fls", "skp", "svp", "sks", "svs")}
    for i in range(depth):
        mod_p, mod_s = split_mods(i)
        j = i // 2
        if i % 2 == 0:
            w_f = jnp.pad(fox_w_in_b[j, :, 3 * d:], ((0, 0), (0, LANES - nh_fox)))
            b_f = row(fox_b_f[j])
            qh, kh, vh, k32, v32, lf, n2 = _fox_in(
                xp, row(norm_mix_g[i]), mod_p[0], mod_p[1], fox_w_in_b, w_f, b_f,
                layer=j, tm=tm_fox, tpb=s // tm_fox, head_major=True, nh=nh_fox, hd=hd_fox)
            lft = lf.reshape(bp, s, nh_fox).transpose(0, 2, 1)
            c = _cumsum_rows(lft.reshape(-1, LANES), s // LANES).reshape(bp * nh_fox, s)
            qmax, kmax = _head_norm_bounds(n2, bp, nh_fox, min(FOX_IN_COLS, d) // hd_fox)
            first = _fox_first_blocks(c, qmax, kmax, tk)
            op = _fox_attn(qh, kh, vh, c, first, tk=tk, nsub=tq // tk)
            xp = _proj_res(op.reshape(mp, d), fox_w_out_b, xp, mod_p[2], layer=j, tm=tm_p, tpb=tpb_p)
            outs["fkp"].append(k32.reshape(bp, s, nh_fox, hd_fox))
            outs["fvp"].append(v32.reshape(bp, s, nh_fox, hd_fox))
            outs["flp"].append(lf.reshape(bp, s, nh_fox))
            qs_, k32s, v32s, lfs = _fox_in(xs, row(norm_mix_g[i]), mod_s[0], mod_s[1], fox_w_in_b, w_f, b_f,
                                           layer=j, tm=ms, tpb=1, head_major=False, nh=nh_fox, hd=hd_fox)
            lf_all = jnp.concatenate([cache_fox_logf[j], lfs.reshape(bs, t, nh_fox)], axis=1)
            lc = LANES * int(2 ** np.ceil(np.log2(-(-(past_len + t) // LANES))))
            lf_all = jnp.pad(lf_all.transpose(0, 2, 1), ((0, 0), (0, 0), (0, lc - past_len - t)))
            c_s = _cumsum_rows(lf_all.reshape(-1, LANES), lc // LANES).reshape(bs, nh_fox, lc)
            os_ = _fox_sample_attn(qs_, cache_fox_k, cache_fox_v, k32s, v32s, c_s, op, layer=j, t=t)
            xs = _proj_res(os_, fox_w_out_b, xs, mod_s[2], layer=j, tm=ms, tpb=1)
            outs["fks"].append(k32s.reshape(bs, t, nh_fox, hd_fox))
            outs["fvs"].append(v32s.reshape(bs, t, nh_fox, hd_fox))
            outs["fls"].append(lfs.reshape(bs, t, nh_fox))
        else:
            q, kk, vv, k32, v32 = _swa_in(xp, row(norm_mix_g[i]), mod_p[0], mod_p[1], swa_w_in_b, tabs_p,
                                          layer=j, tm=tm_p, tpb=tpb_p, kvd=kvd, hd=hd_swa, rope_dims=rope_dims)
            op = _swa_attn(q, kk, vv, bias_p, _sink_rows(swa_sinks[j], kv, group, 2 * CHUNK),
                           nb=bp, s=s, kv=kv, group=group, hd=hd_swa)
            xp = _proj_res(op, swa_w_out_b, xp, mod_p[2], layer=j, tm=tm_p, tpb=tpb_p)
            outs["skp"].append(k32.reshape(bp, s, kvd)[:, -buf:].reshape(bp, buf, kv, hd_swa))
            outs["svp"].append(v32.reshape(bp, s, kvd)[:, -buf:].reshape(bp, buf, kv, hd_swa))
            q, kk, vv, k32, v32 = _swa_in(xs, row(norm_mix_g[i]), mod_s[0], mod_s[1], swa_w_in_b, tabs_s,
                                          layer=j, tm=ms, tpb=1, kvd=kvd, hd=hd_swa, rope_dims=rope_dims)
            os_ = _swa_sample_attn(q, cache_swa_k[j].reshape(bs, buf, kvd), cache_swa_v[j].reshape(bs, buf, kvd),
                                   kk, vv, bias_s, _sink_rows(swa_sinks[j], kv, group, t),
                                   kv=kv, group=group, hd=hd_swa, t=t)
            xs = _proj_res(os_, swa_w_out_b, xs, mod_s[2], layer=j, tm=ms, tpb=1)
            k_all = jnp.concatenate([cache_swa_k[j], k32.reshape(bs, t, kv, hd_swa)], axis=1)
            v_all = jnp.concatenate([cache_swa_v[j], v32.reshape(bs, t, kv, hd_swa)], axis=1)
            outs["sks"].append(k_all[:, -buf:])
            outs["svs"].append(v_all[:, -buf:])
        fg = row(final_g) if i == depth - 1 else None
        xp = _ffn(xp, row(norm_ffn_g[i]), mod_p[3], mod_p[4], mod_p[5], w_up_b, w_down_b, fg,
                  layer=i, tm=tm_p, tpb=tpb_p)
        xs = _ffn(xs, row(norm_ffn_g[i]), mod_s[3], mod_s[4], mod_s[5], w_up_b, w_down_b, fg,
                  layer=i, tm=ms, tpb=1)

    st = lambda k: jnp.stack(outs[k])
    return (xp.reshape(bp, s, d), xs.reshape(bs, t, d),
            st("fkp"), st("fvp"), st("flp"), st("fks"), st("fvs"), st("fls"),
            st("skp"), st("svp"), st("sks"), st("svs"))
```
